```python
import math
import jax, jax.numpy as jnp
from jax import lax
import numpy as np

D_MODEL = 1024
BATCH = 4
SEQ = 4096
DEPTH = 1
DEC_BATCH = 128
DEC_SEQ = 4
PAST_LEN = 8192
PAGE_SIZE = 128

HEAD_DIM = 64
A_Q_HEADS = 8
A_KV_HEADS = 2
A_GROUP = A_Q_HEADS // A_KV_HEADS
A_WINDOW = 128
A_DILATION = 1
B_GROUPS = ((128, 1), (512, 4), (2048, 16))
N_B_GROUPS = 3
B_HEADS = 8
A_WIDTH = A_Q_HEADS * HEAD_DIM
A_KV_WIDTH = A_KV_HEADS * HEAD_DIM
B_WIDTH = B_HEADS * HEAD_DIM
B_QKV_WIDTH = N_B_GROUPS * B_HEADS * HEAD_DIM
H_TOTAL = A_Q_HEADS + N_B_GROUPS * B_HEADS
N_BUCKETS = 32
MAX_DISTANCE = 2048
EPS = 1e-6
NEG_INF = -1e30
Q_SCALE = HEAD_DIM ** -0.5
IN_SPLITS = (A_WIDTH, A_KV_WIDTH, A_KV_WIDTH, A_WIDTH, B_QKV_WIDTH, B_QKV_WIDTH, B_QKV_WIDTH, B_WIDTH, D_MODEL, D_MODEL)
C_IN = A_WIDTH + 2 * A_KV_WIDTH + A_WIDTH + 3 * B_QKV_WIDTH + B_WIDTH + 2 * D_MODEL

kernel_name = "hybrid_swa_sink_dilated_gated_merge_step"


def rmsnorm(x, g):
    x32 = x.astype(jnp.float32)
    y = x32 * lax.rsqrt(jnp.mean(x32 * x32, axis=-1, keepdims=True) + EPS)
    return (y * g.astype(jnp.float32)).astype(x.dtype)


def qk_norm(a, g, scale=1.0):
    a32 = a.astype(jnp.float32)
    y = a32 * lax.rsqrt(jnp.mean(a32 * a32, axis=-1, keepdims=True) + EPS)
    return (y * g.astype(jnp.float32) * scale).astype(a.dtype)


def t5_bucket(dist):
    max_exact = N_BUCKETS // 2
    d = jnp.maximum(dist, 0)
    df = jnp.maximum(d, 1).astype(jnp.float32)
    large = max_exact + (jnp.log(df / max_exact) / math.log(MAX_DISTANCE / max_exact)
                         * (N_BUCKETS - max_exact)).astype(jnp.int32)
    large = jnp.minimum(large, N_BUCKETS - 1)
    return jnp.where(d < max_exact, d, large)


def softmax_stats(s, mask, sink):
    s = jnp.where(mask, s, NEG_INF)
    m = jnp.max(s, axis=-1, keepdims=True)
    if sink is not None:
        m = jnp.maximum(m, sink)
    p = jnp.exp(s - m)
    l = jnp.sum(p, axis=-1, keepdims=True)
    if sink is not None:
        l = l + jnp.exp(sink - m)
    return p, l, m


def banded_window_attention(q, k, v, dilation, n_keys, bias_table, sink):
    N, S, Hk, G, Dh = q.shape
    d = dilation
    M = S // d
    blk = n_keys
    nb = -(-M // blk)
    Mp = nb * blk

    def to_blocks(a):
        a = a.reshape((N, M, d) + a.shape[2:])
        a = jnp.moveaxis(a, 2, 1).reshape((N * d, M) + a.shape[3:])
        a = jnp.pad(a, ((0, 0), (0, Mp - M)) + ((0, 0),) * (a.ndim - 2))
        return a.reshape((N * d, nb, blk) + a.shape[2:])

    def with_prev(a):
        prev = jnp.pad(a, ((0, 0), (1, 0)) + ((0, 0),) * (a.ndim - 2))[:, :-1]
        return jnp.concatenate([prev, a], axis=2)

    qb = to_blocks(q)
    kc = with_prev(to_blocks(k))
    vc = with_prev(to_blocks(v))
    s = jnp.einsum('nbqhgd,nbchd->nbhgqc', qb, kc, preferred_element_type=jnp.float32)
    qi = jnp.arange(blk)[:, None]
    ci = jnp.arange(2 * blk)[None, :]
    delta = qi + blk - ci
    band = (delta >= 0) & (delta < n_keys)
    key_pos = (jnp.arange(nb)[:, None, None] - 1) * blk + ci[None]
    mask = band[None] & (key_pos >= 0)
    bias = bias_table[t5_bucket(delta * d)]
    bias = jnp.transpose(bias, (2, 0, 1)).reshape(Hk, G, blk, 2 * blk).astype(jnp.float32)
    p, l, m = softmax_stats(s + bias, mask[None, :, None, None], sink)
    o = jnp.einsum('nbhgqc,nbchd->nbqhgd', p, vc.astype(jnp.float32))
    o = o / jnp.moveaxis(l[..., 0], -1, 2)[..., None]
    lse = jnp.moveaxis((m + jnp.log(l))[..., 0], -1, 2)

    def from_blocks(a):
        a = a.reshape((N * d, Mp) + a.shape[3:])[:, :M]
        a = a.reshape((N, d, M) + a.shape[2:])
        return jnp.moveaxis(a, 1, 2).reshape((N, S) + a.shape[3:])

    return from_blocks(o).astype(q.dtype), from_blocks(lse)


def gathered_window_attention(q, k_buf, v_buf, k_new, v_new, dilation, n_keys, bias_table, sink):
    N, T, Hk, G, Dh = q.shape
    L = k_buf.shape[1]
    k_all = jnp.concatenate([k_buf, k_new], axis=1)
    v_all = jnp.concatenate([v_buf, v_new], axis=1)
    idx = L + jnp.arange(T)[:, None] - jnp.arange(n_keys)[None, :] * dilation
    valid = idx >= 0
    flat = jnp.maximum(idx, 0).reshape(-1)
    kg = jnp.take(k_all, flat, axis=1).reshape(N, T, n_keys, Hk, Dh)
    vg = jnp.take(v_all, flat, axis=1).reshape(N, T, n_keys, Hk, Dh)
    s = jnp.einsum('nthgd,ntjhd->nhgtj', q, kg, preferred_element_type=jnp.float32)
    bias = bias_table[t5_bucket(jnp.arange(n_keys) * dilation)]
    bias = bias.T.reshape(Hk, G, 1, n_keys).astype(jnp.float32)
    p, l, m = softmax_stats(s + bias, valid, sink)
    o = jnp.einsum('nhgtj,ntjhd->nthgd', p, vg.astype(jnp.float32))
    o = o / jnp.transpose(l, (0, 3, 1, 2, 4))
    lse = jnp.transpose((m + jnp.log(l))[..., 0], (0, 3, 1, 2))
    return o.astype(q.dtype), lse


def mixer_inputs(x, norm_g, w_in, q_gain_a, k_gain_a, q_gain_b, k_gain_b):
    Bn, Sn = x.shape[:2]
    h = rmsnorm(x, norm_g)
    proj = jnp.einsum('bsd,dc->bsc', h, w_in)
    offsets = [int(o) for o in np.cumsum(IN_SPLITS)[:-1]]
    qa, ka, va, ga, qb, kb, vb, gb, ma, mb = jnp.split(proj, offsets, axis=-1)
    qa = qk_norm(qa.reshape(Bn, Sn, A_KV_HEADS, A_GROUP, HEAD_DIM), q_gain_a, Q_SCALE)
    ka = qk_norm(ka.reshape(Bn, Sn, A_KV_HEADS, HEAD_DIM), k_gain_a)
    va = va.reshape(Bn, Sn, A_KV_HEADS, HEAD_DIM)
    qb = qk_norm(qb.reshape(Bn, Sn, N_B_GROUPS, B_HEADS, HEAD_DIM), q_gain_b[:, None, :], Q_SCALE)
    kb = qk_norm(kb.reshape(Bn, Sn, N_B_GROUPS, B_HEADS, HEAD_DIM), k_gain_b[:, None, :])
    vb = vb.reshape(Bn, Sn, N_B_GROUPS, B_HEADS, HEAD_DIM)
    return qa, ka, va, ga, qb, kb, vb, gb, ma, mb


def combine_dilations(outs, lses):
    w = jax.nn.softmax(jnp.stack(lses, axis=0), axis=0)[..., None]
    o = jnp.sum(w * jnp.stack(outs, axis=0).astype(jnp.float32), axis=0)
    return o.astype(outs[0].dtype)


def mixer_output(x, o_a, o_b, ga, gb, ma, mb, w_up_a, w_up_b, w_out):
    Bn, Sn = x.shape[:2]
    ya = jnp.einsum('bsc,cd->bsd', o_a.reshape(Bn, Sn, A_WIDTH) * jax.nn.silu(ga), w_up_a)
    yb = jnp.einsum('bsc,cd->bsd', o_b.reshape(Bn, Sn, B_WIDTH) * jax.nn.silu(gb), w_up_b)
    merged = jax.nn.sigmoid(ma) * ya + jax.nn.sigmoid(mb) * yb
    return x + jnp.einsum('bsd,de->bse', merged, w_out)


def prompt_layer(x, rel_bias, norm_g, w_in, q_gain_a, k_gain_a, sinks_a, q_gain_b, k_gain_b, w_up_a, w_up_b, w_out):
    Sn = x.shape[1]
    qa, ka, va, ga, qb, kb, vb, gb, ma, mb = mixer_inputs(x, norm_g, w_in, q_gain_a, k_gain_a, q_gain_b, k_gain_b)
    sink = sinks_a.reshape(A_KV_HEADS, A_GROUP, 1, 1).astype(jnp.float32)
    o_a, _ = banded_window_attention(qa, ka, va, A_DILATION, A_WINDOW // A_DILATION,
                                     rel_bias[:, :A_Q_HEADS], sink)
    n_a = min(A_WINDOW, Sn)
    states = [jnp.stack([ka[:, Sn - n_a:], va[:, Sn - n_a:]], axis=2)]
    outs, lses = [], []
    for gi, (win, dil) in enumerate(B_GROUPS):
        c0 = A_Q_HEADS + gi * B_HEADS
        o, lse = banded_window_attention(qb[:, :, gi, :, None, :], kb[:, :, gi], vb[:, :, gi], dil, win // dil,
                                         rel_bias[:, c0:c0 + B_HEADS], None)
        outs.append(o)
        lses.append(lse)
        n_g = min(win, Sn)
        states.append(jnp.stack([kb[:, Sn - n_g:, gi], vb[:, Sn - n_g:, gi]], axis=2))
    o_b = combine_dilations(outs, lses)
    y = mixer_output(x, o_a, o_b, ga, gb, ma, mb, w_up_a, w_up_b, w_out)
    return y, states


def sample_layer(x, caches, rel_bias, norm_g, w_in, q_gain_a, k_gain_a, sinks_a, q_gain_b, k_gain_b, w_up_a, w_up_b, w_out):
    qa, ka, va, ga, qb, kb, vb, gb, ma, mb = mixer_inputs(x, norm_g, w_in, q_gain_a, k_gain_a, q_gain_b, k_gain_b)
    sink = sinks_a.reshape(A_KV_HEADS, A_GROUP, 1, 1).astype(jnp.float32)
    cache_a = caches[0]
    o_a, _ = gathered_window_attention(qa, cache_a[:, :, 0], cache_a[:, :, 1], ka, va, A_DILATION,
                                       A_WINDOW // A_DILATION, rel_bias[:, :A_Q_HEADS], sink)
    states = [jnp.stack([ka, va], axis=2)]
    outs, lses = [], []
    for gi, (win, dil) in enumerate(B_GROUPS):
        c0 = A_Q_HEADS + gi * B_HEADS
        cache_g = caches[1 + gi]
        o, lse = gathered_window_attention(qb[:, :, gi, :, None, :], cache_g[:, :, 0], cache_g[:, :, 1],
                                           kb[:, :, gi], vb[:, :, gi], dil, win // dil,
                                           rel_bias[:, c0:c0 + B_HEADS], None)
        outs.append(o)
        lses.append(lse)
        states.append(jnp.stack([kb[:, :, gi], vb[:, :, gi]], axis=2))
    o_b = combine_dilations(outs, lses)
    y = mixer_output(x, o_a, o_b, ga, gb, ma, mb, w_up_a, w_up_b, w_out)
    return y, states


def setup_inputs(seed: int = 0) -> dict:
    key = jax.random.key(seed)
    ks = jax.random.split(key, 20)
    f32 = jnp.float32
    la = min(A_WINDOW, PAST_LEN)
    l1 = min(B_GROUPS[0][0], PAST_LEN)
    l2 = min(B_GROUPS[1][0], PAST_LEN)
    l3 = min(B_GROUPS[2][0], PAST_LEN)
    return {
        "x_prompt": jax.random.normal(ks[0], (BATCH, SEQ, D_MODEL), f32),
        "x_sample": jax.random.normal(ks[1], (DEC_BATCH, DEC_SEQ, D_MODEL), f32),
        "cache_a_kv": jax.random.normal(ks[2], (DEPTH, DEC_BATCH, la, 2, A_KV_HEADS, HEAD_DIM), f32),
        "cache_b1_kv": jax.random.normal(ks[3], (DEPTH, DEC_BATCH, l1, 2, B_HEADS, HEAD_DIM), f32),
        "cache_b2_kv": jax.random.normal(ks[4], (DEPTH, DEC_BATCH, l2, 2, B_HEADS, HEAD_DIM), f32),
        "cache_b3_kv": jax.random.normal(ks[5], (DEPTH, DEC_BATCH, l3, 2, B_HEADS, HEAD_DIM), f32),
        "rel_bias": 0.5 * jax.random.normal(ks[6], (N_BUCKETS, H_TOTAL), f32),
        "norm_gain": 1.0 + 0.1 * jax.random.normal(ks[7], (DEPTH, D_MODEL), f32),
        "w_in": jax.random.normal(ks[8], (DEPTH, D_MODEL, C_IN), f32) * D_MODEL ** -0.5,
        "q_gain_a": 1.0 + 0.1 * jax.random.normal(ks[9], (DEPTH, HEAD_DIM), f32),
        "k_gain_a": 1.0 + 0.1 * jax.random.normal(ks[10], (DEPTH, HEAD_DIM), f32),
        "sinks_a": 0.5 * jax.random.normal(ks[11], (DEPTH, A_Q_HEADS), f32),
        "q_gain_b": 1.0 + 0.1 * jax.random.normal(ks[12], (DEPTH, N_B_GROUPS, HEAD_DIM), f32),
        "k_gain_b": 1.0 + 0.1 * jax.random.normal(ks[13], (DEPTH, N_B_GROUPS, HEAD_DIM), f32),
        "w_up_a": jax.random.normal(ks[14], (DEPTH, A_WIDTH, D_MODEL), f32) * A_WIDTH ** -0.5,
        "w_up_b": jax.random.normal(ks[15], (DEPTH, B_WIDTH, D_MODEL), f32) * B_WIDTH ** -0.5,
        "w_out": jax.random.normal(ks[16], (DEPTH, D_MODEL, D_MODEL), f32) * D_MODEL ** -0.5,
    }


def reference(x_prompt, x_sample, cache_a_kv, cache_b1_kv, cache_b2_kv, cache_b3_kv, rel_bias, norm_gain, w_in,
              q_gain_a, k_gain_a, sinks_a, q_gain_b, k_gain_b, w_up_a, w_up_b, w_out):
    yp = x_prompt
    ys = x_sample
    p_states = []
    s_states = []
    for layer in range(DEPTH):
        params = (norm_gain[layer], w_in[layer], q_gain_a[layer], k_gain_a[layer], sinks_a[layer],
                  q_gain_b[layer], k_gain_b[layer], w_up_a[layer], w_up_b[layer], w_out[layer])
        yp, ps = prompt_layer(yp, rel_bias, *params)
        caches = (cache_a_kv[layer], cache_b1_kv[layer], cache_b2_kv[layer], cache_b3_kv[layer])
        ys, ss = sample_layer(ys, caches, rel_bias, *params)
        p_states.append(ps)
        s_states.append(ss)
    new_prompt_a_kv = jnp.stack([st[0] for st in p_states])
    new_prompt_b1_kv = jnp.stack([st[1] for st in p_states])
    new_prompt_b2_kv = jnp.stack([st[2] for st in p_states])
    new_prompt_b3_kv = jnp.stack([st[3] for st in p_states])
    new_sample_a_kv = jnp.stack([st[0] for st in s_states])
    new_sample_b1_kv = jnp.stack([st[1] for st in s_states])
    new_sample_b2_kv = jnp.stack([st[2] for st in s_states])
    new_sample_b3_kv = jnp.stack([st[3] for st in s_states])
    return (yp, ys, new_prompt_a_kv, new_prompt_b1_kv, new_prompt_b2_kv, new_prompt_b3_kv,
            new_sample_a_kv, new_sample_b1_kv, new_sample_b2_kv, new_sample_b3_kv)
```

```python
import functools
import math

import numpy as np
import jax
import jax.numpy as jnp
from jax import lax
from jax.experimental import pallas as pl
from jax.experimental.pallas import tpu as pltpu

D_MODEL = 1024
HEAD_DIM = 64
A_Q_HEADS = 8
A_KV_HEADS = 2
A_GROUP = A_Q_HEADS // A_KV_HEADS
A_WINDOW = 128
B_GROUPS = ((128, 1), (512, 4), (2048, 16))
N_B_GROUPS = 3
B_HEADS = 8
N_KEYS = 128
A_WIDTH = A_Q_HEADS * HEAD_DIM
A_KV_WIDTH = A_KV_HEADS * HEAD_DIM
B_WIDTH = B_HEADS * HEAD_DIM
N_BUCKETS = 32
MAX_DISTANCE = 2048
EPS = 1e-6
NEG_INF = -1e30
Q_SCALE = HEAD_DIM ** -0.5

LANES = 128
MXU_COLS = 256
VMEM_LIMIT = 56 * 1024 * 1024

A_HEAD_ORDER = (0, 4, 1, 5, 2, 6, 3, 7)

Q_COLS = A_WIDTH + N_B_GROUPS * B_WIDTH
KV_COLS = A_KV_WIDTH + N_B_GROUPS * B_WIDTH
G_COLS = A_WIDTH + B_WIDTH + 2 * D_MODEL


def _t5_bucket_np(dist):
    max_exact = N_BUCKETS // 2
    d = np.maximum(dist, 0)
    df = np.maximum(d, 1).astype(np.float32)
    large = max_exact + (np.log(df / np.float32(max_exact)) / np.float32(math.log(MAX_DISTANCE / max_exact))
                         * np.float32(N_BUCKETS - max_exact)).astype(np.int32)
    large = np.minimum(large, N_BUCKETS - 1)
    return np.where(d < max_exact, d, large)


def _state_plan(n_rows, seq, tm):
    tpb = seq // tm
    r = min(n_rows, tm)
    nblk = max(n_rows // tm, 1)
    return tpb, r, nblk, tpb - nblk


def _inproj_kernel(x_ref, ng_ref, w_ref, gq_ref, gk_ref, ones_ref,
                   q_ref, k_ref, v_ref, g_ref, sa_ref, sb1_ref, sb2_ref, sb3_ref,
                   *, tm, seq, state_rows):
    i = pl.program_id(0)
    x = x_ref[...]
    ms = jnp.mean(x * x, axis=-1, keepdims=True)
    h = (x * lax.rsqrt(ms + EPS) * ng_ref[...]).astype(jnp.bfloat16)
    ones = ones_ref[...]

    def proj(c0, width):
        return jnp.dot(h, w_ref[:, c0:c0 + width], preferred_element_type=jnp.float32)

    def headnorm(a, gain):
        sq = (a * a).astype(jnp.bfloat16)
        parts = []
        for c in range(0, a.shape[1], MXU_COLS):
            w = min(MXU_COLS, a.shape[1] - c)
            parts.append(jnp.dot(sq[:, c:c + w], ones[:w, :w], preferred_element_type=jnp.float32))
        ss = parts[0] if len(parts) == 1 else jnp.concatenate(parts, axis=1)
        return a * lax.rsqrt(ss + EPS) * gain

    for c0 in range(0, Q_COLS, MXU_COLS):
        a = proj(c0, MXU_COLS)
        q_ref[:, c0:c0 + MXU_COLS] = headnorm(a, gq_ref[:, c0:c0 + MXU_COLS]).astype(jnp.bfloat16)

    state_refs = (sa_ref, sb1_ref, sb2_ref, sb3_ref)
    kv_widths = (A_KV_WIDTH, B_WIDTH, B_WIDTH, B_WIDTH)
    j = i % (seq // tm)
    off = 0
    for g in range(4):
        wd = kv_widths[g]
        kn = headnorm(proj(Q_COLS + off, wd), gk_ref[:, off:off + wd])
        vv = proj(Q_COLS + KV_COLS + off, wd)
        k_ref[:, off:off + wd] = kn.astype(jnp.bfloat16)
        v_ref[:, off:off + wd] = vv.astype(jnp.bfloat16)
        _, r, _, j0 = _state_plan(state_rows[g], seq, tm)
        sref = state_refs[g]

        @pl.when(j >= j0)
        def _():
            sref[:, 0:wd] = kn[tm - r:, :]
            sref[:, wd:2 * wd] = vv[tm - r:, :]
        off += wd

    g0 = Q_COLS + 2 * KV_COLS
    for c0 in range(0, G_COLS, MXU_COLS):
        a = proj(g0 + c0, MXU_COLS)
        sg = 1.0 / (1.0 + jnp.exp(-a))
        if c0 < A_WIDTH + B_WIDTH:
            sg = a * sg
        g_ref[:, c0:c0 + MXU_COLS] = sg.astype(jnp.bfloat16)


def _inproj(x2d, ng, w_perm, gq, gk, ones, *, seq, state_rows, tm):
    t = x2d.shape[0]
    nb = t // seq
    tpb = seq // tm
    c_in = w_perm.shape[1]

    def state_spec(g, width):
        _, r, nblk, j0 = _state_plan(state_rows[g], seq, tm)
        return pl.BlockSpec((r, width), lambda i: ((i // tpb) * nblk + jnp.maximum(i % tpb - j0, 0), 0))

    const = lambda i: (0, 0)
    row = lambda i: (i, 0)
    out_shape = (
        jax.ShapeDtypeStruct((t, Q_COLS), jnp.bfloat16),
        jax.ShapeDtypeStruct((t, KV_COLS), jnp.bfloat16),
        jax.ShapeDtypeStruct((t, KV_COLS), jnp.bfloat16),
        jax.ShapeDtypeStruct((t, G_COLS), jnp.bfloat16),
        jax.ShapeDtypeStruct((nb * state_rows[0], 2 * A_KV_WIDTH), jnp.float32),
        jax.ShapeDtypeStruct((nb * state_rows[1], 2 * B_WIDTH), jnp.float32),
        jax.ShapeDtypeStruct((nb * state_rows[2], 2 * B_WIDTH), jnp.float32),
        jax.ShapeDtypeStruct((nb * state_rows[3], 2 * B_WIDTH), jnp.float32),
    )
    return pl.pallas_call(
        functools.partial(_inproj_kernel, tm=tm, seq=seq, state_rows=state_rows),
        grid=(t // tm,),
        in_specs=[
            pl.BlockSpec((tm, D_MODEL), row),
            pl.BlockSpec((1, D_MODEL), const),
            pl.BlockSpec((D_MODEL, c_in), const, pipeline_mode=pl.Buffered(1)),
            pl.BlockSpec((1, Q_COLS), const),
            pl.BlockSpec((1, KV_COLS), const),
            pl.BlockSpec((MXU_COLS, MXU_COLS), const),
        ],
        out_specs=(
            pl.BlockSpec((tm, Q_COLS), row),
            pl.BlockSpec((tm, KV_COLS), row),
            pl.BlockSpec((tm, KV_COLS), row),
            pl.BlockSpec((tm, G_COLS), row),
            state_spec(0, 2 * A_KV_WIDTH),
            state_spec(1, 2 * B_WIDTH),
            state_spec(2, 2 * B_WIDTH),
            state_spec(3, 2 * B_WIDTH),
        ),
        out_shape=out_shape,
        compiler_params=pltpu.CompilerParams(
            dimension_semantics=("arbitrary",), vmem_limit_bytes=VMEM_LIMIT),
        name="inproj",
    )(x2d, ng, w_perm, gq, gk, ones)


def _band_attn_kernel(q_ref, k_ref, kp_ref, v_ref, vp_ref, bias_ref, sink_ref, *rest,
                      tq, shared_kv, has_sink, want_lse):
    if want_lse:
        o_ref, lse_ref, kbuf, vbuf = rest
    else:
        o_ref, kbuf, vbuf = rest
        lse_ref = None
    blk = N_KEYS
    i = pl.program_id(2)
    kbuf[0:blk, :] = kp_ref[0]
    kbuf[blk:, :] = k_ref[0]
    vbuf[0:blk, :] = vp_ref[0]
    vbuf[blk:, :] = v_ref[0]

    row = lax.broadcasted_iota(jnp.int32, (blk, blk), 0)
    lane = lax.broadcasted_iota(jnp.int32, (blk, blk), 1)
    ahead = lane - row
    tri = ahead <= 0
    left = lane < HEAD_DIM
    side_q = (left.astype(jnp.bfloat16), jnp.logical_not(left).astype(jnp.bfloat16))
    left_kv = lax.broadcasted_iota(jnp.int32, (2 * blk, LANES), 1) < HEAD_DIM
    side_kv = (left_kv.astype(jnp.bfloat16), jnp.logical_not(left_kv).astype(jnp.bfloat16))
    n_pairs = q_ref.shape[2] // LANES

    def block(j, carry):
        r0 = pl.multiple_of(j * blk, blk)
        reach = jnp.where(jnp.logical_and(i == 0, j == 0), 0, blk)
        valid = ahead <= reach
        for p in range(n_pairs):
            kcol = 0 if shared_kv else p * LANES
            qp = q_ref[0, pl.ds(r0, blk), p * LANES:(p + 1) * LANES]
            kc = kbuf[pl.ds(r0, 2 * blk), kcol:kcol + LANES]
            vc = vbuf[pl.ds(r0, 2 * blk), kcol:kcol + LANES]
            o_pair = None
            lse_pair = None
            for side in range(2):
                hidx = 2 * p + side
                qm = qp * side_q[side]
                s2 = lax.dot_general(qm, kc, (((1,), (1,)), ((), ())),
                                     preferred_element_type=jnp.float32)
                s = jnp.where(tri, s2[:, blk:], s2[:, :blk]) + bias_ref[hidx]
                s = jnp.where(valid, s, NEG_INF)
                m = jnp.max(s, axis=-1, keepdims=True)
                if has_sink:
                    sk = sink_ref[hidx:hidx + 1, 0:1]
                    m = jnp.maximum(m, sk)
                pe = jnp.exp(s - m)
                l = jnp.sum(pe, axis=-1, keepdims=True)
                if has_sink:
                    l = l + jnp.exp(sk - m)
                p_prev = jnp.where(tri, 0.0, pe).astype(jnp.bfloat16)
                p_cur = jnp.where(tri, pe, 0.0).astype(jnp.bfloat16)
                p2 = jnp.concatenate([p_prev, p_cur], axis=1)
                vm = vc * side_kv[side]
                o_side = jnp.dot(p2, vm, preferred_element_type=jnp.float32) / l
                o_pair = o_side if o_pair is None else o_pair + o_side
                if want_lse:
                    lse_side = jnp.broadcast_to(m + jnp.log(l), (blk, LANES))
                    lse_pair = lse_side if lse_pair is None else jnp.where(left, lse_pair, lse_side)
            o_ref[0, pl.ds(r0, blk), p * LANES:(p + 1) * LANES] = o_pair.astype(o_ref.dtype)
            if want_lse:
                lse_ref[0, pl.ds(r0, blk), p * LANES:(p + 1) * LANES] = lse_pair
        return carry

    lax.fori_loop(0, tq // blk, block, 0)


def _band_attention(q, k, v, bias, sink, *, dil, shared_kv, has_sink, want_lse):
    nb, m, _ = q.shape
    wq = q.shape[2] // dil
    wkv = k.shape[2] // dil
    tq = min(m, 1024)
    nblk_prev = tq // N_KEYS
    cur = lambda b, r, i: (b, i, r)
    prev = lambda b, r, i: (b, jnp.maximum(i * nblk_prev - 1, 0), r)
    out_shape = [jax.ShapeDtypeStruct(q.shape, jnp.bfloat16)]
    out_specs = [pl.BlockSpec((1, tq, wq), cur)]
    if want_lse:
        out_shape.append(jax.ShapeDtypeStruct(q.shape, jnp.float32))
        out_specs.append(pl.BlockSpec((1, tq, wq), cur))
    res = pl.pallas_call(
        functools.partial(_band_attn_kernel, tq=tq, shared_kv=shared_kv, has_sink=has_sink,
                          want_lse=want_lse),
        grid=(nb, dil, m // tq),
        in_specs=[
            pl.BlockSpec((1, tq, wq), cur),
            pl.BlockSpec((1, tq, wkv), cur),
            pl.BlockSpec((1, N_KEYS, wkv), prev),
            pl.BlockSpec((1, tq, wkv), cur),
            pl.BlockSpec((1, N_KEYS, wkv), prev),
            pl.BlockSpec(bias.shape, lambda b, r, i: (0, 0, 0)),
            pl.BlockSpec(sink.shape, lambda b, r, i: (0, 0)),
        ],
        out_specs=out_specs,
        out_shape=out_shape,
        scratch_shapes=[pltpu.VMEM((tq + N_KEYS, wkv), jnp.bfloat16),
                        pltpu.VMEM((tq + N_KEYS, wkv), jnp.bfloat16)],
        compiler_params=pltpu.CompilerParams(
            dimension_semantics=("arbitrary", "arbitrary", "arbitrary"), vmem_limit_bytes=VMEM_LIMIT),
        name="band_attn_d%d" % dil,
    )(q, k, k, v, v, bias, sink)
    return res


def _decode_kernel(q_ref, na_ref, nb1_ref, nb2_ref, nb3_ref, ca_ref, cb1_ref, cb2_ref, cb3_ref,
                   ta_ref, tb1_ref, tb2_ref, tb3_ref, tn_ref, sink_ref, oa_ref, ob_ref, *, dec_seq):
    t = dec_seq
    left8 = lax.broadcasted_iota(jnp.int32, (2 * t, LANES), 1) < HEAD_DIM
    top8 = lax.broadcasted_iota(jnp.int32, (2 * t, LANES), 0) < t
    own = (left8 == top8).astype(jnp.float32)
    left4 = lax.broadcasted_iota(jnp.int32, (t, LANES), 1) < HEAD_DIM
    nt = (((1,), (1,)), ((), ()))

    def q_rows(c0):
        qp = q_ref[0, :, c0:c0 + LANES].astype(jnp.float32)
        return jnp.concatenate([qp, qp], axis=0) * own

    def attend(qr, kt, vt, knew, vnew, tbl, tbl_new, sink):
        s_c = jnp.dot(qr.astype(jnp.bfloat16), kt, preferred_element_type=jnp.float32) + tbl
        s_n = [jnp.sum(qr * knew[u:u + 1, :], axis=-1, keepdims=True) + tbl_new[:, u:u + 1] for u in range(t)]
        m = functools.reduce(jnp.maximum, s_n, jnp.max(s_c, axis=-1, keepdims=True))
        if sink is not None:
            m = jnp.maximum(m, sink)
        pc = jnp.exp(s_c - m)
        pn = [jnp.exp(x - m) for x in s_n]
        l = functools.reduce(lambda a, b: a + b, pn, jnp.sum(pc, axis=-1, keepdims=True))
        if sink is not None:
            l = l + jnp.exp(sink - m)
        o = lax.dot_general(pc.astype(jnp.bfloat16), vt, nt, preferred_element_type=jnp.float32)
        for u in range(t):
            o = o + pn[u] * vnew[u:u + 1, :]
        return o / l, m + jnp.log(l)

    def fold(x8):
        return jnp.where(left4, x8[:t], x8[t:])

    kt = ca_ref[0, 0].reshape(LANES, ca_ref.shape[-1]).astype(jnp.bfloat16)
    vt = ca_ref[0, 1].reshape(LANES, ca_ref.shape[-1]).astype(jnp.bfloat16)
    knew = na_ref[0, :, 0:LANES]
    vnew = na_ref[0, :, LANES:2 * LANES]
    for p in range(A_Q_HEADS // 2):
        rows = slice(p * 2 * t, (p + 1) * 2 * t)
        o8, _ = attend(q_rows(p * LANES), kt, vt, knew, vnew, ta_ref[rows, :],
                       tn_ref[0, rows, :], sink_ref[rows, 0:1])
        oa_ref[0, :, p * LANES:(p + 1) * LANES] = fold(o8).astype(oa_ref.dtype)

    caches = (cb1_ref, cb2_ref, cb3_ref)
    news = (nb1_ref, nb2_ref, nb3_ref)
    tbls = (tb1_ref, tb2_ref, tb3_ref)
    for p in range(B_HEADS // 2):
        rows = slice(p * 2 * t, (p + 1) * 2 * t)
        outs, lses = [], []
        for g in range(N_B_GROUPS):
            c_ref = caches[g]
            kt = c_ref[0, 0, 2 * p:2 * p + 2].reshape(LANES, c_ref.shape[-1]).astype(jnp.bfloat16)
            vt = c_ref[0, 1, 2 * p:2 * p + 2].reshape(LANES, c_ref.shape[-1]).astype(jnp.bfloat16)
            knew = news[g][0, :, p * LANES:(p + 1) * LANES]
            vnew = news[g][0, :, B_WIDTH + p * LANES:B_WIDTH + (p + 1) * LANES]
            o8, lse8 = attend(q_rows(A_WIDTH + g * B_WIDTH + p * LANES), kt, vt, knew, vnew,
                              tbls[g][rows, :], tn_ref[1 + g, rows, :], None)
            outs.append(fold(o8))
            lses.append(fold(jnp.broadcast_to(lse8, (2 * t, LANES))))
        mx = jnp.maximum(jnp.maximum(lses[0], lses[1]), lses[2])
        es = [jnp.exp(x - mx) for x in lses]
        den = es[0] + es[1] + es[2]
        comb = (es[0] * outs[0] + es[1] * outs[1] + es[2] * outs[2]) / den
        ob_ref[0, :, p * LANES:(p + 1) * LANES] = comb.astype(ob_ref.dtype)


def _decode_attention(q, news, caches, tbls, tbl_new, sink_rows):
    n, t, _ = q.shape
    seq3 = lambda i: (i, 0, 0)
    seq5 = lambda i: (i, 0, 0, 0, 0)
    in_specs = [pl.BlockSpec((1, t, q.shape[2]), seq3)]
    in_specs += [pl.BlockSpec((1, t, a.shape[2]), seq3) for a in news]
    in_specs += [pl.BlockSpec((1,) + c.shape[1:], seq5) for c in caches]
    in_specs += [pl.BlockSpec(tb.shape, lambda i: (0, 0)) for tb in tbls]
    in_specs += [pl.BlockSpec(tbl_new.shape, lambda i: (0, 0, 0)),
                 pl.BlockSpec(sink_rows.shape, lambda i: (0, 0))]
    return pl.pallas_call(
        functools.partial(_decode_kernel, dec_seq=t),
        grid=(n,),
        in_specs=in_specs,
        out_specs=(pl.BlockSpec((1, t, A_WIDTH), seq3), pl.BlockSpec((1, t, B_WIDTH), seq3)),
        out_shape=(jax.ShapeDtypeStruct((n, t, A_WIDTH), jnp.float32),
                   jax.ShapeDtypeStruct((n, t, B_WIDTH), jnp.float32)),
        compiler_params=pltpu.CompilerParams(
            dimension_semantics=("arbitrary",), vmem_limit_bytes=VMEM_LIMIT),
        name="decode_attn",
    )(q, *news, *caches, *tbls, tbl_new, sink_rows)


def _out_kernel(*refs, n_groups):
    x_ref, oa_ref, g_ref = refs[0:3]
    ob_refs = refs[3:3 + n_groups]
    lse_refs = refs[3 + n_groups:3 + 2 * n_groups] if n_groups > 1 else ()
    wa_ref, wb_ref, wo_ref, y_ref = refs[-4:]
    if n_groups > 1:
        lses = [r[...] for r in lse_refs]
        mx = functools.reduce(jnp.maximum, lses)
        es = [jnp.exp(x - mx) for x in lses]
        den = functools.reduce(lambda a, b: a + b, es)
        num = functools.reduce(lambda a, b: a + b,
                               [e * r[...].astype(jnp.float32) for e, r in zip(es, ob_refs)])
        ob = num / den
    else:
        ob = ob_refs[0][...].astype(jnp.float32)
    oa = oa_ref[...].astype(jnp.float32)
    ga = g_ref[:, 0:A_WIDTH].astype(jnp.float32)
    gb = g_ref[:, A_WIDTH:A_WIDTH + B_WIDTH].astype(jnp.float32)
    ya = jnp.dot((oa * ga).astype(jnp.bfloat16), wa_ref[...], preferred_element_type=jnp.float32)
    yb = jnp.dot((ob * gb).astype(jnp.bfloat16), wb_ref[...], preferred_element_type=jnp.float32)
    m0 = A_WIDTH + B_WIDTH
    ma = g_ref[:, m0:m0 + D_MODEL].astype(jnp.float32)
    mb = g_ref[:, m0 + D_MODEL:m0 + 2 * D_MODEL].astype(jnp.float32)
    merged = (ma * ya + mb * yb).astype(jnp.bfloat16)
    y_ref[...] = x_ref[...] + jnp.dot(merged, wo_ref[...], preferred_element_type=jnp.float32)


def _out_proj(x2d, oa, gates, obs, lses, wa, wb, wo, *, tm):
    t = x2d.shape[0]
    row = lambda i: (i, 0)
    const = lambda i: (0, 0)
    n_groups = len(obs)
    in_specs = [pl.BlockSpec((tm, D_MODEL), row), pl.BlockSpec((tm, A_WIDTH), row),
                pl.BlockSpec((tm, G_COLS), row)]
    in_specs += [pl.BlockSpec((tm, B_WIDTH), row) for _ in obs]
    in_specs += [pl.BlockSpec((tm, B_WIDTH), row) for _ in lses]
    in_specs += [pl.BlockSpec(wa.shape, const), pl.BlockSpec(wb.shape, const), pl.BlockSpec(wo.shape, const)]
    return pl.pallas_call(
        functools.partial(_out_kernel, n_groups=n_groups),
        grid=(t // tm,),
        in_specs=in_specs,
        out_specs=pl.BlockSpec((tm, D_MODEL), row),
        out_shape=jax.ShapeDtypeStruct((t, D_MODEL), jnp.float32),
        compiler_params=pltpu.CompilerParams(
            dimension_semantics=("arbitrary",), vmem_limit_bytes=VMEM_LIMIT),
        name="out_proj",
    )(x2d, oa, gates, *obs, *lses, wa, wb, wo)


def _prep_params(rel_bias, w_in, q_gain_a, k_gain_a, sinks_a, q_gain_b, k_gain_b, w_up_a, w_up_b):
    offs = np.cumsum((0, A_WIDTH, A_KV_WIDTH, A_KV_WIDTH, A_WIDTH, 3 * B_WIDTH, 3 * B_WIDTH, 3 * B_WIDTH,
                      B_WIDTH, D_MODEL, D_MODEL))
    seg = lambda s: w_in[:, int(offs[s]):int(offs[s + 1])]

    def perm_heads(w2d):
        return jnp.concatenate([w2d[:, h * HEAD_DIM:(h + 1) * HEAD_DIM] for h in A_HEAD_ORDER], axis=1)

    w_perm = jnp.concatenate(
        [perm_heads(seg(0)), seg(4), seg(1), seg(5), seg(2), seg(6), perm_heads(seg(3)), seg(7), seg(8), seg(9)],
        axis=1).astype(jnp.bfloat16)
    gq = jnp.concatenate([jnp.tile(q_gain_a, A_Q_HEADS)] +
                         [jnp.tile(q_gain_b[g], B_HEADS) for g in range(N_B_GROUPS)])[None, :] * Q_SCALE
    gk = jnp.concatenate([jnp.tile(k_gain_a, A_KV_HEADS)] +
                         [jnp.tile(k_gain_b[g], B_HEADS) for g in range(N_B_GROUPS)])[None, :]
    hd = np.arange(MXU_COLS) // HEAD_DIM
    ones = jnp.asarray((hd[:, None] == hd[None, :]).astype(np.float32) / HEAD_DIM, jnp.bfloat16)
    wa = jnp.concatenate([w_up_a[h * HEAD_DIM:(h + 1) * HEAD_DIM] for h in A_HEAD_ORDER], axis=0)
    head_cols = [np.asarray(A_HEAD_ORDER)] + [A_Q_HEADS + g * B_HEADS + np.arange(B_HEADS)
                                              for g in range(N_B_GROUPS)]
    dils = (1,) + tuple(d for _, d in B_GROUPS)
    qi = np.arange(N_KEYS)[:, None]
    ci = np.arange(N_KEYS)[None, :]
    steps = (qi - ci) % N_KEYS
    band_bias = [jnp.transpose(rel_bias[_t5_bucket_np(steps * d)][:, :, cols], (2, 0, 1))
                 for d, cols in zip(dils, head_cols)]
    sink_perm = sinks_a[np.asarray(A_HEAD_ORDER)]
    return w_perm, gq, gk, ones, wa.astype(jnp.bfloat16), w_up_b.astype(jnp.bfloat16), band_bias, sink_perm, \
        head_cols, dils


def _decode_tables(rel_bias, sink_perm, head_cols, dils, cache_lens, dec_seq):
    t_idx = np.arange(dec_seq)
    tbls = []
    new_rows = []
    for d, cols, ln in zip(dils, head_cols, cache_lens):
        pos = np.arange(ln)
        back = ln + t_idx[:, None] - pos[None, :]
        ok = (back % d == 0) & (back // d >= 1) & (back // d < N_KEYS)
        bk = _t5_bucket_np(np.where(ok, back, 0))
        b = rel_bias[bk][:, :, cols]
        b = jnp.where(ok[:, :, None], b, NEG_INF)
        b = jnp.transpose(b, (2, 0, 1))
        tbls.append(b.reshape(len(cols) * dec_seq, ln))
        backn = t_idx[:, None] - t_idx[None, :]
        okn = (backn >= 0) & (backn % d == 0) & (backn // d < N_KEYS)
        bn = rel_bias[_t5_bucket_np(np.where(okn, backn, 0))][:, :, cols]
        bn = jnp.where(okn[:, :, None], bn, NEG_INF)
        new_rows.append(jnp.transpose(bn, (2, 0, 1)).reshape(len(cols) * dec_seq, dec_seq))
    tbl_new = jnp.stack(new_rows)
    sink_rows = jnp.broadcast_to(jnp.repeat(sink_perm, dec_seq)[:, None], (A_Q_HEADS * dec_seq, LANES))
    return tbls, tbl_new, sink_rows


def _to_classes(a, dil):
    nb, s, w = a.shape
    return a.reshape(nb, s // dil, dil * w)


def kernel(x_prompt, x_sample, cache_a_kv, cache_b1_kv, cache_b2_kv, cache_b3_kv, rel_bias, norm_gain, w_in,
           q_gain_a, k_gain_a, sinks_a, q_gain_b, k_gain_b, w_up_a, w_up_b, w_out):
    assert norm_gain.shape[0] == 1, "single layer"
    nb, seq, _ = x_prompt.shape
    n_dec, dec_seq, _ = x_sample.shape
    (w_perm, gq, gk, ones, wa, wb, band_bias, sink_perm, head_cols, dils) = _prep_params(
        rel_bias, w_in[0], q_gain_a[0], k_gain_a[0], sinks_a[0], q_gain_b[0], k_gain_b[0], w_up_a[0], w_up_b[0])
    wo = w_out[0].astype(jnp.bfloat16)
    ng = norm_gain
    windows = (A_WINDOW,) + tuple(w for w, _ in B_GROUPS)

    p_rows = tuple(min(w, seq) for w in windows)
    xp = x_prompt.reshape(nb * seq, D_MODEL)
    q, k, v, gates, sa, sb1, sb2, sb3 = _inproj(xp, ng, w_perm, gq, gk, ones, seq=seq, state_rows=p_rows, tm=256)
    q = q.reshape(nb, seq, Q_COLS)
    k = k.reshape(nb, seq, KV_COLS)
    v = v.reshape(nb, seq, KV_COLS)
    sink_tbl = jnp.broadcast_to(sink_perm[:, None], (A_Q_HEADS, LANES))
    (oa,) = _band_attention(q[:, :, :A_WIDTH], k[:, :, :A_KV_WIDTH], v[:, :, :A_KV_WIDTH], band_bias[0], sink_tbl,
                            dil=1, shared_kv=True, has_sink=True, want_lse=False)
    obs, lses = [], []
    for g in range(N_B_GROUPS):
        d = dils[1 + g]
        qg = _to_classes(q[:, :, A_WIDTH + g * B_WIDTH:A_WIDTH + (g + 1) * B_WIDTH], d)
        kg = _to_classes(k[:, :, A_KV_WIDTH + g * B_WIDTH:A_KV_WIDTH + (g + 1) * B_WIDTH], d)
        vg = _to_classes(v[:, :, A_KV_WIDTH + g * B_WIDTH:A_KV_WIDTH + (g + 1) * B_WIDTH], d)
        og, lg = _band_attention(qg, kg, vg, band_bias[1 + g], sink_tbl,
                                 dil=d, shared_kv=False, has_sink=False, want_lse=True)
        obs.append(og.reshape(nb * seq, B_WIDTH))
        lses.append(lg.reshape(nb * seq, B_WIDTH))
    y_prompt = _out_proj(xp, oa.reshape(nb * seq, A_WIDTH), gates, obs, lses, wa, wb, wo, tm=512)
    y_prompt = y_prompt.reshape(nb, seq, D_MODEL)

    def prompt_state(s2d, rows, heads):
        s5 = s2d.reshape(nb, rows, 2, heads, HEAD_DIM)
        return s5[None]

    new_prompt = (prompt_state(sa, p_rows[0], A_KV_HEADS), prompt_state(sb1, p_rows[1], B_HEADS),
                  prompt_state(sb2, p_rows[2], B_HEADS), prompt_state(sb3, p_rows[3], B_HEADS))

    t_dec = n_dec * dec_seq
    xs = x_sample.reshape(t_dec, D_MODEL)
    qs, _, _, gates_s, na, nb1, nb2, nb3 = _inproj(xs, ng, w_perm, gq, gk, ones, seq=t_dec,
                                                   state_rows=(t_dec,) * 4, tm=256)
    caches = [jnp.transpose(c[0], (0, 2, 3, 4, 1)) for c in (cache_a_kv, cache_b1_kv, cache_b2_kv, cache_b3_kv)]
    cache_lens = tuple(c.shape[-1] for c in caches)
    tbls, tbl_new, sink_rows = _decode_tables(rel_bias, sink_perm, head_cols, dils, cache_lens, dec_seq)
    news = [a.reshape(n_dec, dec_seq, a.shape[1]) for a in (na, nb1, nb2, nb3)]
    oa_s, ob_s = _decode_attention(qs.reshape(n_dec, dec_seq, Q_COLS), news, caches, tbls, tbl_new, sink_rows)
    y_sample = _out_proj(xs, oa_s.reshape(t_dec, A_WIDTH), gates_s, [ob_s.reshape(t_dec, B_WIDTH)], [],
                         wa, wb, wo, tm=t_dec)
    y_sample = y_sample.reshape(n_dec, dec_seq, D_MODEL)
    new_sample = (news[0].reshape(1, n_dec, dec_seq, 2, A_KV_HEADS, HEAD_DIM),
                  news[1].reshape(1, n_dec, dec_seq, 2, B_HEADS, HEAD_DIM),
                  news[2].reshape(1, n_dec, dec_seq, 2, B_HEADS, HEAD_DIM),
                  news[3].reshape(1, n_dec, dec_seq, 2, B_HEADS, HEAD_DIM))
    return (y_prompt, y_sample) + new_prompt + new_sample
```

```python
import functools
import math

import numpy as np
import jax
import jax.numpy as jnp
from jax import lax
from jax.experimental import pallas as pl
from jax.experimental.pallas import tpu as pltpu

D_MODEL = 1024
HEAD_DIM = 64
A_Q_HEADS = 8
A_KV_HEADS = 2
A_WINDOW = 128
B_GROUPS = ((128, 1), (512, 4), (2048, 16))
N_B_GROUPS = 3
B_HEADS = 8
N_KEYS = 128
A_WIDTH = A_Q_HEADS * HEAD_DIM
A_KV_WIDTH = A_KV_HEADS * HEAD_DIM
B_WIDTH = B_HEADS * HEAD_DIM
N_BUCKETS = 32
MAX_DISTANCE = 2048
EPS = 1e-6
NEG_INF = -1e30
Q_SCALE = HEAD_DIM ** -0.5

LANES = 128
MXU_COLS = 256
VMEM_LIMIT = 56 * 1024 * 1024
N_PAIRS = A_WIDTH // LANES

A_HEAD_ORDER = (0, 4, 1, 5, 2, 6, 3, 7)

Q_COLS = A_WIDTH + N_B_GROUPS * B_WIDTH
KV_COLS = A_KV_WIDTH + N_B_GROUPS * B_WIDTH
QKV_COLS = Q_COLS + 2 * KV_COLS
G_COLS = A_WIDTH + B_WIDTH + 2 * D_MODEL
KV_WIDTHS = (A_KV_WIDTH, B_WIDTH, B_WIDTH, B_WIDTH)
HIGHEST = lax.Precision.HIGHEST


def _t5_bucket_np(dist):
    max_exact = N_BUCKETS // 2
    d = np.maximum(dist, 0)
    df = np.maximum(d, 1).astype(np.float32)
    large = max_exact + (np.log(df / np.float32(max_exact)) / np.float32(math.log(MAX_DISTANCE / max_exact))
                         * np.float32(N_BUCKETS - max_exact)).astype(np.int32)
    large = np.minimum(large, N_BUCKETS - 1)
    return np.where(d < max_exact, d, large)


def _rmsnorm_bf16(x, gain):
    ms = jnp.mean(x * x, axis=-1, keepdims=True)
    return (x * lax.rsqrt(ms + EPS) * gain).astype(jnp.bfloat16)


def _state_plan(n_rows, seq, tm):
    tpb = seq // tm
    r = min(n_rows, tm)
    nblk = max(n_rows // tm, 1)
    return tpb, r, nblk, tpb - nblk


def _qkv_kernel(x_ref, ng_ref, w_ref, gq_ref, gk_ref, ones_ref, *rest, tm, seq, state_rows, dils):
    q_refs = rest[0:4]
    k_refs = rest[4:8]
    v_refs = rest[8:12]
    state_refs = rest[12:16]
    tmp_ref = rest[16]
    i = pl.program_id(0)
    j = i % (seq // tm)
    h = _rmsnorm_bf16(x_ref[...], ng_ref[...])
    ones = ones_ref[...]

    def headnorm(a, gain):
        w = a.shape[1]
        ss = jnp.dot((a * a).astype(jnp.bfloat16), ones[:w, :w], preferred_element_type=jnp.float32)
        return a * lax.rsqrt(ss + EPS) * gain

    def project(col0, width, gain_ref, gcol0, out_ref, d, state=None):
        for c in range(0, width, MXU_COLS):
            w = min(MXU_COLS, width - c)
            a = jnp.dot(h, w_ref[:, col0 + c:col0 + c + w], preferred_element_type=jnp.float32)
            if gain_ref is not None:
                a = headnorm(a, gain_ref[:, gcol0 + c:gcol0 + c + w])
            if d == 1:
                out_ref[0, 0, :, c:c + w] = a.astype(out_ref.dtype)
            else:
                for s in range(w // LANES):
                    tmp_ref[c // LANES + s] = a[:, s * LANES:(s + 1) * LANES]
            if state is not None:
                sref, scol0, r, j0 = state

                @pl.when(j >= j0)
                def _():
                    sref[:, scol0 + c:scol0 + c + w] = a[tm - r:, :]
        if d > 1:
            for cls in range(d):
                for s in range(width // LANES):
                    out_ref[0, cls, :, s * LANES:(s + 1) * LANES] = (
                        tmp_ref[s, pl.ds(cls, tm // d, stride=d), :].astype(out_ref.dtype))

    q_off = (0, A_WIDTH, A_WIDTH + B_WIDTH, A_WIDTH + 2 * B_WIDTH)
    kv_off = (0, A_KV_WIDTH, A_KV_WIDTH + B_WIDTH, A_KV_WIDTH + 2 * B_WIDTH)
    for g in range(4):
        wd = KV_WIDTHS[g]
        _, r, _, j0 = _state_plan(state_rows[g], seq, tm)
        project(q_off[g], A_WIDTH, gq_ref, q_off[g], q_refs[g], dils[g])
        project(Q_COLS + kv_off[g], wd, gk_ref, kv_off[g], k_refs[g], dils[g], (state_refs[g], 0, r, j0))
        project(Q_COLS + KV_COLS + kv_off[g], wd, None, 0, v_refs[g], dils[g], (state_refs[g], wd, r, j0))


def _qkv_proj(x2d, ng, w_qkv, gq, gk, ones, *, nb, seq, state_rows, dils, tm):
    t = x2d.shape[0]
    tpb = seq // tm

    def cls_shape(d, width):
        return jax.ShapeDtypeStruct((nb, d, seq // d, width), jnp.bfloat16)

    def cls_spec(d, width):
        return pl.BlockSpec((1, d, tm // d, width), lambda i: (i // tpb, 0, i % tpb, 0))

    def state_spec(g, width):
        _, r, nblk, j0 = _state_plan(state_rows[g], seq, tm)
        return pl.BlockSpec((r, width), lambda i: ((i // tpb) * nblk + jnp.maximum(i % tpb - j0, 0), 0))

    const = lambda i: (0, 0)
    out_shape = ([cls_shape(d, A_WIDTH) for d in dils]
                 + [cls_shape(d, w) for d, w in zip(dils, KV_WIDTHS)] * 2
                 + [jax.ShapeDtypeStruct((nb * state_rows[g], 2 * KV_WIDTHS[g]), jnp.float32) for g in range(4)])
    out_specs = ([cls_spec(d, A_WIDTH) for d in dils]
                 + [cls_spec(d, w) for d, w in zip(dils, KV_WIDTHS)] * 2
                 + [state_spec(g, 2 * KV_WIDTHS[g]) for g in range(4)])
    res = pl.pallas_call(
        functools.partial(_qkv_kernel, tm=tm, seq=seq, state_rows=state_rows, dils=dils),
        grid=(t // tm,),
        in_specs=[
            pl.BlockSpec((tm, D_MODEL), lambda i: (i, 0)),
            pl.BlockSpec((1, D_MODEL), const),
            pl.BlockSpec((D_MODEL, QKV_COLS), const, pipeline_mode=pl.Buffered(1)),
            pl.BlockSpec((1, Q_COLS), const),
            pl.BlockSpec((1, KV_COLS), const),
            pl.BlockSpec((MXU_COLS, MXU_COLS), const),
        ],
        out_specs=out_specs,
        out_shape=out_shape,
        scratch_shapes=[pltpu.VMEM((N_PAIRS, tm, LANES), jnp.float32)],
        compiler_params=pltpu.CompilerParams(
            dimension_semantics=("arbitrary",), vmem_limit_bytes=VMEM_LIMIT),
        name="qkv_proj",
    )(x2d, ng, w_qkv, gq, gk, ones)
    return res[0:4], res[4:8], res[8:12], res[12:16]


def _band_attn_kernel(q_ref, k_ref, kp_ref, v_ref, vp_ref, bias_ref, sink_ref, *rest,
                      tt, dil, shared_kv, has_sink, want_lse):
    if want_lse:
        o_ref, lse_ref, kbuf, vbuf, o_scr = rest
    else:
        o_ref, kbuf, vbuf, o_scr = rest
        lse_ref = None
    blk = N_KEYS
    i = pl.program_id(1)
    nblk = tt // dil // blk
    kbuf[:, 0:blk, :] = kp_ref[0]
    kbuf[:, blk:, :] = k_ref[0]
    vbuf[:, 0:blk, :] = vp_ref[0]
    vbuf[:, blk:, :] = v_ref[0]

    row = lax.broadcasted_iota(jnp.int32, (2 * blk, blk), 0)
    lane = lax.broadcasted_iota(jnp.int32, (2 * blk, blk), 1)
    ahead = lane - jnp.bitwise_and(row, blk - 1)
    tri = ahead <= 0
    lane1 = lax.broadcasted_iota(jnp.int32, (blk, LANES), 1)
    left = lane1 < HEAD_DIM
    mask_l = left.astype(jnp.bfloat16)
    mask_r = jnp.logical_not(left).astype(jnp.bfloat16)
    ones_kv = jnp.ones((2 * blk, LANES), jnp.bfloat16)
    nt = (((1,), (1,)), ((), ()))

    def rows_of(cls, r0):
        start = cls + dil * r0
        return pl.ds(start, blk) if dil == 1 else pl.ds(start, blk, stride=dil)

    def block(c, carry):
        cls = c // nblk
        j = c % nblk
        r0 = pl.multiple_of(j * blk, blk)
        reach = jnp.where(jnp.logical_and(i == 0, j == 0), 0, blk)
        valid = ahead <= reach
        for p in range(N_PAIRS):
            kcol = 0 if shared_kv else p * LANES
            qp = q_ref[0, cls, pl.ds(r0, blk), p * LANES:(p + 1) * LANES]
            qs = jnp.concatenate([qp * mask_l, qp * mask_r], axis=0)
            kc = kbuf[cls, pl.ds(r0, 2 * blk), kcol:kcol + LANES]
            vc = vbuf[cls, pl.ds(r0, 2 * blk), kcol:kcol + LANES]
            s2 = lax.dot_general(qs, kc, nt, preferred_element_type=jnp.float32)
            s = jnp.where(tri, s2[:, blk:], s2[:, :blk]) + bias_ref[p]
            s = jnp.where(valid, s, NEG_INF)
            m = jnp.max(s, axis=-1, keepdims=True)
            if has_sink:
                sk = sink_ref[p][:, 0:1]
                m = jnp.maximum(m, sk)
            pe = jnp.exp(s - m)
            p2 = jnp.concatenate([jnp.where(tri, 0.0, pe).astype(jnp.bfloat16),
                                  jnp.where(tri, pe, 0.0).astype(jnp.bfloat16)], axis=1)
            ov = jnp.dot(p2, jnp.concatenate([vc, ones_kv], axis=1), preferred_element_type=jnp.float32)
            l = ov[:, LANES:]
            if has_sink:
                l = l + jnp.exp(sk - m)
            o2 = ov[:, :LANES] / l
            o_scr[p, rows_of(cls, r0), :] = jnp.where(left, o2[:blk], o2[blk:])
            if want_lse:
                lse2 = m + jnp.log(l)
                lse_ref[0, p, rows_of(cls, r0), :] = jnp.where(left, lse2[:blk], lse2[blk:])
        return carry

    lax.fori_loop(0, dil * nblk, block, 0)
    for p in range(N_PAIRS):
        o_ref[0, p] = o_scr[p].astype(o_ref.dtype)


def _band_attention(q, k, v, bias2, sink2, *, shared_kv, has_sink, want_lse):
    nb, dil, m, wq = q.shape
    wkv = k.shape[3]
    seq = dil * m
    tt = max(min(seq, 1024), dil * N_KEYS)
    nblk_prev = tt // dil // N_KEYS
    cur = lambda b, i: (b, 0, i, 0)
    prev = lambda b, i: (b, 0, jnp.maximum(i * nblk_prev - 1, 0), 0)
    slab = pl.BlockSpec((1, N_PAIRS, tt, LANES), cur)
    out_shape = [jax.ShapeDtypeStruct((nb, N_PAIRS, seq, LANES), jnp.bfloat16)]
    out_specs = [slab]
    if want_lse:
        out_shape.append(jax.ShapeDtypeStruct((nb, N_PAIRS, seq, LANES), jnp.float32))
        out_specs.append(slab)
    return pl.pallas_call(
        functools.partial(_band_attn_kernel, tt=tt, dil=dil, shared_kv=shared_kv, has_sink=has_sink,
                          want_lse=want_lse),
        grid=(nb, seq // tt),
        in_specs=[
            pl.BlockSpec((1, dil, tt // dil, wq), cur),
            pl.BlockSpec((1, dil, tt // dil, wkv), cur),
            pl.BlockSpec((1, dil, N_KEYS, wkv), prev),
            pl.BlockSpec((1, dil, tt // dil, wkv), cur),
            pl.BlockSpec((1, dil, N_KEYS, wkv), prev),
            pl.BlockSpec(bias2.shape, lambda b, i: (0, 0, 0)),
            pl.BlockSpec(sink2.shape, lambda b, i: (0, 0, 0)),
        ],
        out_specs=out_specs,
        out_shape=out_shape,
        scratch_shapes=[pltpu.VMEM((dil, tt // dil + N_KEYS, wkv), jnp.bfloat16),
                        pltpu.VMEM((dil, tt // dil + N_KEYS, wkv), jnp.bfloat16),
                        pltpu.VMEM((N_PAIRS, tt, LANES), jnp.float32)],
        compiler_params=pltpu.CompilerParams(
            dimension_semantics=("arbitrary", "arbitrary"), vmem_limit_bytes=VMEM_LIMIT),
        name="band_attn_d%d" % dil,
    )(q, k, k, v, v, bias2, sink2)


def _decode_kernel(q_ref, na_ref, nb1_ref, nb2_ref, nb3_ref, ca_ref, cb1_ref, cb2_ref, cb3_ref,
                   ta_ref, tb1_ref, tb2_ref, tb3_ref, tn_ref, sink_ref, oa_ref, ob_ref, *, dec_seq):
    t = dec_seq
    left8 = lax.broadcasted_iota(jnp.int32, (2 * t, LANES), 1) < HEAD_DIM
    top8 = lax.broadcasted_iota(jnp.int32, (2 * t, LANES), 0) < t
    own = (left8 == top8).astype(jnp.float32)
    left4 = lax.broadcasted_iota(jnp.int32, (t, LANES), 1) < HEAD_DIM
    nt = (((1,), (1,)), ((), ()))

    def q_rows(c0):
        qp = q_ref[0, :, c0:c0 + LANES].astype(jnp.float32)
        return jnp.concatenate([qp, qp], axis=0) * own

    def attend(qr, kt, vt, knew, vnew, tbl, tbl_new, sink):
        s_c = jnp.dot(qr.astype(jnp.bfloat16), kt, preferred_element_type=jnp.float32) + tbl
        s_n = [jnp.sum(qr * knew[u:u + 1, :], axis=-1, keepdims=True) + tbl_new[:, u:u + 1] for u in range(t)]
        m = functools.reduce(jnp.maximum, s_n, jnp.max(s_c, axis=-1, keepdims=True))
        if sink is not None:
            m = jnp.maximum(m, sink)
        pc = jnp.exp(s_c - m)
        pn = [jnp.exp(x - m) for x in s_n]
        l = functools.reduce(lambda a, b: a + b, pn, jnp.sum(pc, axis=-1, keepdims=True))
        if sink is not None:
            l = l + jnp.exp(sink - m)
        o = lax.dot_general(pc.astype(jnp.bfloat16), vt, nt, preferred_element_type=jnp.float32)
        for u in range(t):
            o = o + pn[u] * vnew[u:u + 1, :]
        return o / l, m + jnp.log(l)

    def fold(x8):
        return jnp.where(left4, x8[:t], x8[t:])

    kt = ca_ref[0, 0].reshape(LANES, ca_ref.shape[-1]).astype(jnp.bfloat16)
    vt = ca_ref[0, 1].reshape(LANES, ca_ref.shape[-1]).astype(jnp.bfloat16)
    knew = na_ref[0, :, 0:LANES]
    vnew = na_ref[0, :, LANES:2 * LANES]
    for p in range(N_PAIRS):
        rows = slice(p * 2 * t, (p + 1) * 2 * t)
        o8, _ = attend(q_rows(p * LANES), kt, vt, knew, vnew, ta_ref[rows, :],
                       tn_ref[0, rows, :], sink_ref[rows, 0:1])
        oa_ref[0, p] = fold(o8).astype(oa_ref.dtype)

    caches = (cb1_ref, cb2_ref, cb3_ref)
    news = (nb1_ref, nb2_ref, nb3_ref)
    tbls = (tb1_ref, tb2_ref, tb3_ref)
    for p in range(N_PAIRS):
        rows = slice(p * 2 * t, (p + 1) * 2 * t)
        outs, lses = [], []
        for g in range(N_B_GROUPS):
            c_ref = caches[g]
            kt = c_ref[0, 0, 2 * p:2 * p + 2].reshape(LANES, c_ref.shape[-1]).astype(jnp.bfloat16)
            vt = c_ref[0, 1, 2 * p:2 * p + 2].reshape(LANES, c_ref.shape[-1]).astype(jnp.bfloat16)
            knew = news[g][0, :, p * LANES:(p + 1) * LANES]
            vnew = news[g][0, :, B_WIDTH + p * LANES:B_WIDTH + (p + 1) * LANES]
            o8, lse8 = attend(q_rows(A_WIDTH + g * B_WIDTH + p * LANES), kt, vt, knew, vnew,
                              tbls[g][rows, :], tn_ref[1 + g, rows, :], None)
            outs.append(fold(o8))
            lses.append(fold(jnp.broadcast_to(lse8, (2 * t, LANES))))
        mx = jnp.maximum(jnp.maximum(lses[0], lses[1]), lses[2])
        es = [jnp.exp(x - mx) for x in lses]
        den = es[0] + es[1] + es[2]
        comb = (es[0] * outs[0] + es[1] * outs[1] + es[2] * outs[2]) / den
        ob_ref[0, p] = comb.astype(ob_ref.dtype)


def _decode_attention(q, news, caches, tbls, tbl_new, sink_rows):
    n, t, _ = q.shape
    seq3 = lambda i: (i, 0, 0)
    seq4 = lambda i: (i, 0, 0, 0)
    seq5 = lambda i: (i, 0, 0, 0, 0)
    in_specs = [pl.BlockSpec((1, t, q.shape[2]), seq3)]
    in_specs += [pl.BlockSpec((1, t, a.shape[2]), seq3) for a in news]
    in_specs += [pl.BlockSpec((1,) + c.shape[1:], seq5) for c in caches]
    in_specs += [pl.BlockSpec(tb.shape, lambda i: (0, 0)) for tb in tbls]
    in_specs += [pl.BlockSpec(tbl_new.shape, lambda i: (0, 0, 0)),
                 pl.BlockSpec(sink_rows.shape, lambda i: (0, 0))]
    slab = pl.BlockSpec((1, N_PAIRS, t, LANES), seq4)
    return pl.pallas_call(
        functools.partial(_decode_kernel, dec_seq=t),
        grid=(n,),
        in_specs=in_specs,
        out_specs=(slab, slab),
        out_shape=(jax.ShapeDtypeStruct((n, N_PAIRS, t, LANES), jnp.float32),
                   jax.ShapeDtypeStruct((n, N_PAIRS, t, LANES), jnp.float32)),
        compiler_params=pltpu.CompilerParams(
            dimension_semantics=("arbitrary",), vmem_limit_bytes=VMEM_LIMIT),
        name="decode_attn",
    )(q, *news, *caches, *tbls, tbl_new, sink_rows)


def _out_kernel(*refs, n_groups):
    x_ref, ng_ref, wg_ref, oa_ref = refs[0:4]
    ob_refs = refs[4:4 + n_groups]
    lse_refs = refs[4 + n_groups:4 + 2 * n_groups] if n_groups > 1 else ()
    wa_ref, wb_ref, wo_ref, y_ref = refs[-4:]
    x = x_ref[0]
    h = _rmsnorm_bf16(x, ng_ref[...])

    def gate(c0, width, silu):
        a = jnp.dot(h, wg_ref[:, c0:c0 + width], preferred_element_type=jnp.float32)
        sg = 1.0 / (1.0 + jnp.exp(-a))
        return a * sg if silu else sg

    def slabs(ref):
        return jnp.concatenate([ref[0, p].astype(jnp.float32) for p in range(N_PAIRS)], axis=1)

    if n_groups > 1:
        parts = []
        for p in range(N_PAIRS):
            lses = [r[0, p] for r in lse_refs]
            mx = functools.reduce(jnp.maximum, lses)
            es = [jnp.exp(v - mx) for v in lses]
            den = functools.reduce(lambda a, b: a + b, es)
            num = functools.reduce(lambda a, b: a + b,
                                   [e * r[0, p].astype(jnp.float32) for e, r in zip(es, ob_refs)])
            parts.append(num / den)
        ob = jnp.concatenate(parts, axis=1)
    else:
        ob = slabs(ob_refs[0])
    oa = slabs(oa_ref)
    ya = jnp.dot((oa * gate(0, A_WIDTH, True)).astype(jnp.bfloat16), wa_ref[...],
                 preferred_element_type=jnp.float32)
    yb = jnp.dot((ob * gate(A_WIDTH, B_WIDTH, True)).astype(jnp.bfloat16), wb_ref[...],
                 preferred_element_type=jnp.float32)
    m0 = A_WIDTH + B_WIDTH
    merged = (gate(m0, D_MODEL, False) * ya + gate(m0 + D_MODEL, D_MODEL, False) * yb).astype(jnp.bfloat16)
    y_ref[0] = x + jnp.dot(merged, wo_ref[...], preferred_element_type=jnp.float32)


def _out_proj(x3d, ng, wg, oa, obs, lses, wa, wb, wo, *, tm):
    nb, seq, _ = x3d.shape
    row = lambda b, i: (b, i, 0)
    slab = pl.BlockSpec((1, N_PAIRS, tm, LANES), lambda b, i: (b, 0, i, 0))
    const = lambda b, i: (0, 0)
    once = dict(pipeline_mode=pl.Buffered(1))
    in_specs = [pl.BlockSpec((1, tm, D_MODEL), row), pl.BlockSpec((1, D_MODEL), const),
                pl.BlockSpec(wg.shape, const, **once), slab]
    in_specs += [slab for _ in obs] + [slab for _ in lses]
    in_specs += [pl.BlockSpec(wa.shape, const, **once), pl.BlockSpec(wb.shape, const, **once),
                 pl.BlockSpec(wo.shape, const, **once)]
    return pl.pallas_call(
        functools.partial(_out_kernel, n_groups=len(obs)),
        grid=(nb, seq // tm),
        in_specs=in_specs,
        out_specs=pl.BlockSpec((1, tm, D_MODEL), row),
        out_shape=jax.ShapeDtypeStruct((nb, seq, D_MODEL), jnp.float32),
        compiler_params=pltpu.CompilerParams(
            dimension_semantics=("arbitrary", "arbitrary"), vmem_limit_bytes=VMEM_LIMIT),
        name="out_proj",
    )(x3d, ng, wg, oa, *obs, *lses, wa, wb, wo)


def _bias_lookup(rel_bias_cols, buckets):
    flat = np.asarray(buckets).reshape(-1)
    onehot = jnp.asarray((flat[:, None] == np.arange(N_BUCKETS)[None, :]).astype(np.float32))
    out = jnp.dot(onehot, rel_bias_cols, precision=HIGHEST)
    return out.reshape(tuple(np.shape(buckets)) + (rel_bias_cols.shape[1],))


def _prep_params(rel_bias, w_in, q_gain_a, k_gain_a, sinks_a, q_gain_b, k_gain_b, w_up_a, w_up_b):
    offs = np.cumsum((0, A_WIDTH, A_KV_WIDTH, A_KV_WIDTH, A_WIDTH, 3 * B_WIDTH, 3 * B_WIDTH, 3 * B_WIDTH,
                      B_WIDTH, D_MODEL, D_MODEL))
    seg = lambda s: w_in[:, int(offs[s]):int(offs[s + 1])]

    def perm_heads(w2d):
        return jnp.concatenate([w2d[:, h * HEAD_DIM:(h + 1) * HEAD_DIM] for h in A_HEAD_ORDER], axis=1)

    w_qkv = jnp.concatenate([perm_heads(seg(0)), seg(4), seg(1), seg(5), seg(2), seg(6)],
                            axis=1).astype(jnp.bfloat16)
    w_gate = jnp.concatenate([perm_heads(seg(3)), seg(7), seg(8), seg(9)], axis=1).astype(jnp.bfloat16)
    gq = jnp.concatenate([jnp.tile(q_gain_a, A_Q_HEADS)] +
                         [jnp.tile(q_gain_b[g], B_HEADS) for g in range(N_B_GROUPS)])[None, :] * Q_SCALE
    gk = jnp.concatenate([jnp.tile(k_gain_a, A_KV_HEADS)] +
                         [jnp.tile(k_gain_b[g], B_HEADS) for g in range(N_B_GROUPS)])[None, :]
    hd = np.arange(MXU_COLS) // HEAD_DIM
    ones = jnp.asarray((hd[:, None] == hd[None, :]).astype(np.float32) / HEAD_DIM, jnp.bfloat16)
    wa = jnp.concatenate([w_up_a[h * HEAD_DIM:(h + 1) * HEAD_DIM] for h in A_HEAD_ORDER], axis=0)
    head_cols = [np.asarray(A_HEAD_ORDER)] + [A_Q_HEADS + g * B_HEADS + np.arange(B_HEADS)
                                              for g in range(N_B_GROUPS)]
    rb_cols = []
    for cols in head_cols:
        sel = np.zeros((rel_bias.shape[1], len(cols)), np.float32)
        sel[cols, np.arange(len(cols))] = 1.0
        rb_cols.append(jnp.dot(rel_bias, jnp.asarray(sel), precision=HIGHEST))
    dils = (1,) + tuple(d for _, d in B_GROUPS)
    qi = np.arange(N_KEYS)[:, None]
    ci = np.arange(N_KEYS)[None, :]
    steps = (qi - ci) % N_KEYS
    band_bias = []
    for d, rb in zip(dils, rb_cols):
        b = jnp.transpose(_bias_lookup(rb, _t5_bucket_np(steps * d)), (2, 0, 1))
        band_bias.append(b.reshape(N_PAIRS, 2 * N_KEYS, N_KEYS))
    sink_perm = jnp.dot(sinks_a[None, :], jnp.asarray(np.eye(A_Q_HEADS, dtype=np.float32)[:, list(A_HEAD_ORDER)]),
                        precision=HIGHEST)[0]
    return (w_qkv, w_gate, gq, gk, ones, wa.astype(jnp.bfloat16), w_up_b.astype(jnp.bfloat16), band_bias,
            sink_perm, rb_cols, dils)


def _decode_tables(rb_cols, sink_perm, dils, cache_lens, dec_seq):
    t_idx = np.arange(dec_seq)
    tbls = []
    new_rows = []
    for d, rb, ln in zip(dils, rb_cols, cache_lens):
        nh = rb.shape[1]
        pos = np.arange(ln)
        back = ln + t_idx[:, None] - pos[None, :]
        ok = (back % d == 0) & (back // d >= 1) & (back // d < N_KEYS)
        b = _bias_lookup(rb, _t5_bucket_np(np.where(ok, back, 0)))
        b = b + jnp.asarray(np.where(ok, 0.0, NEG_INF).astype(np.float32))[:, :, None]
        tbls.append(jnp.transpose(b, (2, 0, 1)).reshape(nh * dec_seq, ln))
        backn = t_idx[:, None] - t_idx[None, :]
        okn = (backn >= 0) & (backn % d == 0) & (backn // d < N_KEYS)
        bn = _bias_lookup(rb, _t5_bucket_np(np.where(okn, backn, 0)))
        bn = bn + jnp.asarray(np.where(okn, 0.0, NEG_INF).astype(np.float32))[:, :, None]
        new_rows.append(jnp.transpose(bn, (2, 0, 1)).reshape(nh * dec_seq, dec_seq))
    tbl_new = jnp.stack(new_rows)
    sink_rows = jnp.broadcast_to(jnp.repeat(sink_perm, dec_seq)[:, None], (A_Q_HEADS * dec_seq, LANES))
    return tbls, tbl_new, sink_rows


def kernel(x_prompt, x_sample, cache_a_kv, cache_b1_kv, cache_b2_kv, cache_b3_kv, rel_bias, norm_gain, w_in,
           q_gain_a, k_gain_a, sinks_a, q_gain_b, k_gain_b, w_up_a, w_up_b, w_out):
    assert norm_gain.shape[0] == 1, "single layer"
    nb, seq, _ = x_prompt.shape
    n_dec, dec_seq, _ = x_sample.shape
    (w_qkv, w_gate, gq, gk, ones, wa, wb, band_bias, sink_perm, rb_cols, dils) = _prep_params(
        rel_bias, w_in[0], q_gain_a[0], k_gain_a[0], sinks_a[0], q_gain_b[0], k_gain_b[0], w_up_a[0], w_up_b[0])
    wo = w_out[0].astype(jnp.bfloat16)
    ng = norm_gain
    windows = (A_WINDOW,) + tuple(w for w, _ in B_GROUPS)
    sink2 = jnp.broadcast_to(sink_perm.reshape(N_PAIRS, 2, 1, 1),
                             (N_PAIRS, 2, N_KEYS, LANES)).reshape(N_PAIRS, 2 * N_KEYS, LANES)

    p_rows = tuple(min(w, seq) for w in windows)
    qs, ks, vs, states = _qkv_proj(x_prompt.reshape(nb * seq, D_MODEL), ng, w_qkv, gq, gk, ones,
                                   nb=nb, seq=seq, state_rows=p_rows, dils=dils, tm=512)
    (oa,) = _band_attention(qs[0], ks[0], vs[0], band_bias[0], sink2,
                            shared_kv=True, has_sink=True, want_lse=False)
    obs, lses = [], []
    for g in range(1, 4):
        og, lg = _band_attention(qs[g], ks[g], vs[g], band_bias[g], sink2,
                                 shared_kv=False, has_sink=False, want_lse=True)
        obs.append(og)
        lses.append(lg)
    y_prompt = _out_proj(x_prompt, ng, w_gate, oa, obs, lses, wa, wb, wo, tm=512)
    heads = (A_KV_HEADS, B_HEADS, B_HEADS, B_HEADS)
    new_prompt = tuple(states[g].reshape(1, nb, p_rows[g], 2, heads[g], HEAD_DIM) for g in range(4))

    t_dec = n_dec * dec_seq
    qd, _, _, news = _qkv_proj(x_sample.reshape(t_dec, D_MODEL), ng, w_qkv, gq, gk, ones,
                               nb=1, seq=t_dec, state_rows=(t_dec,) * 4, dils=(1, 1, 1, 1), tm=t_dec)
    q_dec = jnp.concatenate([a.reshape(n_dec, dec_seq, A_WIDTH) for a in qd], axis=2)
    caches = [jnp.transpose(c[0], (0, 2, 3, 4, 1)) for c in (cache_a_kv, cache_b1_kv, cache_b2_kv, cache_b3_kv)]
    cache_lens = tuple(c.shape[-1] for c in caches)
    tbls, tbl_new, sink_rows = _decode_tables(rb_cols, sink_perm, dils, cache_lens, dec_seq)
    news3 = [a.reshape(n_dec, dec_seq, a.shape[1]) for a in news]
    oa_s, ob_s = _decode_attention(q_dec, news3, caches, tbls, tbl_new, sink_rows)
    to_slabs = lambda o: jnp.transpose(o, (1, 0, 2, 3)).reshape(1, N_PAIRS, t_dec, LANES)
    y_sample = _out_proj(x_sample.reshape(1, t_dec, D_MODEL), ng, w_gate, to_slabs(oa_s), [to_slabs(ob_s)], [],
                         wa, wb, wo, tm=t_dec)
    y_sample = y_sample.reshape(n_dec, dec_seq, D_MODEL)
    new_sample = tuple(news3[g].reshape(1, n_dec, dec_seq, 2, heads[g], HEAD_DIM) for g in range(4))
    return (y_prompt, y_sample) + new_prompt + new_sample
```

```python
import functools
import math

import numpy as np
import jax
import jax.numpy as jnp
from jax import lax
from jax.experimental import pallas as pl
from jax.experimental.pallas import tpu as pltpu

D_MODEL = 1024
HEAD_DIM = 64
A_Q_HEADS = 8
A_KV_HEADS = 2
A_WINDOW = 128
B_GROUPS = ((128, 1), (512, 4), (2048, 16))
N_B_GROUPS = 3
B_HEADS = 8
N_KEYS = 128
A_WIDTH = A_Q_HEADS * HEAD_DIM
A_KV_WIDTH = A_KV_HEADS * HEAD_DIM
B_WIDTH = B_HEADS * HEAD_DIM
N_BUCKETS = 32
MAX_DISTANCE = 2048
EPS = 1e-6
NEG_INF = -1e30
Q_SCALE = HEAD_DIM ** -0.5

LANES = 128
MXU_COLS = 256
VMEM_LIMIT = 56 * 1024 * 1024
N_PAIRS = A_WIDTH // LANES

A_HEAD_ORDER = (0, 4, 1, 5, 2, 6, 3, 7)

Q_COLS = A_WIDTH + N_B_GROUPS * B_WIDTH
KV_COLS = A_KV_WIDTH + N_B_GROUPS * B_WIDTH
QKV_COLS = Q_COLS + 2 * KV_COLS
G_COLS = A_WIDTH + B_WIDTH + 2 * D_MODEL
KV_WIDTHS = (A_KV_WIDTH, B_WIDTH, B_WIDTH, B_WIDTH)
HIGHEST = lax.Precision.HIGHEST


def _t5_bucket_np(dist):
    max_exact = N_BUCKETS // 2
    d = np.maximum(dist, 0)
    df = np.maximum(d, 1).astype(np.float32)
    large = max_exact + (np.log(df / np.float32(max_exact)) / np.float32(math.log(MAX_DISTANCE / max_exact))
                         * np.float32(N_BUCKETS - max_exact)).astype(np.int32)
    large = np.minimum(large, N_BUCKETS - 1)
    return np.where(d < max_exact, d, large)


def _rmsnorm_bf16(x, gain):
    ms = jnp.mean(x * x, axis=-1, keepdims=True)
    return (x * lax.rsqrt(ms + EPS) * gain).astype(jnp.bfloat16)


def _state_plan(n_rows, seq, tm):
    tpb = seq // tm
    r = min(n_rows, tm)
    nblk = max(n_rows // tm, 1)
    return tpb, r, nblk, tpb - nblk


def _qkv_kernel(x_ref, ng_ref, w_ref, gq_ref, gk_ref, ones_ref, *rest, tm, seq, state_rows, dils):
    q_refs = rest[0:4]
    k_refs = rest[4:8]
    v_refs = rest[8:12]
    state_refs = rest[12:16]
    tmp_ref = rest[16]
    h = _rmsnorm_bf16(x_ref[...], ng_ref[...])
    ones = ones_ref[...]

    def headnorm(a, gain):
        w = a.shape[1]
        ss = jnp.dot((a * a).astype(jnp.bfloat16), ones[:w, :w], preferred_element_type=jnp.float32)
        return a * lax.rsqrt(ss + EPS) * gain

    def project(col0, width, gain_ref, gcol0, out_ref, d, state=None):
        for c in range(0, width, MXU_COLS):
            w = min(MXU_COLS, width - c)
            a = jnp.dot(h, w_ref[:, col0 + c:col0 + c + w], preferred_element_type=jnp.float32)
            if gain_ref is not None:
                a = headnorm(a, gain_ref[:, gcol0 + c:gcol0 + c + w])
            if d == 1:
                out_ref[0, 0, :, c:c + w] = a.astype(out_ref.dtype)
            else:
                for s in range(w // LANES):
                    tmp_ref[c // LANES + s] = a[:, s * LANES:(s + 1) * LANES]
            if state is not None:
                sref, scol0, r = state
                sref[:, scol0 + c:scol0 + c + w] = a[tm - r:, :]
        if d > 1:
            for cls in range(d):
                for s in range(width // LANES):
                    out_ref[0, cls, :, s * LANES:(s + 1) * LANES] = (
                        tmp_ref[s, pl.ds(cls, tm // d, stride=d), :].astype(out_ref.dtype))

    q_off = (0, A_WIDTH, A_WIDTH + B_WIDTH, A_WIDTH + 2 * B_WIDTH)
    kv_off = (0, A_KV_WIDTH, A_KV_WIDTH + B_WIDTH, A_KV_WIDTH + 2 * B_WIDTH)
    for g in range(4):
        wd = KV_WIDTHS[g]
        _, r, _, _ = _state_plan(state_rows[g], seq, tm)
        project(q_off[g], A_WIDTH, gq_ref, q_off[g], q_refs[g], dils[g])
        project(Q_COLS + kv_off[g], wd, gk_ref, kv_off[g], k_refs[g], dils[g], (state_refs[g], 0, r))
        project(Q_COLS + KV_COLS + kv_off[g], wd, None, 0, v_refs[g], dils[g], (state_refs[g], wd, r))


def _qkv_proj(x2d, ng, w_qkv, gq, gk, ones, *, nb, seq, state_rows, dils, tm):
    t = x2d.shape[0]
    tpb = seq // tm

    def cls_shape(d, width):
        return jax.ShapeDtypeStruct((nb, d, seq // d, width), jnp.bfloat16)

    def cls_spec(d, width):
        return pl.BlockSpec((1, d, tm // d, width), lambda i: (i // tpb, 0, i % tpb, 0))

    def state_spec(g, width):
        _, r, nblk, j0 = _state_plan(state_rows[g], seq, tm)
        return pl.BlockSpec((r, width), lambda i: ((i // tpb) * nblk + jnp.maximum(i % tpb - j0, 0), 0))

    const = lambda i: (0, 0)
    out_shape = ([cls_shape(d, A_WIDTH) for d in dils]
                 + [cls_shape(d, w) for d, w in zip(dils, KV_WIDTHS)] * 2
                 + [jax.ShapeDtypeStruct((nb * state_rows[g], 2 * KV_WIDTHS[g]), jnp.float32) for g in range(4)])
    out_specs = ([cls_spec(d, A_WIDTH) for d in dils]
                 + [cls_spec(d, w) for d, w in zip(dils, KV_WIDTHS)] * 2
                 + [state_spec(g, 2 * KV_WIDTHS[g]) for g in range(4)])
    res = pl.pallas_call(
        functools.partial(_qkv_kernel, tm=tm, seq=seq, state_rows=state_rows, dils=dils),
        grid=(t // tm,),
        in_specs=[
            pl.BlockSpec((tm, D_MODEL), lambda i: (i, 0)),
            pl.BlockSpec((1, D_MODEL), const),
            pl.BlockSpec((D_MODEL, QKV_COLS), const, pipeline_mode=pl.Buffered(1)),
            pl.BlockSpec((1, Q_COLS), const),
            pl.BlockSpec((1, KV_COLS), const),
            pl.BlockSpec((MXU_COLS, MXU_COLS), const),
        ],
        out_specs=out_specs,
        out_shape=out_shape,
        scratch_shapes=[pltpu.VMEM((N_PAIRS, tm, LANES), jnp.float32)],
        compiler_params=pltpu.CompilerParams(
            dimension_semantics=("arbitrary",), vmem_limit_bytes=VMEM_LIMIT),
        name="qkv_proj",
    )(x2d, ng, w_qkv, gq, gk, ones)
    return res[0:4], res[4:8], res[8:12], res[12:16]


def _band_attn_kernel(q_ref, k_ref, kp_ref, v_ref, vp_ref, bias_ref, sink_ref, *rest,
                      tt, dil, shared_kv, has_sink, want_lse):
    if want_lse:
        o_ref, lse_ref, kbuf, vbuf, o_scr = rest
    else:
        o_ref, kbuf, vbuf, o_scr = rest
        lse_ref = None
    blk = N_KEYS
    i = pl.program_id(1)
    nblk = tt // dil // blk
    kbuf[:, 0:blk, :] = kp_ref[0]
    kbuf[:, blk:, :] = k_ref[0]
    vbuf[:, 0:blk, :] = vp_ref[0]
    vbuf[:, blk:, :] = v_ref[0]

    row = lax.broadcasted_iota(jnp.int32, (2 * blk, blk), 0)
    lane = lax.broadcasted_iota(jnp.int32, (2 * blk, blk), 1)
    ahead = lane - jnp.bitwise_and(row, blk - 1)
    tri = ahead <= 0
    cur_part = tri.astype(jnp.bfloat16)
    prev_part = jnp.logical_not(tri).astype(jnp.bfloat16)
    lane1 = lax.broadcasted_iota(jnp.int32, (blk, LANES), 1)
    left = lane1 < HEAD_DIM
    mask_l = left.astype(jnp.bfloat16)
    mask_r = jnp.logical_not(left).astype(jnp.bfloat16)
    ones_kv = jnp.ones((2 * blk, LANES), jnp.bfloat16)
    nt = (((1,), (1,)), ((), ()))

    def rows_of(cls, r0):
        start = cls + dil * r0
        return pl.ds(start, blk) if dil == 1 else pl.ds(start, blk, stride=dil)

    def block(c, carry, *, first_tile):
        cls = c // nblk
        j = c % nblk
        r0 = pl.multiple_of(j * blk, blk)
        if first_tile:
            valid = ahead <= jnp.where(j == 0, 0, blk)
        for p in range(N_PAIRS):
            kcol = 0 if shared_kv else p * LANES
            qp = q_ref[0, cls, pl.ds(r0, blk), p * LANES:(p + 1) * LANES]
            qs = jnp.concatenate([qp * mask_l, qp * mask_r], axis=0)
            kc = kbuf[cls, pl.ds(r0, 2 * blk), kcol:kcol + LANES]
            vc = vbuf[cls, pl.ds(r0, 2 * blk), kcol:kcol + LANES]
            s2 = lax.dot_general(qs, kc, nt, preferred_element_type=jnp.float32)
            s = jnp.where(tri, s2[:, blk:], s2[:, :blk]) + bias_ref[p]
            if first_tile:
                s = jnp.where(valid, s, NEG_INF)
            m = jnp.max(s, axis=-1, keepdims=True)
            if has_sink:
                sk = sink_ref[p]
                m = jnp.maximum(m, sk)
            pb = jnp.exp(s - m).astype(jnp.bfloat16)
            p2 = jnp.concatenate([pb * prev_part, pb * cur_part], axis=1)
            ov = jnp.dot(p2, jnp.concatenate([vc, ones_kv], axis=1), preferred_element_type=jnp.float32)
            l = ov[:, LANES:]
            if has_sink:
                l = l + jnp.exp(sk - m)
            num = jnp.where(left, ov[:blk, :LANES], ov[blk:, :LANES])
            den = jnp.where(left, l[:blk], l[blk:])
            o_scr[p, rows_of(cls, r0), :] = num / den
            if want_lse:
                m1 = jnp.where(left, m[:blk], m[blk:])
                lse_ref[0, p, rows_of(cls, r0), :] = m1 + jnp.log(den)
        return carry

    n_blocks = dil * nblk

    @pl.when(i == 0)
    def _():
        lax.fori_loop(0, n_blocks, functools.partial(block, first_tile=True), 0, unroll=min(4, n_blocks))

    @pl.when(i > 0)
    def _():
        lax.fori_loop(0, n_blocks, functools.partial(block, first_tile=False), 0, unroll=min(4, n_blocks))

    for p in range(N_PAIRS):
        o_ref[0, p] = o_scr[p].astype(o_ref.dtype)


def _band_attention(q, k, v, bias2, sink2, *, shared_kv, has_sink, want_lse):
    nb, dil, m, wq = q.shape
    wkv = k.shape[3]
    seq = dil * m
    tt = max(min(seq, 1024), dil * N_KEYS)
    nblk_prev = tt // dil // N_KEYS
    cur = lambda b, i: (b, 0, i, 0)
    prev = lambda b, i: (b, 0, jnp.maximum(i * nblk_prev - 1, 0), 0)
    slab = pl.BlockSpec((1, N_PAIRS, tt, LANES), cur)
    out_shape = [jax.ShapeDtypeStruct((nb, N_PAIRS, seq, LANES), jnp.bfloat16)]
    out_specs = [slab]
    if want_lse:
        out_shape.append(jax.ShapeDtypeStruct((nb, N_PAIRS, seq, LANES), jnp.float32))
        out_specs.append(slab)
    return pl.pallas_call(
        functools.partial(_band_attn_kernel, tt=tt, dil=dil, shared_kv=shared_kv, has_sink=has_sink,
                          want_lse=want_lse),
        grid=(nb, seq // tt),
        in_specs=[
            pl.BlockSpec((1, dil, tt // dil, wq), cur),
            pl.BlockSpec((1, dil, tt // dil, wkv), cur),
            pl.BlockSpec((1, dil, N_KEYS, wkv), prev),
            pl.BlockSpec((1, dil, tt // dil, wkv), cur),
            pl.BlockSpec((1, dil, N_KEYS, wkv), prev),
            pl.BlockSpec(bias2.shape, lambda b, i: (0, 0, 0)),
            pl.BlockSpec(sink2.shape, lambda b, i: (0, 0, 0)),
        ],
        out_specs=out_specs,
        out_shape=out_shape,
        scratch_shapes=[pltpu.VMEM((dil, tt // dil + N_KEYS, wkv), jnp.bfloat16),
                        pltpu.VMEM((dil, tt // dil + N_KEYS, wkv), jnp.bfloat16),
                        pltpu.VMEM((N_PAIRS, tt, LANES), jnp.float32)],
        compiler_params=pltpu.CompilerParams(
            dimension_semantics=("arbitrary", "arbitrary"), vmem_limit_bytes=VMEM_LIMIT),
        name="band_attn_d%d" % dil,
    )(q, k, k, v, v, bias2, sink2)


def _decode_kernel(q_ref, na_ref, nb1_ref, nb2_ref, nb3_ref, ca_ref, cb1_ref, cb2_ref, cb3_ref,
                   ta_ref, tb1_ref, tb2_ref, tb3_ref, tn_ref, sink_ref, oa_ref, ob_ref, *, dec_seq):
    t = dec_seq
    left8 = lax.broadcasted_iota(jnp.int32, (2 * t, LANES), 1) < HEAD_DIM
    top8 = lax.broadcasted_iota(jnp.int32, (2 * t, LANES), 0) < t
    own = (left8 == top8).astype(jnp.float32)
    left4 = lax.broadcasted_iota(jnp.int32, (t, LANES), 1) < HEAD_DIM
    nt = (((1,), (1,)), ((), ()))

    def q_rows(c0):
        qp = q_ref[0, :, c0:c0 + LANES].astype(jnp.float32)
        return jnp.concatenate([qp, qp], axis=0) * own

    def attend(qrs, kts, vts, knews, vnews, tbl, tbl_new, sink):
        shared = len(kts) == 1
        rows8 = lambda a, u: jnp.broadcast_to(a[u:u + 1, :], (2 * t, LANES))
        qr = jnp.concatenate(qrs, axis=0)
        qb = qr.astype(jnp.bfloat16)
        if shared:
            s_c = jnp.dot(qb, kts[0], preferred_element_type=jnp.float32)
        else:
            s_c = jnp.concatenate([jnp.dot(qb[2 * t * p:2 * t * (p + 1)], kts[p], preferred_element_type=jnp.float32)
                                   for p in range(N_PAIRS)], axis=0)
        s_c = s_c + tbl
        kn = [jnp.concatenate([rows8(knews[0 if shared else p], u) for p in range(N_PAIRS)], axis=0)
              for u in range(t)]
        vn = [jnp.concatenate([rows8(vnews[0 if shared else p], u) for p in range(N_PAIRS)], axis=0)
              for u in range(t)]
        s_n = [jnp.sum(qr * kn[u], axis=-1, keepdims=True) + tbl_new[:, u:u + 1] for u in range(t)]
        m = functools.reduce(jnp.maximum, s_n, jnp.max(s_c, axis=-1, keepdims=True))
        if sink is not None:
            m = jnp.maximum(m, sink)
        pc = jnp.exp(s_c - m)
        pn = [jnp.exp(x - m) for x in s_n]
        l = functools.reduce(lambda a, b: a + b, pn, jnp.sum(pc, axis=-1, keepdims=True))
        if sink is not None:
            l = l + jnp.exp(sink - m)
        pb = pc.astype(jnp.bfloat16)
        if shared:
            o = lax.dot_general(pb, vts[0], nt, preferred_element_type=jnp.float32)
        else:
            o = jnp.concatenate([lax.dot_general(pb[2 * t * p:2 * t * (p + 1)], vts[p], nt,
                                                 preferred_element_type=jnp.float32)
                                 for p in range(N_PAIRS)], axis=0)
        for u in range(t):
            o = o + pn[u] * vn[u]
        return o / l, m + jnp.log(l)

    def fold(x, p):
        return jnp.where(left4, x[2 * t * p:2 * t * p + t], x[2 * t * p + t:2 * t * (p + 1)])

    def cache_kv(c_ref, kv, p):
        return c_ref[0, kv, 2 * p:2 * p + 2].reshape(LANES, c_ref.shape[-1]).astype(jnp.bfloat16)

    oa, _ = attend([q_rows(p * LANES) for p in range(N_PAIRS)],
                   [cache_kv(ca_ref, 0, 0)], [cache_kv(ca_ref, 1, 0)],
                   [na_ref[0, :, 0:LANES]], [na_ref[0, :, LANES:2 * LANES]],
                   ta_ref[...], tn_ref[0], sink_ref[:, 0:1])
    for p in range(N_PAIRS):
        oa_ref[0, p] = fold(oa, p).astype(oa_ref.dtype)

    caches = (cb1_ref, cb2_ref, cb3_ref)
    news = (nb1_ref, nb2_ref, nb3_ref)
    tbls = (tb1_ref, tb2_ref, tb3_ref)
    outs, lses = [], []
    for g in range(N_B_GROUPS):
        o, lse = attend([q_rows(A_WIDTH + g * B_WIDTH + p * LANES) for p in range(N_PAIRS)],
                        [cache_kv(caches[g], 0, p) for p in range(N_PAIRS)],
                        [cache_kv(caches[g], 1, p) for p in range(N_PAIRS)],
                        [news[g][0, :, p * LANES:(p + 1) * LANES] for p in range(N_PAIRS)],
                        [news[g][0, :, B_WIDTH + p * LANES:B_WIDTH + (p + 1) * LANES] for p in range(N_PAIRS)],
                        tbls[g][...], tn_ref[1 + g], None)
        outs.append(o)
        lses.append(jnp.broadcast_to(lse, o.shape))
    mx = jnp.maximum(jnp.maximum(lses[0], lses[1]), lses[2])
    es = [jnp.exp(x - mx) for x in lses]
    den = es[0] + es[1] + es[2]
    comb = (es[0] * outs[0] + es[1] * outs[1] + es[2] * outs[2]) / den
    for p in range(N_PAIRS):
        ob_ref[0, p] = fold(comb, p).astype(ob_ref.dtype)


def _decode_attention(q, news, caches, tbls, tbl_new, sink_rows):
    n, t, _ = q.shape
    seq3 = lambda i: (i, 0, 0)
    seq4 = lambda i: (i, 0, 0, 0)
    seq5 = lambda i: (i, 0, 0, 0, 0)
    in_specs = [pl.BlockSpec((1, t, q.shape[2]), seq3)]
    in_specs += [pl.BlockSpec((1, t, a.shape[2]), seq3) for a in news]
    in_specs += [pl.BlockSpec((1,) + c.shape[1:], seq5) for c in caches]
    in_specs += [pl.BlockSpec(tb.shape, lambda i: (0, 0)) for tb in tbls]
    in_specs += [pl.BlockSpec(tbl_new.shape, lambda i: (0, 0, 0)),
                 pl.BlockSpec(sink_rows.shape, lambda i: (0, 0))]
    slab = pl.BlockSpec((1, N_PAIRS, t, LANES), seq4)
    return pl.pallas_call(
        functools.partial(_decode_kernel, dec_seq=t),
        grid=(n,),
        in_specs=in_specs,
        out_specs=(slab, slab),
        out_shape=(jax.ShapeDtypeStruct((n, N_PAIRS, t, LANES), jnp.float32),
                   jax.ShapeDtypeStruct((n, N_PAIRS, t, LANES), jnp.float32)),
        compiler_params=pltpu.CompilerParams(
            dimension_semantics=("arbitrary",), vmem_limit_bytes=VMEM_LIMIT),
        name="decode_attn",
    )(q, *news, *caches, *tbls, tbl_new, sink_rows)


def _out_kernel(*refs, n_groups):
    x_ref, ng_ref, wg_ref, oa_ref = refs[0:4]
    ob_refs = refs[4:4 + n_groups]
    lse_refs = refs[4 + n_groups:4 + 2 * n_groups] if n_groups > 1 else ()
    wa_ref, wb_ref, wo_ref, y_ref = refs[-4:]
    x = x_ref[0]
    h = _rmsnorm_bf16(x, ng_ref[...])

    def gate(c0, width, silu):
        a = jnp.dot(h, wg_ref[:, c0:c0 + width], preferred_element_type=jnp.float32)
        sg = 1.0 / (1.0 + jnp.exp(-a))
        return a * sg if silu else sg

    def slabs(ref):
        return jnp.concatenate([ref[0, p].astype(jnp.float32) for p in range(N_PAIRS)], axis=1)

    if n_groups > 1:
        parts = []
        for p in range(N_PAIRS):
            lses = [r[0, p] for r in lse_refs]
            mx = functools.reduce(jnp.maximum, lses)
            es = [jnp.exp(v - mx) for v in lses]
            den = functools.reduce(lambda a, b: a + b, es)
            num = functools.reduce(lambda a, b: a + b,
                                   [e * r[0, p].astype(jnp.float32) for e, r in zip(es, ob_refs)])
            parts.append(num / den)
        ob = jnp.concatenate(parts, axis=1)
    else:
        ob = slabs(ob_refs[0])
    oa = slabs(oa_ref)
    ya = jnp.dot((oa * gate(0, A_WIDTH, True)).astype(jnp.bfloat16), wa_ref[...],
                 preferred_element_type=jnp.float32)
    yb = jnp.dot((ob * gate(A_WIDTH, B_WIDTH, True)).astype(jnp.bfloat16), wb_ref[...],
                 preferred_element_type=jnp.float32)
    m0 = A_WIDTH + B_WIDTH
    merged = (gate(m0, D_MODEL, False) * ya + gate(m0 + D_MODEL, D_MODEL, False) * yb).astype(jnp.bfloat16)
    y_ref[0] = x + jnp.dot(merged, wo_ref[...], preferred_element_type=jnp.float32)


def _out_proj(x3d, ng, wg, oa, obs, lses, wa, wb, wo, *, tm):
    nb, seq, _ = x3d.shape
    row = lambda b, i: (b, i, 0)
    slab = pl.BlockSpec((1, N_PAIRS, tm, LANES), lambda b, i: (b, 0, i, 0))
    const = lambda b, i: (0, 0)
    once = dict(pipeline_mode=pl.Buffered(1))
    in_specs = [pl.BlockSpec((1, tm, D_MODEL), row), pl.BlockSpec((1, D_MODEL), const),
                pl.BlockSpec(wg.shape, const, **once), slab]
    in_specs += [slab for _ in obs] + [slab for _ in lses]
    in_specs += [pl.BlockSpec(wa.shape, const, **once), pl.BlockSpec(wb.shape, const, **once),
                 pl.BlockSpec(wo.shape, const, **once)]
    return pl.pallas_call(
        functools.partial(_out_kernel, n_groups=len(obs)),
        grid=(nb, seq // tm),
        in_specs=in_specs,
        out_specs=pl.BlockSpec((1, tm, D_MODEL), row),
        out_shape=jax.ShapeDtypeStruct((nb, seq, D_MODEL), jnp.float32),
        compiler_params=pltpu.CompilerParams(
            dimension_semantics=("arbitrary", "arbitrary"), vmem_limit_bytes=VMEM_LIMIT),
        name="out_proj",
    )(x3d, ng, wg, oa, *obs, *lses, wa, wb, wo)


def _bias_lookup(rel_bias_cols, buckets):
    flat = np.asarray(buckets).reshape(-1)
    onehot = jnp.asarray((flat[:, None] == np.arange(N_BUCKETS)[None, :]).astype(np.float32))
    out = jnp.dot(onehot, rel_bias_cols, precision=HIGHEST)
    return out.reshape(tuple(np.shape(buckets)) + (rel_bias_cols.shape[1],))


def _prep_params(rel_bias, w_in, q_gain_a, k_gain_a, sinks_a, q_gain_b, k_gain_b, w_up_a, w_up_b):
    offs = np.cumsum((0, A_WIDTH, A_KV_WIDTH, A_KV_WIDTH, A_WIDTH, 3 * B_WIDTH, 3 * B_WIDTH, 3 * B_WIDTH,
                      B_WIDTH, D_MODEL, D_MODEL))
    seg = lambda s: w_in[:, int(offs[s]):int(offs[s + 1])]

    def perm_heads(w2d):
        return jnp.concatenate([w2d[:, h * HEAD_DIM:(h + 1) * HEAD_DIM] for h in A_HEAD_ORDER], axis=1)

    w_qkv = jnp.concatenate([perm_heads(seg(0)), seg(4), seg(1), seg(5), seg(2), seg(6)],
                            axis=1).astype(jnp.bfloat16)
    w_gate = jnp.concatenate([perm_heads(seg(3)), seg(7), seg(8), seg(9)], axis=1).astype(jnp.bfloat16)
    gq = jnp.concatenate([jnp.tile(q_gain_a, A_Q_HEADS)] +
                         [jnp.tile(q_gain_b[g], B_HEADS) for g in range(N_B_GROUPS)])[None, :] * Q_SCALE
    gk = jnp.concatenate([jnp.tile(k_gain_a, A_KV_HEADS)] +
                         [jnp.tile(k_gain_b[g], B_HEADS) for g in range(N_B_GROUPS)])[None, :]
    hd = np.arange(MXU_COLS) // HEAD_DIM
    ones = jnp.asarray((hd[:, None] == hd[None, :]).astype(np.float32) / HEAD_DIM, jnp.bfloat16)
    wa = jnp.concatenate([w_up_a[h * HEAD_DIM:(h + 1) * HEAD_DIM] for h in A_HEAD_ORDER], axis=0)
    head_cols = [np.asarray(A_HEAD_ORDER)] + [A_Q_HEADS + g * B_HEADS + np.arange(B_HEADS)
                                              for g in range(N_B_GROUPS)]
    rb_cols = []
    for cols in head_cols:
        sel = np.zeros((rel_bias.shape[1], len(cols)), np.float32)
        sel[cols, np.arange(len(cols))] = 1.0
        rb_cols.append(jnp.dot(rel_bias, jnp.asarray(sel), precision=HIGHEST))
    dils = (1,) + tuple(d for _, d in B_GROUPS)
    qi = np.arange(N_KEYS)[:, None]
    ci = np.arange(N_KEYS)[None, :]
    steps = (qi - ci) % N_KEYS
    band_bias = []
    for d, rb in zip(dils, rb_cols):
        b = jnp.transpose(_bias_lookup(rb, _t5_bucket_np(steps * d)), (2, 0, 1))
        band_bias.append(b.reshape(N_PAIRS, 2 * N_KEYS, N_KEYS))
    sink_perm = jnp.dot(sinks_a[None, :], jnp.asarray(np.eye(A_Q_HEADS, dtype=np.float32)[:, list(A_HEAD_ORDER)]),
                        precision=HIGHEST)[0]
    return (w_qkv, w_gate, gq, gk, ones, wa.astype(jnp.bfloat16), w_up_b.astype(jnp.bfloat16), band_bias,
            sink_perm, rb_cols, dils)


def _decode_tables(rb_cols, sink_perm, dils, cache_lens, dec_seq):
    t_idx = np.arange(dec_seq)
    tbls = []
    new_rows = []
    for d, rb, ln in zip(dils, rb_cols, cache_lens):
        nh = rb.shape[1]
        pos = np.arange(ln)
        back = ln + t_idx[:, None] - pos[None, :]
        ok = (back % d == 0) & (back // d >= 1) & (back // d < N_KEYS)
        b = _bias_lookup(rb, _t5_bucket_np(np.where(ok, back, 0)))
        b = b + jnp.asarray(np.where(ok, 0.0, NEG_INF).astype(np.float32))[:, :, None]
        tbls.append(jnp.transpose(b, (2, 0, 1)).reshape(nh * dec_seq, ln))
        backn = t_idx[:, None] - t_idx[None, :]
        okn = (backn >= 0) & (backn % d == 0) & (backn // d < N_KEYS)
        bn = _bias_lookup(rb, _t5_bucket_np(np.where(okn, backn, 0)))
        bn = bn + jnp.asarray(np.where(okn, 0.0, NEG_INF).astype(np.float32))[:, :, None]
        new_rows.append(jnp.transpose(bn, (2, 0, 1)).reshape(nh * dec_seq, dec_seq))
    tbl_new = jnp.stack(new_rows)
    sink_rows = jnp.broadcast_to(jnp.repeat(sink_perm, dec_seq)[:, None], (A_Q_HEADS * dec_seq, LANES))
    return tbls, tbl_new, sink_rows


def kernel(x_prompt, x_sample, cache_a_kv, cache_b1_kv, cache_b2_kv, cache_b3_kv, rel_bias, norm_gain, w_in,
           q_gain_a, k_gain_a, sinks_a, q_gain_b, k_gain_b, w_up_a, w_up_b, w_out):
    assert norm_gain.shape[0] == 1, "single layer"
    nb, seq, _ = x_prompt.shape
    n_dec, dec_seq, _ = x_sample.shape
    (w_qkv, w_gate, gq, gk, ones, wa, wb, band_bias, sink_perm, rb_cols, dils) = _prep_params(
        rel_bias, w_in[0], q_gain_a[0], k_gain_a[0], sinks_a[0], q_gain_b[0], k_gain_b[0], w_up_a[0], w_up_b[0])
    wo = w_out[0].astype(jnp.bfloat16)
    ng = norm_gain
    windows = (A_WINDOW,) + tuple(w for w, _ in B_GROUPS)
    sink2 = jnp.broadcast_to(sink_perm.reshape(N_PAIRS, 2, 1, 1),
                             (N_PAIRS, 2, N_KEYS, LANES)).reshape(N_PAIRS, 2 * N_KEYS, LANES)

    p_rows = tuple(min(w, seq) for w in windows)
    qs, ks, vs, states = _qkv_proj(x_prompt.reshape(nb * seq, D_MODEL), ng, w_qkv, gq, gk, ones,
                                   nb=nb, seq=seq, state_rows=p_rows, dils=dils, tm=512)
    (oa,) = _band_attention(qs[0], ks[0], vs[0], band_bias[0], sink2,
                            shared_kv=True, has_sink=True, want_lse=False)
    obs, lses = [], []
    for g in range(1, 4):
        og, lg = _band_attention(qs[g], ks[g], vs[g], band_bias[g], sink2,
                                 shared_kv=False, has_sink=False, want_lse=True)
        obs.append(og)
        lses.append(lg)
    y_prompt = _out_proj(x_prompt, ng, w_gate, oa, obs, lses, wa, wb, wo, tm=512)
    heads = (A_KV_HEADS, B_HEADS, B_HEADS, B_HEADS)
    new_prompt = tuple(states[g].reshape(1, nb, p_rows[g], 2, heads[g], HEAD_DIM) for g in range(4))

    t_dec = n_dec * dec_seq
    qd, _, _, news = _qkv_proj(x_sample.reshape(t_dec, D_MODEL), ng, w_qkv, gq, gk, ones,
                               nb=1, seq=t_dec, state_rows=(t_dec,) * 4, dils=(1, 1, 1, 1), tm=t_dec)
    q_dec = jnp.concatenate([a.reshape(n_dec, dec_seq, A_WIDTH) for a in qd], axis=2)
    caches = [jnp.transpose(c[0], (0, 2, 3, 4, 1)) for c in (cache_a_kv, cache_b1_kv, cache_b2_kv, cache_b3_kv)]
    cache_lens = tuple(c.shape[-1] for c in caches)
    tbls, tbl_new, sink_rows = _decode_tables(rb_cols, sink_perm, dils, cache_lens, dec_seq)
    news3 = [a.reshape(n_dec, dec_seq, a.shape[1]) for a in news]
    oa_s, ob_s = _decode_attention(q_dec, news3, caches, tbls, tbl_new, sink_rows)
    to_slabs = lambda o: jnp.transpose(o, (1, 0, 2, 3)).reshape(1, N_PAIRS, t_dec, LANES)
    y_sample = _out_proj(x_sample.reshape(1, t_dec, D_MODEL), ng, w_gate, to_slabs(oa_s), [to_slabs(ob_s)], [],
                         wa, wb, wo, tm=t_dec)
    y_sample = y_sample.reshape(n_dec, dec_seq, D_MODEL)
    new_sample = tuple(news3[g].reshape(1, n_dec, dec_seq, 2, heads[g], HEAD_DIM) for g in range(4))
    return (y_prompt, y_sample) + new_prompt + new_sample
```

```python
import functools
import math

import numpy as np
import jax
import jax.numpy as jnp
from jax import lax
from jax.experimental import pallas as pl
from jax.experimental.pallas import tpu as pltpu

D_MODEL = 1024
HEAD_DIM = 64
A_Q_HEADS = 8
A_KV_HEADS = 2
A_WINDOW = 128
B_GROUPS = ((128, 1), (512, 4), (2048, 16))
N_B_GROUPS = 3
B_HEADS = 8
N_KEYS = 128
A_WIDTH = A_Q_HEADS * HEAD_DIM
A_KV_WIDTH = A_KV_HEADS * HEAD_DIM
B_WIDTH = B_HEADS * HEAD_DIM
N_BUCKETS = 32
MAX_DISTANCE = 2048
EPS = 1e-6
NEG_INF = -1e30
Q_SCALE = HEAD_DIM ** -0.5

LANES = 128
MXU_COLS = 256
VMEM_LIMIT = 56 * 1024 * 1024
N_PAIRS = A_WIDTH // LANES

A_HEAD_ORDER = (0, 4, 1, 5, 2, 6, 3, 7)

Q_COLS = A_WIDTH + N_B_GROUPS * B_WIDTH
KV_COLS = A_KV_WIDTH + N_B_GROUPS * B_WIDTH
QKV_COLS = Q_COLS + 2 * KV_COLS
G_COLS = A_WIDTH + B_WIDTH + 2 * D_MODEL
KV_WIDTHS = (A_KV_WIDTH, B_WIDTH, B_WIDTH, B_WIDTH)
NORM_BATCH = 4 * B_WIDTH // MXU_COLS
HIGHEST = lax.Precision.HIGHEST


def _t5_bucket_np(dist):
    max_exact = N_BUCKETS // 2
    d = np.maximum(dist, 0)
    df = np.maximum(d, 1).astype(np.float32)
    large = max_exact + (np.log(df / np.float32(max_exact)) / np.float32(math.log(MAX_DISTANCE / max_exact))
                         * np.float32(N_BUCKETS - max_exact)).astype(np.int32)
    large = np.minimum(large, N_BUCKETS - 1)
    return np.where(d < max_exact, d, large)


def _rmsnorm_bf16(x, gain):
    ms = jnp.mean(x * x, axis=-1, keepdims=True)
    return (x * lax.rsqrt(ms + EPS) * gain).astype(jnp.bfloat16)


def _state_plan(n_rows, seq, tm):
    tpb = seq // tm
    r = min(n_rows, tm)
    nblk = max(n_rows // tm, 1)
    return tpb, r, nblk, tpb - nblk


def _qkv_kernel(x_ref, ng_ref, w_ref, gq_ref, gk_ref, ones_ref, *rest, tm, seq, state_rows, dils, states_t):
    q_refs = rest[0:4]
    k_refs = rest[4:8]
    v_refs = rest[8:12]
    state_refs = rest[12:16]
    tmp_ref, a_scr, sq_scr, ss_scr = rest[16:20]
    h = _rmsnorm_bf16(x_ref[...], ng_ref[...])
    ones = ones_ref[...]

    def proj(col0, w):
        return jnp.dot(h, w_ref[:, col0:col0 + w], preferred_element_type=jnp.float32)

    def emit(a, c, out_ref, d, state):
        w = a.shape[1]
        if d == 1:
            out_ref[0, 0, :, c:c + w] = a.astype(out_ref.dtype)
        else:
            for s in range(w // LANES):
                tmp_ref[c // LANES + s] = a[:, s * LANES:(s + 1) * LANES]
        if state is not None:
            sref, kv, r = state
            if states_t:
                sref[0, kv, c:c + w, :] = a[tm - r:, :].T
            else:
                wd = sref.shape[1] // 2
                sref[:, kv * wd + c:kv * wd + c + w] = a[tm - r:, :]

    def finish(out_ref, d, width):
        if d > 1:
            for cls in range(d):
                for s in range(width // LANES):
                    out_ref[0, cls, :, s * LANES:(s + 1) * LANES] = (
                        tmp_ref[s, pl.ds(cls, tm // d, stride=d), :].astype(out_ref.dtype))

    def normed_batch(arrays):
        chunks = [(ai, c) for ai in range(len(arrays)) for c in range(0, B_WIDTH, MXU_COLS)]
        for ci, (ai, c) in enumerate(chunks):
            a = proj(arrays[ai][0] + c, MXU_COLS)
            a_scr[ci] = a
            sq_scr[ci * tm:(ci + 1) * tm, :] = (a * a).astype(jnp.bfloat16)
        n = len(chunks) * tm
        ss_scr[0:n, :] = jnp.dot(sq_scr[0:n, :], ones, preferred_element_type=jnp.float32)
        for ci, (ai, c) in enumerate(chunks):
            _, gain_ref, gcol0, out_ref, d, state = arrays[ai]
            o = (a_scr[ci] * lax.rsqrt(ss_scr[ci * tm:(ci + 1) * tm, :] + EPS)
                 * gain_ref[:, gcol0 + c:gcol0 + c + MXU_COLS])
            emit(o, c, out_ref, d, state)
            if c + MXU_COLS == B_WIDTH:
                finish(out_ref, d, B_WIDTH)

    q_off = (0, A_WIDTH, A_WIDTH + B_WIDTH, A_WIDTH + 2 * B_WIDTH)
    kv_off = (0, A_KV_WIDTH, A_KV_WIDTH + B_WIDTH, A_KV_WIDTH + 2 * B_WIDTH)
    plan = [_state_plan(state_rows[g], seq, tm)[1] for g in range(4)]
    normed_batch([(q_off[g], gq_ref, q_off[g], q_refs[g], dils[g], None) for g in range(4)])
    normed_batch([(Q_COLS + kv_off[g], gk_ref, kv_off[g], k_refs[g], dils[g], (state_refs[g], 0, plan[g]))
                  for g in range(1, 4)])
    a = proj(Q_COLS, A_KV_WIDTH)
    ss = jnp.dot((a * a).astype(jnp.bfloat16), ones[:A_KV_WIDTH, :A_KV_WIDTH], preferred_element_type=jnp.float32)
    emit(a * lax.rsqrt(ss + EPS) * gk_ref[:, 0:A_KV_WIDTH], 0, k_refs[0], dils[0], (state_refs[0], 0, plan[0]))
    for g in range(4):
        wd = KV_WIDTHS[g]
        for c in range(0, wd, MXU_COLS):
            w = min(MXU_COLS, wd - c)
            emit(proj(Q_COLS + KV_COLS + kv_off[g] + c, w), c, v_refs[g], dils[g], (state_refs[g], 1, plan[g]))
        finish(v_refs[g], dils[g], wd)


def _qkv_proj(x2d, ng, w_qkv, gq, gk, ones, *, nb, seq, state_rows, dils, tm, states_t):
    t = x2d.shape[0]
    tpb = seq // tm

    def cls_shape(d, width):
        return jax.ShapeDtypeStruct((nb, d, seq // d, width), jnp.bfloat16)

    def cls_spec(d, width):
        return pl.BlockSpec((1, d, tm // d, width), lambda i: (i // tpb, 0, i % tpb, 0))

    def state_shape(g):
        if states_t:
            return jax.ShapeDtypeStruct((nb, 2, KV_WIDTHS[g], state_rows[g]), jnp.float32)
        return jax.ShapeDtypeStruct((nb * state_rows[g], 2 * KV_WIDTHS[g]), jnp.float32)

    def state_spec(g):
        _, r, nblk, j0 = _state_plan(state_rows[g], seq, tm)
        if states_t:
            return pl.BlockSpec((1, 2, KV_WIDTHS[g], r), lambda i: (i // tpb, 0, 0, jnp.maximum(i % tpb - j0, 0)))
        return pl.BlockSpec((r, 2 * KV_WIDTHS[g]),
                            lambda i: ((i // tpb) * nblk + jnp.maximum(i % tpb - j0, 0), 0))

    const = lambda i: (0, 0)
    out_shape = ([cls_shape(d, A_WIDTH) for d in dils]
                 + [cls_shape(d, w) for d, w in zip(dils, KV_WIDTHS)] * 2
                 + [state_shape(g) for g in range(4)])
    out_specs = ([cls_spec(d, A_WIDTH) for d in dils]
                 + [cls_spec(d, w) for d, w in zip(dils, KV_WIDTHS)] * 2
                 + [state_spec(g) for g in range(4)])
    res = pl.pallas_call(
        functools.partial(_qkv_kernel, tm=tm, seq=seq, state_rows=state_rows, dils=dils, states_t=states_t),
        grid=(t // tm,),
        in_specs=[
            pl.BlockSpec((tm, D_MODEL), lambda i: (i, 0)),
            pl.BlockSpec((1, D_MODEL), const),
            pl.BlockSpec((D_MODEL, QKV_COLS), const, pipeline_mode=pl.Buffered(1)),
            pl.BlockSpec((1, Q_COLS), const),
            pl.BlockSpec((1, KV_COLS), const),
            pl.BlockSpec((MXU_COLS, MXU_COLS), const),
        ],
        out_specs=out_specs,
        out_shape=out_shape,
        scratch_shapes=[pltpu.VMEM((N_PAIRS, tm, LANES), jnp.float32),
                        pltpu.VMEM((NORM_BATCH, tm, MXU_COLS), jnp.float32),
                        pltpu.VMEM((NORM_BATCH * tm, MXU_COLS), jnp.bfloat16),
                        pltpu.VMEM((NORM_BATCH * tm, MXU_COLS), jnp.float32)],
        compiler_params=pltpu.CompilerParams(
            dimension_semantics=("arbitrary",), vmem_limit_bytes=VMEM_LIMIT),
        name="qkv_proj",
    )(x2d, ng, w_qkv, gq, gk, ones)
    return res[0:4], res[4:8], res[8:12], res[12:16]


def _band_attn_kernel(q_ref, k_ref, kp_ref, v_ref, vp_ref, bias_ref, sink_ref, *rest,
                      tt, dil, shared_kv, has_sink, want_lse):
    if want_lse:
        o_ref, lse_ref, kbuf, vbuf, o_scr = rest
    else:
        o_ref, kbuf, vbuf, o_scr = rest
        lse_ref = None
    blk = N_KEYS
    i = pl.program_id(1)
    nblk = tt // dil // blk
    kbuf[:, 0:blk, :] = kp_ref[0]
    kbuf[:, blk:, :] = k_ref[0]
    vbuf[:, 0:blk, :] = vp_ref[0]
    vbuf[:, blk:, :] = v_ref[0]

    row = lax.broadcasted_iota(jnp.int32, (2 * blk, blk), 0)
    lane = lax.broadcasted_iota(jnp.int32, (2 * blk, blk), 1)
    ahead = lane - jnp.bitwise_and(row, blk - 1)
    tri = ahead <= 0
    cur_part = tri.astype(jnp.bfloat16)
    prev_part = jnp.logical_not(tri).astype(jnp.bfloat16)
    lane1 = lax.broadcasted_iota(jnp.int32, (blk, LANES), 1)
    left = lane1 < HEAD_DIM
    mask_l = left.astype(jnp.bfloat16)
    mask_r = jnp.logical_not(left).astype(jnp.bfloat16)
    ones_kv = jnp.ones((2 * blk, LANES), jnp.bfloat16)
    nt = (((1,), (1,)), ((), ()))

    def rows_of(cls, r0):
        start = cls + dil * r0
        return pl.ds(start, blk) if dil == 1 else pl.ds(start, blk, stride=dil)

    def block(c, carry, *, first_tile):
        cls = c // nblk
        j = c % nblk
        r0 = pl.multiple_of(j * blk, blk)
        if first_tile:
            valid = ahead <= jnp.where(j == 0, 0, blk)
        for p in range(N_PAIRS):
            kcol = 0 if shared_kv else p * LANES
            qp = q_ref[0, cls, pl.ds(r0, blk), p * LANES:(p + 1) * LANES]
            qs = jnp.concatenate([qp * mask_l, qp * mask_r], axis=0)
            kc = kbuf[cls, pl.ds(r0, 2 * blk), kcol:kcol + LANES]
            vc = vbuf[cls, pl.ds(r0, 2 * blk), kcol:kcol + LANES]
            s2 = lax.dot_general(qs, kc, nt, preferred_element_type=jnp.float32)
            s = jnp.where(tri, s2[:, blk:], s2[:, :blk]) + bias_ref[p]
            if first_tile:
                s = jnp.where(valid, s, NEG_INF)
            m = jnp.max(s, axis=-1, keepdims=True)
            if has_sink:
                sk = sink_ref[p]
                m = jnp.maximum(m, sk)
            pb = jnp.exp(s - m).astype(jnp.bfloat16)
            p2 = jnp.concatenate([pb * prev_part, pb * cur_part], axis=1)
            ov = jnp.dot(p2, jnp.concatenate([vc, ones_kv], axis=1), preferred_element_type=jnp.float32)
            l = ov[:, LANES:]
            if has_sink:
                l = l + jnp.exp(sk - m)
            num = jnp.where(left, ov[:blk, :LANES], ov[blk:, :LANES])
            den = jnp.where(left, l[:blk], l[blk:])
            o_scr[p, rows_of(cls, r0), :] = num / den
            if want_lse:
                m1 = jnp.where(left, m[:blk], m[blk:])
                lse_ref[0, p, rows_of(cls, r0), :] = m1 + jnp.log(den)
        return carry

    n_blocks = dil * nblk

    @pl.when(i == 0)
    def _():
        lax.fori_loop(0, n_blocks, functools.partial(block, first_tile=True), 0, unroll=min(4, n_blocks))

    @pl.when(i > 0)
    def _():
        lax.fori_loop(0, n_blocks, functools.partial(block, first_tile=False), 0, unroll=min(4, n_blocks))

    for p in range(N_PAIRS):
        o_ref[0, p] = o_scr[p].astype(o_ref.dtype)


def _band_attention(q, k, v, bias2, sink2, *, shared_kv, has_sink, want_lse):
    nb, dil, m, wq = q.shape
    wkv = k.shape[3]
    seq = dil * m
    tt = max(min(seq, 1024), dil * N_KEYS)
    nblk_prev = tt // dil // N_KEYS
    cur = lambda b, i: (b, 0, i, 0)
    prev = lambda b, i: (b, 0, jnp.maximum(i * nblk_prev - 1, 0), 0)
    slab = pl.BlockSpec((1, N_PAIRS, tt, LANES), cur)
    out_shape = [jax.ShapeDtypeStruct((nb, N_PAIRS, seq, LANES), jnp.bfloat16)]
    out_specs = [slab]
    if want_lse:
        out_shape.append(jax.ShapeDtypeStruct((nb, N_PAIRS, seq, LANES), jnp.float32))
        out_specs.append(slab)
    return pl.pallas_call(
        functools.partial(_band_attn_kernel, tt=tt, dil=dil, shared_kv=shared_kv, has_sink=has_sink,
                          want_lse=want_lse),
        grid=(nb, seq // tt),
        in_specs=[
            pl.BlockSpec((1, dil, tt // dil, wq), cur),
            pl.BlockSpec((1, dil, tt // dil, wkv), cur),
            pl.BlockSpec((1, dil, N_KEYS, wkv), prev),
            pl.BlockSpec((1, dil, tt // dil, wkv), cur),
            pl.BlockSpec((1, dil, N_KEYS, wkv), prev),
            pl.BlockSpec(bias2.shape, lambda b, i: (0, 0, 0)),
            pl.BlockSpec(sink2.shape, lambda b, i: (0, 0, 0)),
        ],
        out_specs=out_specs,
        out_shape=out_shape,
        scratch_shapes=[pltpu.VMEM((dil, tt // dil + N_KEYS, wkv), jnp.bfloat16),
                        pltpu.VMEM((dil, tt // dil + N_KEYS, wkv), jnp.bfloat16),
                        pltpu.VMEM((N_PAIRS, tt, LANES), jnp.float32)],
        compiler_params=pltpu.CompilerParams(
            dimension_semantics=("arbitrary", "arbitrary"), vmem_limit_bytes=VMEM_LIMIT),
        name="band_attn_d%d" % dil,
    )(q, k, k, v, v, bias2, sink2)


def _decode_kernel(q_ref, na_ref, nb1_ref, nb2_ref, nb3_ref, ca_ref, cb1_ref, cb2_ref, cb3_ref,
                   ta_ref, tb1_ref, tb2_ref, tb3_ref, tn_ref, sink_ref, oa_ref, ob_ref, *, dec_seq):
    t = dec_seq
    left8 = lax.broadcasted_iota(jnp.int32, (2 * t, LANES), 1) < HEAD_DIM
    top8 = lax.broadcasted_iota(jnp.int32, (2 * t, LANES), 0) < t
    own = (left8 == top8).astype(jnp.float32)
    left4 = lax.broadcasted_iota(jnp.int32, (t, LANES), 1) < HEAD_DIM
    nt = (((1,), (1,)), ((), ()))

    def q_rows(c0):
        qp = q_ref[0, :, c0:c0 + LANES].astype(jnp.float32)
        return jnp.concatenate([qp, qp], axis=0) * own

    def attend(qrs, kts, vts, knews, vnews, tbl, tbl_new, sink):
        shared = len(kts) == 1
        rows8 = lambda a, u: jnp.broadcast_to(a[u:u + 1, :], (2 * t, LANES))
        qr = jnp.concatenate(qrs, axis=0)
        qb = qr.astype(jnp.bfloat16)
        if shared:
            s_c = jnp.dot(qb, kts[0], preferred_element_type=jnp.float32)
        else:
            s_c = jnp.concatenate([jnp.dot(qb[2 * t * p:2 * t * (p + 1)], kts[p], preferred_element_type=jnp.float32)
                                   for p in range(N_PAIRS)], axis=0)
        s_c = s_c + tbl
        kn = [jnp.concatenate([rows8(knews[0 if shared else p], u) for p in range(N_PAIRS)], axis=0)
              for u in range(t)]
        vn = [jnp.concatenate([rows8(vnews[0 if shared else p], u) for p in range(N_PAIRS)], axis=0)
              for u in range(t)]
        s_n = [jnp.sum(qr * kn[u], axis=-1, keepdims=True) + tbl_new[:, u:u + 1] for u in range(t)]
        m = functools.reduce(jnp.maximum, s_n, jnp.max(s_c, axis=-1, keepdims=True))
        if sink is not None:
            m = jnp.maximum(m, sink)
        pc = jnp.exp(s_c - m)
        pn = [jnp.exp(x - m) for x in s_n]
        l = functools.reduce(lambda a, b: a + b, pn, jnp.sum(pc, axis=-1, keepdims=True))
        if sink is not None:
            l = l + jnp.exp(sink - m)
        pb = pc.astype(jnp.bfloat16)
        if shared:
            o = lax.dot_general(pb, vts[0], nt, preferred_element_type=jnp.float32)
        else:
            o = jnp.concatenate([lax.dot_general(pb[2 * t * p:2 * t * (p + 1)], vts[p], nt,
                                                 preferred_element_type=jnp.float32)
                                 for p in range(N_PAIRS)], axis=0)
        for u in range(t):
            o = o + pn[u] * vn[u]
        return o / l, m + jnp.log(l)

    def fold(x, p):
        return jnp.where(left4, x[2 * t * p:2 * t * p + t], x[2 * t * p + t:2 * t * (p + 1)])

    def cache_kv(c_ref, kv, p):
        return c_ref[0, kv, 2 * p:2 * p + 2].reshape(LANES, c_ref.shape[-1]).astype(jnp.bfloat16)

    oa, _ = attend([q_rows(p * LANES) for p in range(N_PAIRS)],
                   [cache_kv(ca_ref, 0, 0)], [cache_kv(ca_ref, 1, 0)],
                   [na_ref[0, :, 0:LANES]], [na_ref[0, :, LANES:2 * LANES]],
                   ta_ref[...], tn_ref[0], sink_ref[:, 0:1])
    for p in range(N_PAIRS):
        oa_ref[0, p] = fold(oa, p).astype(oa_ref.dtype)

    caches = (cb1_ref, cb2_ref, cb3_ref)
    news = (nb1_ref, nb2_ref, nb3_ref)
    tbls = (tb1_ref, tb2_ref, tb3_ref)
    outs, lses = [], []
    for g in range(N_B_GROUPS):
        o, lse = attend([q_rows(A_WIDTH + g * B_WIDTH + p * LANES) for p in range(N_PAIRS)],
                        [cache_kv(caches[g], 0, p) for p in range(N_PAIRS)],
                        [cache_kv(caches[g], 1, p) for p in range(N_PAIRS)],
                        [news[g][0, :, p * LANES:(p + 1) * LANES] for p in range(N_PAIRS)],
                        [news[g][0, :, B_WIDTH + p * LANES:B_WIDTH + (p + 1) * LANES] for p in range(N_PAIRS)],
                        tbls[g][...], tn_ref[1 + g], None)
        outs.append(o)
        lses.append(jnp.broadcast_to(lse, o.shape))
    mx = jnp.maximum(jnp.maximum(lses[0], lses[1]), lses[2])
    es = [jnp.exp(x - mx) for x in lses]
    den = es[0] + es[1] + es[2]
    comb = (es[0] * outs[0] + es[1] * outs[1] + es[2] * outs[2]) / den
    for p in range(N_PAIRS):
        ob_ref[0, p] = fold(comb, p).astype(ob_ref.dtype)


def _decode_attention(q, news, caches, tbls, tbl_new, sink_rows):
    n, t, _ = q.shape
    seq3 = lambda i: (i, 0, 0)
    seq4 = lambda i: (i, 0, 0, 0)
    seq5 = lambda i: (i, 0, 0, 0, 0)
    in_specs = [pl.BlockSpec((1, t, q.shape[2]), seq3)]
    in_specs += [pl.BlockSpec((1, t, a.shape[2]), seq3) for a in news]
    in_specs += [pl.BlockSpec((1,) + c.shape[1:], seq5) for c in caches]
    in_specs += [pl.BlockSpec(tb.shape, lambda i: (0, 0)) for tb in tbls]
    in_specs += [pl.BlockSpec(tbl_new.shape, lambda i: (0, 0, 0)),
                 pl.BlockSpec(sink_rows.shape, lambda i: (0, 0))]
    slab = pl.BlockSpec((1, N_PAIRS, t, LANES), seq4)
    return pl.pallas_call(
        functools.partial(_decode_kernel, dec_seq=t),
        grid=(n,),
        in_specs=in_specs,
        out_specs=(slab, slab),
        out_shape=(jax.ShapeDtypeStruct((n, N_PAIRS, t, LANES), jnp.float32),
                   jax.ShapeDtypeStruct((n, N_PAIRS, t, LANES), jnp.float32)),
        compiler_params=pltpu.CompilerParams(
            dimension_semantics=("arbitrary",), vmem_limit_bytes=VMEM_LIMIT),
        name="decode_attn",
    )(q, *news, *caches, *tbls, tbl_new, sink_rows)


def _out_kernel(*refs, n_groups):
    x_ref, ng_ref, wg_ref, oa_ref = refs[0:4]
    ob_refs = refs[4:4 + n_groups]
    lse_refs = refs[4 + n_groups:4 + 2 * n_groups] if n_groups > 1 else ()
    wa_ref, wb_ref, wo_ref, y_ref = refs[-4:]
    x = x_ref[0]
    h = _rmsnorm_bf16(x, ng_ref[...])

    def gate(c0, width, silu):
        a = jnp.dot(h, wg_ref[:, c0:c0 + width], preferred_element_type=jnp.float32)
        sg = 1.0 / (1.0 + jnp.exp(-a))
        return a * sg if silu else sg

    def slabs(ref):
        return jnp.concatenate([ref[0, p].astype(jnp.float32) for p in range(N_PAIRS)], axis=1)

    if n_groups > 1:
        parts = []
        for p in range(N_PAIRS):
            lses = [r[0, p] for r in lse_refs]
            mx = functools.reduce(jnp.maximum, lses)
            es = [jnp.exp(v - mx) for v in lses]
            den = functools.reduce(lambda a, b: a + b, es)
            num = functools.reduce(lambda a, b: a + b,
                                   [e * r[0, p].astype(jnp.float32) for e, r in zip(es, ob_refs)])
            parts.append(num / den)
        ob = jnp.concatenate(parts, axis=1)
    else:
        ob = slabs(ob_refs[0])
    oa = slabs(oa_ref)
    ya = jnp.dot((oa * gate(0, A_WIDTH, True)).astype(jnp.bfloat16), wa_ref[...],
                 preferred_element_type=jnp.float32)
    yb = jnp.dot((ob * gate(A_WIDTH, B_WIDTH, True)).astype(jnp.bfloat16), wb_ref[...],
                 preferred_element_type=jnp.float32)
    m0 = A_WIDTH + B_WIDTH
    merged = (gate(m0, D_MODEL, False) * ya + gate(m0 + D_MODEL, D_MODEL, False) * yb).astype(jnp.bfloat16)
    y_ref[0] = x + jnp.dot(merged, wo_ref[...], preferred_element_type=jnp.float32)


def _out_proj(x3d, ng, wg, oa, obs, lses, wa, wb, wo, *, tm):
    nb, seq, _ = x3d.shape
    row = lambda b, i: (b, i, 0)
    slab = pl.BlockSpec((1, N_PAIRS, tm, LANES), lambda b, i: (b, 0, i, 0))
    const = lambda b, i: (0, 0)
    once = dict(pipeline_mode=pl.Buffered(1))
    in_specs = [pl.BlockSpec((1, tm, D_MODEL), row), pl.BlockSpec((1, D_MODEL), const),
                pl.BlockSpec(wg.shape, const, **once), slab]
    in_specs += [slab for _ in obs] + [slab for _ in lses]
    in_specs += [pl.BlockSpec(wa.shape, const, **once), pl.BlockSpec(wb.shape, const, **once),
                 pl.BlockSpec(wo.shape, const, **once)]
    return pl.pallas_call(
        functools.partial(_out_kernel, n_groups=len(obs)),
        grid=(nb, seq // tm),
        in_specs=in_specs,
        out_specs=pl.BlockSpec((1, tm, D_MODEL), row),
        out_shape=jax.ShapeDtypeStruct((nb, seq, D_MODEL), jnp.float32),
        compiler_params=pltpu.CompilerParams(
            dimension_semantics=("arbitrary", "arbitrary"), vmem_limit_bytes=VMEM_LIMIT),
        name="out_proj",
    )(x3d, ng, wg, oa, *obs, *lses, wa, wb, wo)


def _bias_lookup(rel_bias_cols, buckets):
    flat = np.asarray(buckets).reshape(-1)
    onehot = jnp.asarray((flat[:, None] == np.arange(N_BUCKETS)[None, :]).astype(np.float32))
    out = jnp.dot(onehot, rel_bias_cols, precision=HIGHEST)
    return out.reshape(tuple(np.shape(buckets)) + (rel_bias_cols.shape[1],))


def _prep_params(rel_bias, w_in, q_gain_a, k_gain_a, sinks_a, q_gain_b, k_gain_b, w_up_a, w_up_b):
    offs = np.cumsum((0, A_WIDTH, A_KV_WIDTH, A_KV_WIDTH, A_WIDTH, 3 * B_WIDTH, 3 * B_WIDTH, 3 * B_WIDTH,
                      B_WIDTH, D_MODEL, D_MODEL))
    seg = lambda s: w_in[:, int(offs[s]):int(offs[s + 1])]

    def perm_heads(w2d):
        return jnp.concatenate([w2d[:, h * HEAD_DIM:(h + 1) * HEAD_DIM] for h in A_HEAD_ORDER], axis=1)

    w_qkv = jnp.concatenate([perm_heads(seg(0)), seg(4), seg(1), seg(5), seg(2), seg(6)],
                            axis=1).astype(jnp.bfloat16)
    w_gate = jnp.concatenate([perm_heads(seg(3)), seg(7), seg(8), seg(9)], axis=1).astype(jnp.bfloat16)
    gq = jnp.concatenate([jnp.tile(q_gain_a, A_Q_HEADS)] +
                         [jnp.tile(q_gain_b[g], B_HEADS) for g in range(N_B_GROUPS)])[None, :] * Q_SCALE
    gk = jnp.concatenate([jnp.tile(k_gain_a, A_KV_HEADS)] +
                         [jnp.tile(k_gain_b[g], B_HEADS) for g in range(N_B_GROUPS)])[None, :]
    hd = np.arange(MXU_COLS) // HEAD_DIM
    ones = jnp.asarray((hd[:, None] == hd[None, :]).astype(np.float32) / HEAD_DIM, jnp.bfloat16)
    wa = jnp.concatenate([w_up_a[h * HEAD_DIM:(h + 1) * HEAD_DIM] for h in A_HEAD_ORDER], axis=0)
    head_cols = [np.asarray(A_HEAD_ORDER)] + [A_Q_HEADS + g * B_HEADS + np.arange(B_HEADS)
                                              for g in range(N_B_GROUPS)]
    rb_cols = []
    for cols in head_cols:
        sel = np.zeros((rel_bias.shape[1], len(cols)), np.float32)
        sel[cols, np.arange(len(cols))] = 1.0
        rb_cols.append(jnp.dot(rel_bias, jnp.asarray(sel), precision=HIGHEST))
    dils = (1,) + tuple(d for _, d in B_GROUPS)
    qi = np.arange(N_KEYS)[:, None]
    ci = np.arange(N_KEYS)[None, :]
    steps = (qi - ci) % N_KEYS
    band_bias = []
    for d, rb in zip(dils, rb_cols):
        b = jnp.transpose(_bias_lookup(rb, _t5_bucket_np(steps * d)), (2, 0, 1))
        band_bias.append(b.reshape(N_PAIRS, 2 * N_KEYS, N_KEYS))
    sink_perm = jnp.dot(sinks_a[None, :], jnp.asarray(np.eye(A_Q_HEADS, dtype=np.float32)[:, list(A_HEAD_ORDER)]),
                        precision=HIGHEST)[0]
    return (w_qkv, w_gate, gq, gk, ones, wa.astype(jnp.bfloat16), w_up_b.astype(jnp.bfloat16), band_bias,
            sink_perm, rb_cols, dils)


def _decode_tables(rb_cols, sink_perm, dils, cache_lens, dec_seq):
    t_idx = np.arange(dec_seq)
    tbls = []
    new_rows = []
    for d, rb, ln in zip(dils, rb_cols, cache_lens):
        nh = rb.shape[1]
        pos = np.arange(ln)
        back = ln + t_idx[:, None] - pos[None, :]
        ok = (back % d == 0) & (back // d >= 1) & (back // d < N_KEYS)
        b = _bias_lookup(rb, _t5_bucket_np(np.where(ok, back, 0)))
        b = b + jnp.asarray(np.where(ok, 0.0, NEG_INF).astype(np.float32))[:, :, None]
        tbls.append(jnp.transpose(b, (2, 0, 1)).reshape(nh * dec_seq, ln))
        backn = t_idx[:, None] - t_idx[None, :]
        okn = (backn >= 0) & (backn % d == 0) & (backn // d < N_KEYS)
        bn = _bias_lookup(rb, _t5_bucket_np(np.where(okn, backn, 0)))
        bn = bn + jnp.asarray(np.where(okn, 0.0, NEG_INF).astype(np.float32))[:, :, None]
        new_rows.append(jnp.transpose(bn, (2, 0, 1)).reshape(nh * dec_seq, dec_seq))
    tbl_new = jnp.stack(new_rows)
    sink_rows = jnp.broadcast_to(jnp.repeat(sink_perm, dec_seq)[:, None], (A_Q_HEADS * dec_seq, LANES))
    return tbls, tbl_new, sink_rows


def kernel(x_prompt, x_sample, cache_a_kv, cache_b1_kv, cache_b2_kv, cache_b3_kv, rel_bias, norm_gain, w_in,
           q_gain_a, k_gain_a, sinks_a, q_gain_b, k_gain_b, w_up_a, w_up_b, w_out):
    assert norm_gain.shape[0] == 1, "single layer"
    nb, seq, _ = x_prompt.shape
    n_dec, dec_seq, _ = x_sample.shape
    (w_qkv, w_gate, gq, gk, ones, wa, wb, band_bias, sink_perm, rb_cols, dils) = _prep_params(
        rel_bias, w_in[0], q_gain_a[0], k_gain_a[0], sinks_a[0], q_gain_b[0], k_gain_b[0], w_up_a[0], w_up_b[0])
    wo = w_out[0].astype(jnp.bfloat16)
    ng = norm_gain
    windows = (A_WINDOW,) + tuple(w for w, _ in B_GROUPS)
    sink2 = jnp.broadcast_to(sink_perm.reshape(N_PAIRS, 2, 1, 1),
                             (N_PAIRS, 2, N_KEYS, LANES)).reshape(N_PAIRS, 2 * N_KEYS, LANES)

    p_rows = tuple(min(w, seq) for w in windows)
    qs, ks, vs, states = _qkv_proj(x_prompt.reshape(nb * seq, D_MODEL), ng, w_qkv, gq, gk, ones,
                                   nb=nb, seq=seq, state_rows=p_rows, dils=dils, tm=512, states_t=True)
    (oa,) = _band_attention(qs[0], ks[0], vs[0], band_bias[0], sink2,
                            shared_kv=True, has_sink=True, want_lse=False)
    obs, lses = [], []
    for g in range(1, 4):
        og, lg = _band_attention(qs[g], ks[g], vs[g], band_bias[g], sink2,
                                 shared_kv=False, has_sink=False, want_lse=True)
        obs.append(og)
        lses.append(lg)
    y_prompt = _out_proj(x_prompt, ng, w_gate, oa, obs, lses, wa, wb, wo, tm=512)
    heads = (A_KV_HEADS, B_HEADS, B_HEADS, B_HEADS)
    new_prompt = tuple(jnp.transpose(states[g].reshape(1, nb, 2, heads[g], HEAD_DIM, p_rows[g]), (0, 1, 5, 2, 3, 4))
                       for g in range(4))

    t_dec = n_dec * dec_seq
    qd, _, _, news = _qkv_proj(x_sample.reshape(t_dec, D_MODEL), ng, w_qkv, gq, gk, ones,
                               nb=1, seq=t_dec, state_rows=(t_dec,) * 4, dils=(1, 1, 1, 1), tm=t_dec,
                               states_t=False)
    q_dec = jnp.concatenate([a.reshape(n_dec, dec_seq, A_WIDTH) for a in qd], axis=2)
    caches = [jnp.transpose(c[0], (0, 2, 3, 4, 1)) for c in (cache_a_kv, cache_b1_kv, cache_b2_kv, cache_b3_kv)]
    cache_lens = tuple(c.shape[-1] for c in caches)
    tbls, tbl_new, sink_rows = _decode_tables(rb_cols, sink_perm, dils, cache_lens, dec_seq)
    news3 = [a.reshape(n_dec, dec_seq, a.shape[1]) for a in news]
    oa_s, ob_s = _decode_attention(q_dec, news3, caches, tbls, tbl_new, sink_rows)
    to_slabs = lambda o: jnp.transpose(o, (1, 0, 2, 3)).reshape(1, N_PAIRS, t_dec, LANES)
    y_sample = _out_proj(x_sample.reshape(1, t_dec, D_MODEL), ng, w_gate, to_slabs(oa_s), [to_slabs(ob_s)], [],
                         wa, wb, wo, tm=t_dec)
    y_sample = y_sample.reshape(n_dec, dec_seq, D_MODEL)
    new_sample = tuple(news3[g].reshape(1, n_dec, dec_seq, 2, heads[g], HEAD_DIM) for g in range(4))
    return (y_prompt, y_sample) + new_prompt + new_sample
```

```python
import functools
import math

import numpy as np
import jax
import jax.numpy as jnp
from jax import lax
from jax.experimental import pallas as pl
from jax.experimental.pallas import tpu as pltpu

D_MODEL = 1024
HEAD_DIM = 64
A_Q_HEADS = 8
A_KV_HEADS = 2
A_WINDOW = 128
B_GROUPS = ((128, 1), (512, 4), (2048, 16))
N_B_GROUPS = 3
B_HEADS = 8
N_KEYS = 128
A_WIDTH = A_Q_HEADS * HEAD_DIM
A_KV_WIDTH = A_KV_HEADS * HEAD_DIM
B_WIDTH = B_HEADS * HEAD_DIM
N_BUCKETS = 32
MAX_DISTANCE = 2048
EPS = 1e-6
NEG_INF = -1e30
Q_SCALE = HEAD_DIM ** -0.5

LANES = 128
MXU_COLS = 256
VMEM_LIMIT = 56 * 1024 * 1024
N_PAIRS = A_WIDTH // LANES

A_HEAD_ORDER = (0, 4, 1, 5, 2, 6, 3, 7)

Q_COLS = A_WIDTH + N_B_GROUPS * B_WIDTH
KV_COLS = A_KV_WIDTH + N_B_GROUPS * B_WIDTH
QKV_COLS = Q_COLS + 2 * KV_COLS
G_COLS = A_WIDTH + B_WIDTH + 2 * D_MODEL
KV_WIDTHS = (A_KV_WIDTH, B_WIDTH, B_WIDTH, B_WIDTH)
NORM_BATCH = 4 * B_WIDTH // MXU_COLS


def _t5_bucket_np(dist):
    max_exact = N_BUCKETS // 2
    d = np.maximum(dist, 0)
    df = np.maximum(d, 1).astype(np.float32)
    large = max_exact + (np.log(df / np.float32(max_exact)) / np.float32(math.log(MAX_DISTANCE / max_exact))
                         * np.float32(N_BUCKETS - max_exact)).astype(np.int32)
    large = np.minimum(large, N_BUCKETS - 1)
    return np.where(d < max_exact, d, large)


def _rmsnorm_bf16(x, gain):
    ms = jnp.mean(x * x, axis=-1, keepdims=True)
    return (x * lax.rsqrt(ms + EPS) * gain).astype(jnp.bfloat16)


def _state_plan(n_rows, seq, tm):
    tpb = seq // tm
    r = min(n_rows, tm)
    nblk = max(n_rows // tm, 1)
    return tpb, r, nblk, tpb - nblk


def _qkv_kernel(x_ref, ng_ref, w_ref, gq_ref, gk_ref, ones_ref, *rest, tm, seq, state_rows, dils, states_t):
    q_refs = rest[0:4]
    k_refs = rest[4:8]
    v_refs = rest[8:12]
    state_refs = rest[12:16]
    tmp_ref, a_scr, sq_scr, ss_scr = rest[16:20]
    h = _rmsnorm_bf16(x_ref[...], ng_ref[...])
    ones = ones_ref[...]

    def proj(col0, w):
        return jnp.dot(h, w_ref[:, col0:col0 + w], preferred_element_type=jnp.float32)

    def emit(a, c, out_ref, d, state):
        w = a.shape[1]
        if d == 1:
            out_ref[0, 0, :, c:c + w] = a.astype(out_ref.dtype)
        else:
            for s in range(w // LANES):
                tmp_ref[c // LANES + s] = a[:, s * LANES:(s + 1) * LANES]
        if state is not None:
            sref, kv, r = state
            if states_t:
                sref[0, kv, c:c + w, :] = a[tm - r:, :].T
            else:
                wd = sref.shape[1] // 2
                sref[:, kv * wd + c:kv * wd + c + w] = a[tm - r:, :]

    def finish(out_ref, d, width):
        if d > 1:
            for cls in range(d):
                for s in range(width // LANES):
                    out_ref[0, cls, :, s * LANES:(s + 1) * LANES] = (
                        tmp_ref[s, pl.ds(cls, tm // d, stride=d), :].astype(out_ref.dtype))

    def normed_batch(arrays):
        chunks = [(ai, c) for ai in range(len(arrays)) for c in range(0, B_WIDTH, MXU_COLS)]
        for ci, (ai, c) in enumerate(chunks):
            a = proj(arrays[ai][0] + c, MXU_COLS)
            a_scr[ci] = a
            sq_scr[ci * tm:(ci + 1) * tm, :] = (a * a).astype(jnp.bfloat16)
        n = len(chunks) * tm
        ss_scr[0:n, :] = jnp.dot(sq_scr[0:n, :], ones, preferred_element_type=jnp.float32)
        for ci, (ai, c) in enumerate(chunks):
            _, gain_ref, gcol0, out_ref, d, state = arrays[ai]
            o = (a_scr[ci] * lax.rsqrt(ss_scr[ci * tm:(ci + 1) * tm, :] + EPS)
                 * gain_ref[:, gcol0 + c:gcol0 + c + MXU_COLS])
            emit(o, c, out_ref, d, state)
            if c + MXU_COLS == B_WIDTH:
                finish(out_ref, d, B_WIDTH)

    q_off = (0, A_WIDTH, A_WIDTH + B_WIDTH, A_WIDTH + 2 * B_WIDTH)
    kv_off = (0, A_KV_WIDTH, A_KV_WIDTH + B_WIDTH, A_KV_WIDTH + 2 * B_WIDTH)
    plan = [_state_plan(state_rows[g], seq, tm)[1] for g in range(4)]
    normed_batch([(q_off[g], gq_ref, q_off[g], q_refs[g], dils[g], None) for g in range(4)])
    normed_batch([(Q_COLS + kv_off[g], gk_ref, kv_off[g], k_refs[g], dils[g], (state_refs[g], 0, plan[g]))
                  for g in range(1, 4)])
    a = proj(Q_COLS, A_KV_WIDTH)
    ss = jnp.dot((a * a).astype(jnp.bfloat16), ones[:A_KV_WIDTH, :A_KV_WIDTH], preferred_element_type=jnp.float32)
    emit(a * lax.rsqrt(ss + EPS) * gk_ref[:, 0:A_KV_WIDTH], 0, k_refs[0], dils[0], (state_refs[0], 0, plan[0]))
    for g in range(4):
        wd = KV_WIDTHS[g]
        for c in range(0, wd, MXU_COLS):
            w = min(MXU_COLS, wd - c)
            emit(proj(Q_COLS + KV_COLS + kv_off[g] + c, w), c, v_refs[g], dils[g], (state_refs[g], 1, plan[g]))
        finish(v_refs[g], dils[g], wd)


def _qkv_proj(x2d, ng, w_qkv, gq, gk, ones, *, nb, seq, state_rows, dils, tm, states_t):
    t = x2d.shape[0]
    tpb = seq // tm

    def cls_shape(d, width):
        return jax.ShapeDtypeStruct((nb, d, seq // d, width), jnp.bfloat16)

    def cls_spec(d, width):
        return pl.BlockSpec((1, d, tm // d, width), lambda i: (i // tpb, 0, i % tpb, 0))

    def state_shape(g):
        if states_t:
            return jax.ShapeDtypeStruct((nb, 2, KV_WIDTHS[g], state_rows[g]), jnp.float32)
        return jax.ShapeDtypeStruct((nb * state_rows[g], 2 * KV_WIDTHS[g]), jnp.float32)

    def state_spec(g):
        _, r, nblk, j0 = _state_plan(state_rows[g], seq, tm)
        if states_t:
            return pl.BlockSpec((1, 2, KV_WIDTHS[g], r), lambda i: (i // tpb, 0, 0, jnp.maximum(i % tpb - j0, 0)))
        return pl.BlockSpec((r, 2 * KV_WIDTHS[g]),
                            lambda i: ((i // tpb) * nblk + jnp.maximum(i % tpb - j0, 0), 0))

    const = lambda i: (0, 0)
    out_shape = ([cls_shape(d, A_WIDTH) for d in dils]
                 + [cls_shape(d, w) for d, w in zip(dils, KV_WIDTHS)] * 2
                 + [state_shape(g) for g in range(4)])
    out_specs = ([cls_spec(d, A_WIDTH) for d in dils]
                 + [cls_spec(d, w) for d, w in zip(dils, KV_WIDTHS)] * 2
                 + [state_spec(g) for g in range(4)])
    res = pl.pallas_call(
        functools.partial(_qkv_kernel, tm=tm, seq=seq, state_rows=state_rows, dils=dils, states_t=states_t),
        grid=(t // tm,),
        in_specs=[
            pl.BlockSpec((tm, D_MODEL), lambda i: (i, 0)),
            pl.BlockSpec((1, D_MODEL), const),
            pl.BlockSpec((D_MODEL, QKV_COLS), const, pipeline_mode=pl.Buffered(1)),
            pl.BlockSpec((1, Q_COLS), const),
            pl.BlockSpec((1, KV_COLS), const),
            pl.BlockSpec((MXU_COLS, MXU_COLS), const),
        ],
        out_specs=out_specs,
        out_shape=out_shape,
        scratch_shapes=[pltpu.VMEM((N_PAIRS, tm, LANES), jnp.float32),
                        pltpu.VMEM((NORM_BATCH, tm, MXU_COLS), jnp.float32),
                        pltpu.VMEM((NORM_BATCH * tm, MXU_COLS), jnp.bfloat16),
                        pltpu.VMEM((NORM_BATCH * tm, MXU_COLS), jnp.float32)],
        compiler_params=pltpu.CompilerParams(
            dimension_semantics=("arbitrary",), vmem_limit_bytes=VMEM_LIMIT),
        name="qkv_proj",
    )(x2d, ng, w_qkv, gq, gk, ones)
    return res[0:4], res[4:8], res[8:12], res[12:16]


def _lookup_rows(bucket_map, top, table_ref, col_top, col_bottom, fill):
    acc = jnp.full(bucket_map.shape, fill, jnp.float32)
    for b in range(N_BUCKETS):
        val = jnp.where(top, table_ref[b, col_top], table_ref[b, col_bottom])
        acc = jnp.where(bucket_map == b, val, acc)
    return acc


def _band_attn_kernel(q_ref, k_ref, kp_ref, v_ref, vp_ref, bmap_ref, relb_ref, sinks_ref, *rest,
                      tt, dil, shared_kv, head_cols, sink_idx, want_lse):
    if want_lse:
        o_ref, lse_ref, kbuf, vbuf, o_scr, bias_ref, sink_ref = rest
    else:
        o_ref, kbuf, vbuf, o_scr, bias_ref, sink_ref = rest
        lse_ref = None
    has_sink = sink_idx is not None
    blk = N_KEYS
    i = pl.program_id(1)

    @pl.when(jnp.logical_and(pl.program_id(0) == 0, i == 0))
    def _():
        top = lax.broadcasted_iota(jnp.int32, (2 * blk, blk), 0) < blk
        for p in range(N_PAIRS):
            bias_ref[p] = _lookup_rows(bmap_ref[...], top, relb_ref, head_cols[2 * p], head_cols[2 * p + 1], 0.0)
            if has_sink:
                sink_ref[p] = jnp.where(top, sinks_ref[0, sink_idx[2 * p]], sinks_ref[0, sink_idx[2 * p + 1]])

    nblk = tt // dil // blk
    kbuf[:, 0:blk, :] = kp_ref[0]
    kbuf[:, blk:, :] = k_ref[0]
    vbuf[:, 0:blk, :] = vp_ref[0]
    vbuf[:, blk:, :] = v_ref[0]

    row = lax.broadcasted_iota(jnp.int32, (2 * blk, blk), 0)
    lane = lax.broadcasted_iota(jnp.int32, (2 * blk, blk), 1)
    ahead = lane - jnp.bitwise_and(row, blk - 1)
    tri = ahead <= 0
    cur_part = tri.astype(jnp.bfloat16)
    prev_part = jnp.logical_not(tri).astype(jnp.bfloat16)
    lane1 = lax.broadcasted_iota(jnp.int32, (blk, LANES), 1)
    left = lane1 < HEAD_DIM
    mask_l = left.astype(jnp.bfloat16)
    mask_r = jnp.logical_not(left).astype(jnp.bfloat16)
    ones_kv = jnp.ones((2 * blk, LANES), jnp.bfloat16)
    nt = (((1,), (1,)), ((), ()))

    def rows_of(cls, r0):
        start = cls + dil * r0
        return pl.ds(start, blk) if dil == 1 else pl.ds(start, blk, stride=dil)

    def block(c, carry, *, first_tile):
        cls = c // nblk
        j = c % nblk
        r0 = pl.multiple_of(j * blk, blk)
        if first_tile:
            valid = ahead <= jnp.where(j == 0, 0, blk)
        for p in range(N_PAIRS):
            kcol = 0 if shared_kv else p * LANES
            qp = q_ref[0, cls, pl.ds(r0, blk), p * LANES:(p + 1) * LANES]
            qs = jnp.concatenate([qp * mask_l, qp * mask_r], axis=0)
            kc = kbuf[cls, pl.ds(r0, 2 * blk), kcol:kcol + LANES]
            vc = vbuf[cls, pl.ds(r0, 2 * blk), kcol:kcol + LANES]
            s2 = lax.dot_general(qs, kc, nt, preferred_element_type=jnp.float32)
            s = jnp.where(tri, s2[:, blk:], s2[:, :blk]) + bias_ref[p]
            if first_tile:
                s = jnp.where(valid, s, NEG_INF)
            m = jnp.max(s, axis=-1, keepdims=True)
            if has_sink:
                sk = sink_ref[p]
                m = jnp.maximum(m, sk)
            pb = jnp.exp(s - m).astype(jnp.bfloat16)
            p2 = jnp.concatenate([pb * prev_part, pb * cur_part], axis=1)
            ov = jnp.dot(p2, jnp.concatenate([vc, ones_kv], axis=1), preferred_element_type=jnp.float32)
            l = ov[:, LANES:]
            if has_sink:
                l = l + jnp.exp(sk - m)
            num = jnp.where(left, ov[:blk, :LANES], ov[blk:, :LANES])
            den = jnp.where(left, l[:blk], l[blk:])
            o_scr[p, rows_of(cls, r0), :] = num / den
            if want_lse:
                m1 = jnp.where(left, m[:blk], m[blk:])
                lse_ref[0, p, rows_of(cls, r0), :] = m1 + jnp.log(den)
        return carry

    n_blocks = dil * nblk

    @pl.when(i == 0)
    def _():
        lax.fori_loop(0, n_blocks, functools.partial(block, first_tile=True), 0, unroll=min(8, n_blocks))

    @pl.when(i > 0)
    def _():
        lax.fori_loop(0, n_blocks, functools.partial(block, first_tile=False), 0, unroll=min(8, n_blocks))

    for p in range(N_PAIRS):
        o_ref[0, p] = o_scr[p].astype(o_ref.dtype)


def _band_attention(q, k, v, rel_bias, sinks, *, shared_kv, head_cols, sink_idx, want_lse):
    nb, dil, m, wq = q.shape
    qi = np.arange(N_KEYS)[:, None]
    ci = np.arange(N_KEYS)[None, :]
    bmap = jnp.asarray(np.tile(_t5_bucket_np(((qi - ci) % N_KEYS) * dil), (2, 1)).astype(np.int32))
    smem = functools.partial(pl.BlockSpec, memory_space=pltpu.SMEM)
    wkv = k.shape[3]
    seq = dil * m
    tt = max(min(seq, 1024), dil * N_KEYS)
    nblk_prev = tt // dil // N_KEYS
    cur = lambda b, i: (b, 0, i, 0)
    prev = lambda b, i: (b, 0, jnp.maximum(i * nblk_prev - 1, 0), 0)
    slab = pl.BlockSpec((1, N_PAIRS, tt, LANES), cur)
    out_shape = [jax.ShapeDtypeStruct((nb, N_PAIRS, seq, LANES), jnp.bfloat16)]
    out_specs = [slab]
    if want_lse:
        out_shape.append(jax.ShapeDtypeStruct((nb, N_PAIRS, seq, LANES), jnp.float32))
        out_specs.append(slab)
    return pl.pallas_call(
        functools.partial(_band_attn_kernel, tt=tt, dil=dil, shared_kv=shared_kv, head_cols=head_cols,
                          sink_idx=sink_idx, want_lse=want_lse),
        grid=(nb, seq // tt),
        in_specs=[
            pl.BlockSpec((1, dil, tt // dil, wq), cur),
            pl.BlockSpec((1, dil, tt // dil, wkv), cur),
            pl.BlockSpec((1, dil, N_KEYS, wkv), prev),
            pl.BlockSpec((1, dil, tt // dil, wkv), cur),
            pl.BlockSpec((1, dil, N_KEYS, wkv), prev),
            pl.BlockSpec(bmap.shape, lambda b, i: (0, 0)),
            smem(),
            smem(),
        ],
        out_specs=out_specs,
        out_shape=out_shape,
        scratch_shapes=[pltpu.VMEM((dil, tt // dil + N_KEYS, wkv), jnp.bfloat16),
                        pltpu.VMEM((dil, tt // dil + N_KEYS, wkv), jnp.bfloat16),
                        pltpu.VMEM((N_PAIRS, tt, LANES), jnp.float32),
                        pltpu.VMEM((N_PAIRS, 2 * N_KEYS, N_KEYS), jnp.float32),
                        pltpu.VMEM((N_PAIRS, 2 * N_KEYS, N_KEYS), jnp.float32)],
        compiler_params=pltpu.CompilerParams(
            dimension_semantics=("arbitrary", "arbitrary"), vmem_limit_bytes=VMEM_LIMIT),
        name="band_attn_d%d" % dil,
    )(q, k, k, v, v, bmap, rel_bias, sinks)


def _decode_kernel(q_ref, na_ref, nb1_ref, nb2_ref, nb3_ref, ca_ref, cb1_ref, cb2_ref, cb3_ref,
                   ma_ref, mb1_ref, mb2_ref, mb3_ref, mn_ref, relb_ref, sinks_ref, oa_ref, ob_ref,
                   ta_ref, tb1_ref, tb2_ref, tb3_ref, tn_ref, sink_ref, *, dec_seq, head_cols, sink_idx):
    t = dec_seq

    @pl.when(pl.program_id(0) == 0)
    def _():
        for g, (m_ref, t_ref) in enumerate(zip((ma_ref, mb1_ref, mb2_ref, mb3_ref),
                                               (ta_ref, tb1_ref, tb2_ref, tb3_ref))):
            top = lax.broadcasted_iota(jnp.int32, m_ref.shape, 0) < t
            top_n = lax.broadcasted_iota(jnp.int32, (2 * t, LANES), 0) < t
            for p in range(N_PAIRS):
                cols = (head_cols[g][2 * p], head_cols[g][2 * p + 1])
                rows = slice(p * 2 * t, (p + 1) * 2 * t)
                t_ref[rows, :] = _lookup_rows(m_ref[...], top, relb_ref, cols[0], cols[1], NEG_INF)
                tn_ref[g, rows, :] = _lookup_rows(mn_ref[g], top_n, relb_ref, cols[0], cols[1], NEG_INF)
                if g == 0:
                    sink_ref[rows, :] = jnp.where(top_n, sinks_ref[0, sink_idx[2 * p]],
                                                  sinks_ref[0, sink_idx[2 * p + 1]])

    left8 = lax.broadcasted_iota(jnp.int32, (2 * t, LANES), 1) < HEAD_DIM
    top8 = lax.broadcasted_iota(jnp.int32, (2 * t, LANES), 0) < t
    own = (left8 == top8).astype(jnp.float32)
    left4 = lax.broadcasted_iota(jnp.int32, (t, LANES), 1) < HEAD_DIM
    nt = (((1,), (1,)), ((), ()))

    def q_rows(c0):
        qp = q_ref[0, :, c0:c0 + LANES].astype(jnp.float32)
        return jnp.concatenate([qp, qp], axis=0) * own

    def attend(qrs, kts, vts, knews, vnews, tbl, tbl_new, sink):
        shared = len(kts) == 1
        rows8 = lambda a, u: jnp.broadcast_to(a[u:u + 1, :], (2 * t, LANES))
        qr = jnp.concatenate(qrs, axis=0)
        qb = qr.astype(jnp.bfloat16)
        if shared:
            s_c = jnp.dot(qb, kts[0], preferred_element_type=jnp.float32)
        else:
            s_c = jnp.concatenate([jnp.dot(qb[2 * t * p:2 * t * (p + 1)], kts[p], preferred_element_type=jnp.float32)
                                   for p in range(N_PAIRS)], axis=0)
        s_c = s_c + tbl
        kn = [jnp.concatenate([rows8(knews[0 if shared else p], u) for p in range(N_PAIRS)], axis=0)
              for u in range(t)]
        vn = [jnp.concatenate([rows8(vnews[0 if shared else p], u) for p in range(N_PAIRS)], axis=0)
              for u in range(t)]
        s_n = [jnp.sum(qr * kn[u], axis=-1, keepdims=True) + tbl_new[:, u:u + 1] for u in range(t)]
        m = functools.reduce(jnp.maximum, s_n, jnp.max(s_c, axis=-1, keepdims=True))
        if sink is not None:
            m = jnp.maximum(m, sink)
        pc = jnp.exp(s_c - m)
        pn = [jnp.exp(x - m) for x in s_n]
        l = functools.reduce(lambda a, b: a + b, pn, jnp.sum(pc, axis=-1, keepdims=True))
        if sink is not None:
            l = l + jnp.exp(sink - m)
        pb = pc.astype(jnp.bfloat16)
        if shared:
            o = lax.dot_general(pb, vts[0], nt, preferred_element_type=jnp.float32)
        else:
            o = jnp.concatenate([lax.dot_general(pb[2 * t * p:2 * t * (p + 1)], vts[p], nt,
                                                 preferred_element_type=jnp.float32)
                                 for p in range(N_PAIRS)], axis=0)
        for u in range(t):
            o = o + pn[u] * vn[u]
        return o / l, m + jnp.log(l)

    def fold(x, p):
        return jnp.where(left4, x[2 * t * p:2 * t * p + t], x[2 * t * p + t:2 * t * (p + 1)])

    def cache_kv(c_ref, kv, p):
        return c_ref[0, kv, 2 * p:2 * p + 2].reshape(LANES, c_ref.shape[-1]).astype(jnp.bfloat16)

    oa, _ = attend([q_rows(p * LANES) for p in range(N_PAIRS)],
                   [cache_kv(ca_ref, 0, 0)], [cache_kv(ca_ref, 1, 0)],
                   [na_ref[0, :, 0:LANES]], [na_ref[0, :, LANES:2 * LANES]],
                   ta_ref[...], tn_ref[0], sink_ref[:, 0:1])
    for p in range(N_PAIRS):
        oa_ref[0, p] = fold(oa, p).astype(oa_ref.dtype)

    caches = (cb1_ref, cb2_ref, cb3_ref)
    news = (nb1_ref, nb2_ref, nb3_ref)
    tbls = (tb1_ref, tb2_ref, tb3_ref)
    outs, lses = [], []
    for g in range(N_B_GROUPS):
        o, lse = attend([q_rows(A_WIDTH + g * B_WIDTH + p * LANES) for p in range(N_PAIRS)],
                        [cache_kv(caches[g], 0, p) for p in range(N_PAIRS)],
                        [cache_kv(caches[g], 1, p) for p in range(N_PAIRS)],
                        [news[g][0, :, p * LANES:(p + 1) * LANES] for p in range(N_PAIRS)],
                        [news[g][0, :, B_WIDTH + p * LANES:B_WIDTH + (p + 1) * LANES] for p in range(N_PAIRS)],
                        tbls[g][...], tn_ref[1 + g], None)
        outs.append(o)
        lses.append(jnp.broadcast_to(lse, o.shape))
    mx = jnp.maximum(jnp.maximum(lses[0], lses[1]), lses[2])
    es = [jnp.exp(x - mx) for x in lses]
    den = es[0] + es[1] + es[2]
    comb = (es[0] * outs[0] + es[1] * outs[1] + es[2] * outs[2]) / den
    for p in range(N_PAIRS):
        ob_ref[0, p] = fold(comb, p).astype(ob_ref.dtype)


def _decode_bucket_maps(dils, cache_lens, dec_seq):
    t_idx = np.arange(dec_seq)
    maps = []
    new = np.full((len(dils), 2 * dec_seq, LANES), -1, np.int32)
    for g, (d, ln) in enumerate(zip(dils, cache_lens)):
        back = ln + t_idx[:, None] - np.arange(ln)[None, :]
        ok = (back % d == 0) & (back // d >= 1) & (back // d < N_KEYS)
        maps.append(jnp.asarray(np.tile(np.where(ok, _t5_bucket_np(back), -1), (2, 1)).astype(np.int32)))
        backn = t_idx[:, None] - t_idx[None, :]
        okn = (backn >= 0) & (backn % d == 0) & (backn // d < N_KEYS)
        new[g, :, :dec_seq] = np.tile(np.where(okn, _t5_bucket_np(backn), -1), (2, 1))
    return maps, jnp.asarray(new)


def _decode_attention(q, news, caches, rel_bias, sinks, *, dils, head_cols, sink_idx):
    n, t, _ = q.shape
    seq3 = lambda i: (i, 0, 0)
    seq4 = lambda i: (i, 0, 0, 0)
    seq5 = lambda i: (i, 0, 0, 0, 0)
    cache_lens = tuple(c.shape[-1] for c in caches)
    maps, map_new = _decode_bucket_maps(dils, cache_lens, t)
    smem = functools.partial(pl.BlockSpec, memory_space=pltpu.SMEM)
    in_specs = [pl.BlockSpec((1, t, q.shape[2]), seq3)]
    in_specs += [pl.BlockSpec((1, t, a.shape[2]), seq3) for a in news]
    in_specs += [pl.BlockSpec((1,) + c.shape[1:], seq5) for c in caches]
    in_specs += [pl.BlockSpec(m.shape, lambda i: (0, 0)) for m in maps]
    in_specs += [pl.BlockSpec(map_new.shape, lambda i: (0, 0, 0)), smem(), smem()]
    slab = pl.BlockSpec((1, N_PAIRS, t, LANES), seq4)
    n_rows = 2 * t * N_PAIRS
    return pl.pallas_call(
        functools.partial(_decode_kernel, dec_seq=t, head_cols=head_cols, sink_idx=sink_idx),
        grid=(n,),
        in_specs=in_specs,
        out_specs=(slab, slab),
        out_shape=(jax.ShapeDtypeStruct((n, N_PAIRS, t, LANES), jnp.float32),
                   jax.ShapeDtypeStruct((n, N_PAIRS, t, LANES), jnp.float32)),
        scratch_shapes=[pltpu.VMEM((n_rows, ln), jnp.float32) for ln in cache_lens]
        + [pltpu.VMEM((len(dils), n_rows, LANES), jnp.float32), pltpu.VMEM((n_rows, LANES), jnp.float32)],
        compiler_params=pltpu.CompilerParams(
            dimension_semantics=("arbitrary",), vmem_limit_bytes=VMEM_LIMIT),
        name="decode_attn",
    )(q, *news, *caches, *maps, map_new, rel_bias, sinks)


def _out_kernel(*refs, n_groups):
    x_ref, ng_ref, wg_ref, oa_ref = refs[0:4]
    ob_refs = refs[4:4 + n_groups]
    lse_refs = refs[4 + n_groups:4 + 2 * n_groups] if n_groups > 1 else ()
    wa_ref, wb_ref, wo_ref, y_ref = refs[-4:]
    x = x_ref[0]
    h = _rmsnorm_bf16(x, ng_ref[...])

    def gate(c0, width, silu):
        a = jnp.dot(h, wg_ref[:, c0:c0 + width], preferred_element_type=jnp.float32)
        sg = 1.0 / (1.0 + jnp.exp(-a))
        return a * sg if silu else sg

    def slabs(ref):
        return jnp.concatenate([ref[0, p].astype(jnp.float32) for p in range(N_PAIRS)], axis=1)

    if n_groups > 1:
        parts = []
        for p in range(N_PAIRS):
            lses = [r[0, p] for r in lse_refs]
            mx = functools.reduce(jnp.maximum, lses)
            es = [jnp.exp(v - mx) for v in lses]
            den = functools.reduce(lambda a, b: a + b, es)
            num = functools.reduce(lambda a, b: a + b,
                                   [e * r[0, p].astype(jnp.float32) for e, r in zip(es, ob_refs)])
            parts.append(num / den)
        ob = jnp.concatenate(parts, axis=1)
    else:
        ob = slabs(ob_refs[0])
    oa = slabs(oa_ref)
    ya = jnp.dot((oa * gate(0, A_WIDTH, True)).astype(jnp.bfloat16), wa_ref[...],
                 preferred_element_type=jnp.float32)
    yb = jnp.dot((ob * gate(A_WIDTH, B_WIDTH, True)).astype(jnp.bfloat16), wb_ref[...],
                 preferred_element_type=jnp.float32)
    m0 = A_WIDTH + B_WIDTH
    merged = (gate(m0, D_MODEL, False) * ya + gate(m0 + D_MODEL, D_MODEL, False) * yb).astype(jnp.bfloat16)
    y_ref[0] = x + jnp.dot(merged, wo_ref[...], preferred_element_type=jnp.float32)


def _out_proj(x3d, ng, wg, oa, obs, lses, wa, wb, wo, *, tm):
    nb, seq, _ = x3d.shape
    row = lambda b, i: (b, i, 0)
    slab = pl.BlockSpec((1, N_PAIRS, tm, LANES), lambda b, i: (b, 0, i, 0))
    const = lambda b, i: (0, 0)
    once = dict(pipeline_mode=pl.Buffered(1))
    in_specs = [pl.BlockSpec((1, tm, D_MODEL), row), pl.BlockSpec((1, D_MODEL), const),
                pl.BlockSpec(wg.shape, const, **once), slab]
    in_specs += [slab for _ in obs] + [slab for _ in lses]
    in_specs += [pl.BlockSpec(wa.shape, const, **once), pl.BlockSpec(wb.shape, const, **once),
                 pl.BlockSpec(wo.shape, const, **once)]
    return pl.pallas_call(
        functools.partial(_out_kernel, n_groups=len(obs)),
        grid=(nb, seq // tm),
        in_specs=in_specs,
        out_specs=pl.BlockSpec((1, tm, D_MODEL), row),
        out_shape=jax.ShapeDtypeStruct((nb, seq, D_MODEL), jnp.float32),
        compiler_params=pltpu.CompilerParams(
            dimension_semantics=("arbitrary", "arbitrary"), vmem_limit_bytes=VMEM_LIMIT),
        name="out_proj",
    )(x3d, ng, wg, oa, *obs, *lses, wa, wb, wo)


def _prep_params(w_in, q_gain_a, k_gain_a, q_gain_b, k_gain_b, w_up_a, w_up_b):
    offs = np.cumsum((0, A_WIDTH, A_KV_WIDTH, A_KV_WIDTH, A_WIDTH, 3 * B_WIDTH, 3 * B_WIDTH, 3 * B_WIDTH,
                      B_WIDTH, D_MODEL, D_MODEL))
    seg = lambda s: w_in[:, int(offs[s]):int(offs[s + 1])]

    def perm_heads(w2d):
        return jnp.concatenate([w2d[:, h * HEAD_DIM:(h + 1) * HEAD_DIM] for h in A_HEAD_ORDER], axis=1)

    w_qkv = jnp.concatenate([perm_heads(seg(0)), seg(4), seg(1), seg(5), seg(2), seg(6)],
                            axis=1).astype(jnp.bfloat16)
    w_gate = jnp.concatenate([perm_heads(seg(3)), seg(7), seg(8), seg(9)], axis=1).astype(jnp.bfloat16)
    gq = jnp.broadcast_to(jnp.concatenate([q_gain_a[None], q_gain_b], axis=0)[:, None, :] * Q_SCALE,
                          (1 + N_B_GROUPS, A_Q_HEADS, HEAD_DIM)).reshape(1, Q_COLS)
    gk = jnp.concatenate([jnp.broadcast_to(k_gain_a, (A_KV_HEADS, HEAD_DIM)).reshape(A_KV_WIDTH),
                          jnp.broadcast_to(k_gain_b[:, None, :], (N_B_GROUPS, B_HEADS, HEAD_DIM)).reshape(-1)])[None]
    hd = np.arange(MXU_COLS) // HEAD_DIM
    ones = jnp.asarray((hd[:, None] == hd[None, :]).astype(np.float32) / HEAD_DIM, jnp.bfloat16)
    wa = jnp.concatenate([w_up_a[h * HEAD_DIM:(h + 1) * HEAD_DIM] for h in A_HEAD_ORDER], axis=0)
    return w_qkv, w_gate, gq, gk, ones, wa.astype(jnp.bfloat16), w_up_b.astype(jnp.bfloat16)


def kernel(x_prompt, x_sample, cache_a_kv, cache_b1_kv, cache_b2_kv, cache_b3_kv, rel_bias, norm_gain, w_in,
           q_gain_a, k_gain_a, sinks_a, q_gain_b, k_gain_b, w_up_a, w_up_b, w_out):
    assert norm_gain.shape[0] == 1, "single layer"
    nb, seq, _ = x_prompt.shape
    n_dec, dec_seq, _ = x_sample.shape
    w_qkv, w_gate, gq, gk, ones, wa, wb = _prep_params(
        w_in[0], q_gain_a[0], k_gain_a[0], q_gain_b[0], k_gain_b[0], w_up_a[0], w_up_b[0])
    wo = w_out[0].astype(jnp.bfloat16)
    ng = norm_gain
    windows = (A_WINDOW,) + tuple(w for w, _ in B_GROUPS)
    dils = (1,) + tuple(d for _, d in B_GROUPS)
    head_cols = (A_HEAD_ORDER,) + tuple(tuple(range(A_Q_HEADS + g * B_HEADS, A_Q_HEADS + (g + 1) * B_HEADS))
                                        for g in range(N_B_GROUPS))

    p_rows = tuple(min(w, seq) for w in windows)
    qs, ks, vs, states = _qkv_proj(x_prompt.reshape(nb * seq, D_MODEL), ng, w_qkv, gq, gk, ones,
                                   nb=nb, seq=seq, state_rows=p_rows, dils=dils, tm=512, states_t=True)
    (oa,) = _band_attention(qs[0], ks[0], vs[0], rel_bias, sinks_a, shared_kv=True, head_cols=head_cols[0],
                            sink_idx=A_HEAD_ORDER, want_lse=False)
    obs, lses = [], []
    for g in range(1, 4):
        og, lg = _band_attention(qs[g], ks[g], vs[g], rel_bias, sinks_a, shared_kv=False, head_cols=head_cols[g],
                                 sink_idx=None, want_lse=True)
        obs.append(og)
        lses.append(lg)
    y_prompt = _out_proj(x_prompt, ng, w_gate, oa, obs, lses, wa, wb, wo, tm=512)
    heads = (A_KV_HEADS, B_HEADS, B_HEADS, B_HEADS)
    new_prompt = tuple(jnp.transpose(states[g].reshape(1, nb, 2, heads[g], HEAD_DIM, p_rows[g]), (0, 1, 5, 2, 3, 4))
                       for g in range(4))

    t_dec = n_dec * dec_seq
    qd, _, _, news = _qkv_proj(x_sample.reshape(t_dec, D_MODEL), ng, w_qkv, gq, gk, ones,
                               nb=1, seq=t_dec, state_rows=(t_dec,) * 4, dils=(1, 1, 1, 1), tm=t_dec,
                               states_t=False)
    q_dec = jnp.concatenate([a.reshape(n_dec, dec_seq, A_WIDTH) for a in qd], axis=2)
    caches = [jnp.transpose(c[0], (0, 2, 3, 4, 1)) for c in (cache_a_kv, cache_b1_kv, cache_b2_kv, cache_b3_kv)]
    news3 = [a.reshape(n_dec, dec_seq, a.shape[1]) for a in news]
    oa_s, ob_s = _decode_attention(q_dec, news3, caches, rel_bias, sinks_a, dils=dils, head_cols=head_cols,
                                   sink_idx=A_HEAD_ORDER)
    to_slabs = lambda o: jnp.transpose(o, (1, 0, 2, 3)).reshape(1, N_PAIRS, t_dec, LANES)
    y_sample = _out_proj(x_sample.reshape(1, t_dec, D_MODEL), ng, w_gate, to_slabs(oa_s), [to_slabs(ob_s)], [],
                         wa, wb, wo, tm=t_dec)
    y_sample = y_sample.reshape(n_dec, dec_seq, D_MODEL)
    new_sample = tuple(news3[g].reshape(1, n_dec, dec_seq, 2, heads[g], HEAD_DIM) for g in range(4))
    return (y_prompt, y_sample) + new_prompt + new_sample
```

```python
import functools
import math

import numpy as np
import jax
import jax.numpy as jnp
from jax import lax
from jax.experimental import pallas as pl
from jax.experimental.pallas import tpu as pltpu

D_MODEL = 1024
HEAD_DIM = 64
A_Q_HEADS = 8
A_KV_HEADS = 2
A_WINDOW = 128
B_GROUPS = ((128, 1), (512, 4), (2048, 16))
N_B_GROUPS = 3
B_HEADS = 8
N_KEYS = 128
A_WIDTH = A_Q_HEADS * HEAD_DIM
A_KV_WIDTH = A_KV_HEADS * HEAD_DIM
B_WIDTH = B_HEADS * HEAD_DIM
N_BUCKETS = 32
MAX_DISTANCE = 2048
EPS = 1e-6
NEG_INF = -1e30
Q_SCALE = HEAD_DIM ** -0.5

LANES = 128
MXU_COLS = 256
VMEM_LIMIT = 56 * 1024 * 1024
N_PAIRS = A_WIDTH // LANES

A_HEAD_ORDER = (0, 4, 1, 5, 2, 6, 3, 7)

Q_COLS = A_WIDTH + N_B_GROUPS * B_WIDTH
KV_COLS = A_KV_WIDTH + N_B_GROUPS * B_WIDTH
QKV_COLS = Q_COLS + 2 * KV_COLS
G_COLS = A_WIDTH + B_WIDTH + 2 * D_MODEL
KV_WIDTHS = (A_KV_WIDTH, B_WIDTH, B_WIDTH, B_WIDTH)
NORM_BATCH = 4 * B_WIDTH // MXU_COLS


def _t5_bucket_np(dist):
    max_exact = N_BUCKETS // 2
    d = np.maximum(dist, 0)
    df = np.maximum(d, 1).astype(np.float32)
    large = max_exact + (np.log(df / np.float32(max_exact)) / np.float32(math.log(MAX_DISTANCE / max_exact))
                         * np.float32(N_BUCKETS - max_exact)).astype(np.int32)
    large = np.minimum(large, N_BUCKETS - 1)
    return np.where(d < max_exact, d, large)


def _rmsnorm_bf16(x, gain):
    ms = jnp.mean(x * x, axis=-1, keepdims=True)
    return (x * lax.rsqrt(ms + EPS) * gain).astype(jnp.bfloat16)


def _state_plan(n_rows, seq, tm):
    tpb = seq // tm
    r = min(n_rows, tm)
    nblk = max(n_rows // tm, 1)
    return tpb, r, nblk, tpb - nblk


def _qkv_kernel(x_ref, ng_ref, w_ref, gq_ref, gk_ref, ones_ref, *rest, tm, seq, state_rows, dils, states_t):
    q_refs = rest[0:4]
    k_refs = rest[4:8]
    v_refs = rest[8:12]
    state_refs = rest[12:16]
    tmp_ref, a_scr, sq_scr, ss_scr = rest[16:20]
    h = _rmsnorm_bf16(x_ref[...], ng_ref[...])
    ones = ones_ref[...]

    def proj(col0, w):
        return jnp.dot(h, w_ref[:, col0:col0 + w], preferred_element_type=jnp.float32)

    def emit(a, c, out_ref, d, state):
        w = a.shape[1]
        if d == 1:
            out_ref[0, 0, :, c:c + w] = a.astype(out_ref.dtype)
        else:
            for s in range(w // LANES):
                tmp_ref[c // LANES + s] = a[:, s * LANES:(s + 1) * LANES]
        if state is not None:
            sref, kv, r = state
            if states_t:
                sref[0, kv, c:c + w, :] = a[tm - r:, :].T
            else:
                wd = sref.shape[1] // 2
                sref[:, kv * wd + c:kv * wd + c + w] = a[tm - r:, :]

    def finish(out_ref, d, width):
        if d > 1:
            for cls in range(d):
                for s in range(width // LANES):
                    out_ref[0, cls, :, s * LANES:(s + 1) * LANES] = (
                        tmp_ref[s, pl.ds(cls, tm // d, stride=d), :].astype(out_ref.dtype))

    def normed_batch(arrays):
        chunks = [(ai, c) for ai in range(len(arrays)) for c in range(0, B_WIDTH, MXU_COLS)]
        for ci, (ai, c) in enumerate(chunks):
            a = proj(arrays[ai][0] + c, MXU_COLS)
            a_scr[ci] = a
            sq_scr[ci * tm:(ci + 1) * tm, :] = (a * a).astype(jnp.bfloat16)
        n = len(chunks) * tm
        ss_scr[0:n, :] = jnp.dot(sq_scr[0:n, :], ones, preferred_element_type=jnp.float32)
        for ci, (ai, c) in enumerate(chunks):
            _, gain_ref, gcol0, out_ref, d, state = arrays[ai]
            o = (a_scr[ci] * lax.rsqrt(ss_scr[ci * tm:(ci + 1) * tm, :] + EPS)
                 * gain_ref[:, gcol0 + c:gcol0 + c + MXU_COLS])
            emit(o, c, out_ref, d, state)
            if c + MXU_COLS == B_WIDTH:
                finish(out_ref, d, B_WIDTH)

    q_off = (0, A_WIDTH, A_WIDTH + B_WIDTH, A_WIDTH + 2 * B_WIDTH)
    kv_off = (0, A_KV_WIDTH, A_KV_WIDTH + B_WIDTH, A_KV_WIDTH + 2 * B_WIDTH)
    plan = [_state_plan(state_rows[g], seq, tm)[1] for g in range(4)]
    normed_batch([(q_off[g], gq_ref, q_off[g], q_refs[g], dils[g], None) for g in range(4)])
    normed_batch([(Q_COLS + kv_off[g], gk_ref, kv_off[g], k_refs[g], dils[g], (state_refs[g], 0, plan[g]))
                  for g in range(1, 4)])
    a = proj(Q_COLS, A_KV_WIDTH)
    ss = jnp.dot((a * a).astype(jnp.bfloat16), ones[:A_KV_WIDTH, :A_KV_WIDTH], preferred_element_type=jnp.float32)
    emit(a * lax.rsqrt(ss + EPS) * gk_ref[:, 0:A_KV_WIDTH], 0, k_refs[0], dils[0], (state_refs[0], 0, plan[0]))
    for g in range(4):
        wd = KV_WIDTHS[g]
        for c in range(0, wd, MXU_COLS):
            w = min(MXU_COLS, wd - c)
            emit(proj(Q_COLS + KV_COLS + kv_off[g] + c, w), c, v_refs[g], dils[g], (state_refs[g], 1, plan[g]))
        finish(v_refs[g], dils[g], wd)


def _qkv_proj(x2d, ng, w_qkv, gq, gk, ones, *, nb, seq, state_rows, dils, tm, states_t):
    t = x2d.shape[0]
    tpb = seq // tm

    def cls_shape(d, width):
        return jax.ShapeDtypeStruct((nb, d, seq // d, width), jnp.bfloat16)

    def cls_spec(d, width):
        return pl.BlockSpec((1, d, tm // d, width), lambda i: (i // tpb, 0, i % tpb, 0))

    def state_shape(g):
        if states_t:
            return jax.ShapeDtypeStruct((nb, 2, KV_WIDTHS[g], state_rows[g]), jnp.float32)
        return jax.ShapeDtypeStruct((nb * state_rows[g], 2 * KV_WIDTHS[g]), jnp.float32)

    def state_spec(g):
        _, r, nblk, j0 = _state_plan(state_rows[g], seq, tm)
        if states_t:
            return pl.BlockSpec((1, 2, KV_WIDTHS[g], r), lambda i: (i // tpb, 0, 0, jnp.maximum(i % tpb - j0, 0)))
        return pl.BlockSpec((r, 2 * KV_WIDTHS[g]),
                            lambda i: ((i // tpb) * nblk + jnp.maximum(i % tpb - j0, 0), 0))

    const = lambda i: (0, 0)
    out_shape = ([cls_shape(d, A_WIDTH) for d in dils]
                 + [cls_shape(d, w) for d, w in zip(dils, KV_WIDTHS)] * 2
                 + [state_shape(g) for g in range(4)])
    out_specs = ([cls_spec(d, A_WIDTH) for d in dils]
                 + [cls_spec(d, w) for d, w in zip(dils, KV_WIDTHS)] * 2
                 + [state_spec(g) for g in range(4)])
    res = pl.pallas_call(
        functools.partial(_qkv_kernel, tm=tm, seq=seq, state_rows=state_rows, dils=dils, states_t=states_t),
        grid=(t // tm,),
        in_specs=[
            pl.BlockSpec((tm, D_MODEL), lambda i: (i, 0)),
            pl.BlockSpec((1, D_MODEL), const),
            pl.BlockSpec((D_MODEL, QKV_COLS), const, pipeline_mode=pl.Buffered(1)),
            pl.BlockSpec((1, Q_COLS), const),
            pl.BlockSpec((1, KV_COLS), const),
            pl.BlockSpec((MXU_COLS, MXU_COLS), const),
        ],
        out_specs=out_specs,
        out_shape=out_shape,
        scratch_shapes=[pltpu.VMEM((N_PAIRS, tm, LANES), jnp.float32),
                        pltpu.VMEM((NORM_BATCH, tm, MXU_COLS), jnp.float32),
                        pltpu.VMEM((NORM_BATCH * tm, MXU_COLS), jnp.bfloat16),
                        pltpu.VMEM((NORM_BATCH * tm, MXU_COLS), jnp.float32)],
        compiler_params=pltpu.CompilerParams(
            dimension_semantics=("arbitrary",), vmem_limit_bytes=VMEM_LIMIT),
        name="qkv_proj",
    )(x2d, ng, w_qkv, gq, gk, ones)
    return res[0:4], res[4:8], res[8:12], res[12:16]


def _lookup_rows(bucket_map, top, table_ref, col_top, col_bottom, fill):
    acc = jnp.full(bucket_map.shape, fill, jnp.float32)
    for b in range(N_BUCKETS):
        val = jnp.where(top, table_ref[b, col_top], table_ref[b, col_bottom])
        acc = jnp.where(bucket_map == b, val, acc)
    return acc


def _band_block(q_ref, k_ref, kp_ref, v_ref, vp_ref, o_ref, lse_ref, bias_ref, sink_ref, *,
                cls, first, dil, shared_kv):
    blk = N_KEYS
    row = lax.broadcasted_iota(jnp.int32, (2 * blk, blk), 0)
    lane = lax.broadcasted_iota(jnp.int32, (2 * blk, blk), 1)
    ahead = lane - jnp.bitwise_and(row, blk - 1)
    tri = ahead <= 0
    valid = ahead <= jnp.where(first, 0, blk)
    cur_part = tri.astype(jnp.bfloat16)
    prev_part = jnp.logical_not(tri).astype(jnp.bfloat16)
    left = lax.broadcasted_iota(jnp.int32, (blk, LANES), 1) < HEAD_DIM
    mask_l = left.astype(jnp.bfloat16)
    mask_r = jnp.logical_not(left).astype(jnp.bfloat16)
    ones_kv = jnp.ones((2 * blk, LANES), jnp.bfloat16)
    nt = (((1,), (1,)), ((), ()))
    rows = pl.ds(0, blk) if dil == 1 else pl.ds(cls, blk, stride=dil)
    for p in range(N_PAIRS):
        kcol = 0 if shared_kv else p * LANES
        qp = q_ref[0, 0, :, p * LANES:(p + 1) * LANES]
        qs = jnp.concatenate([qp * mask_l, qp * mask_r], axis=0)
        kc = jnp.concatenate([kp_ref[0, 0, :, kcol:kcol + LANES], k_ref[0, 0, :, kcol:kcol + LANES]], axis=0)
        vc = jnp.concatenate([vp_ref[0, 0, :, kcol:kcol + LANES], v_ref[0, 0, :, kcol:kcol + LANES]], axis=0)
        s2 = lax.dot_general(qs, kc, nt, preferred_element_type=jnp.float32)
        s = jnp.where(tri, s2[:, blk:], s2[:, :blk]) + bias_ref[p]
        s = jnp.where(valid, s, NEG_INF)
        m = jnp.max(s, axis=-1, keepdims=True)
        if sink_ref is not None:
            sk = sink_ref[p]
            m = jnp.maximum(m, sk)
        pb = jnp.exp(s - m).astype(jnp.bfloat16)
        p2 = jnp.concatenate([pb * prev_part, pb * cur_part], axis=1)
        ov = jnp.dot(p2, jnp.concatenate([vc, ones_kv], axis=1), preferred_element_type=jnp.float32)
        l = ov[:, LANES:]
        if sink_ref is not None:
            l = l + jnp.exp(sk - m)
        num = jnp.where(left, ov[:blk, :LANES], ov[blk:, :LANES])
        den = jnp.where(left, l[:blk], l[blk:])
        o_ref[0, p, rows, :] = num / den
        if lse_ref is not None:
            m1 = jnp.where(left, m[:blk], m[blk:])
            lse_ref[0, p, rows, :] = m1 + jnp.log(den)


def _decode_kernel(q_ref, na_ref, nb1_ref, nb2_ref, nb3_ref, ca_ref, cb1_ref, cb2_ref, cb3_ref,
                   ma_ref, mb1_ref, mb2_ref, mb3_ref, mn_ref, relb_ref, sinks_ref, oa_ref, ob_ref,
                   ta_ref, tb1_ref, tb2_ref, tb3_ref, tn_ref, sink_ref, *, dec_seq, head_cols, sink_idx, init):
    t = dec_seq
    if init:
        for g, (m_ref, t_ref) in enumerate(zip((ma_ref, mb1_ref, mb2_ref, mb3_ref),
                                               (ta_ref, tb1_ref, tb2_ref, tb3_ref))):
            top = lax.broadcasted_iota(jnp.int32, m_ref.shape, 0) < t
            top_n = lax.broadcasted_iota(jnp.int32, (2 * t, LANES), 0) < t
            for p in range(N_PAIRS):
                cols = (head_cols[g][2 * p], head_cols[g][2 * p + 1])
                rows = slice(p * 2 * t, (p + 1) * 2 * t)
                t_ref[rows, :] = _lookup_rows(m_ref[...], top, relb_ref, cols[0], cols[1], NEG_INF)
                tn_ref[g, rows, :] = _lookup_rows(mn_ref[g], top_n, relb_ref, cols[0], cols[1], NEG_INF)
                if g == 0:
                    sink_ref[rows, :] = jnp.where(top_n, sinks_ref[0, sink_idx[2 * p]],
                                                  sinks_ref[0, sink_idx[2 * p + 1]])
        return

    left8 = lax.broadcasted_iota(jnp.int32, (2 * t, LANES), 1) < HEAD_DIM
    top8 = lax.broadcasted_iota(jnp.int32, (2 * t, LANES), 0) < t
    own = (left8 == top8).astype(jnp.float32)
    left4 = lax.broadcasted_iota(jnp.int32, (t, LANES), 1) < HEAD_DIM
    nt = (((1,), (1,)), ((), ()))

    def q_rows(c0):
        qp = q_ref[0, :, c0:c0 + LANES].astype(jnp.float32)
        return jnp.concatenate([qp, qp], axis=0) * own

    def attend(qrs, kts, vts, knews, vnews, tbl, tbl_new, sink):
        shared = len(kts) == 1
        rows8 = lambda a, u: jnp.broadcast_to(a[u:u + 1, :], (2 * t, LANES))
        qr = jnp.concatenate(qrs, axis=0)
        qb = qr.astype(jnp.bfloat16)
        if shared:
            s_c = jnp.dot(qb, kts[0], preferred_element_type=jnp.float32)
        else:
            s_c = jnp.concatenate([jnp.dot(qb[2 * t * p:2 * t * (p + 1)], kts[p], preferred_element_type=jnp.float32)
                                   for p in range(N_PAIRS)], axis=0)
        s_c = s_c + tbl
        kn = [jnp.concatenate([rows8(knews[0 if shared else p], u) for p in range(N_PAIRS)], axis=0)
              for u in range(t)]
        vn = [jnp.concatenate([rows8(vnews[0 if shared else p], u) for p in range(N_PAIRS)], axis=0)
              for u in range(t)]
        s_n = [jnp.sum(qr * kn[u], axis=-1, keepdims=True) + tbl_new[:, u:u + 1] for u in range(t)]
        m = functools.reduce(jnp.maximum, s_n, jnp.max(s_c, axis=-1, keepdims=True))
        if sink is not None:
            m = jnp.maximum(m, sink)
        pc = jnp.exp(s_c - m)
        pn = [jnp.exp(x - m) for x in s_n]
        l = functools.reduce(lambda a, b: a + b, pn, jnp.sum(pc, axis=-1, keepdims=True))
        if sink is not None:
            l = l + jnp.exp(sink - m)
        pb = pc.astype(jnp.bfloat16)
        if shared:
            o = lax.dot_general(pb, vts[0], nt, preferred_element_type=jnp.float32)
        else:
            o = jnp.concatenate([lax.dot_general(pb[2 * t * p:2 * t * (p + 1)], vts[p], nt,
                                                 preferred_element_type=jnp.float32)
                                 for p in range(N_PAIRS)], axis=0)
        for u in range(t):
            o = o + pn[u] * vn[u]
        return o / l, m + jnp.log(l)

    def fold(x, p):
        return jnp.where(left4, x[2 * t * p:2 * t * p + t], x[2 * t * p + t:2 * t * (p + 1)])

    def cache_kv(c_ref, kv, p):
        return c_ref[0, kv, 2 * p:2 * p + 2].reshape(LANES, c_ref.shape[-1]).astype(jnp.bfloat16)

    oa, _ = attend([q_rows(p * LANES) for p in range(N_PAIRS)],
                   [cache_kv(ca_ref, 0, 0)], [cache_kv(ca_ref, 1, 0)],
                   [na_ref[0, :, 0:LANES]], [na_ref[0, :, LANES:2 * LANES]],
                   ta_ref[...], tn_ref[0], sink_ref[:, 0:1])
    for p in range(N_PAIRS):
        oa_ref[0, p] = fold(oa, p).astype(oa_ref.dtype)

    caches = (cb1_ref, cb2_ref, cb3_ref)
    news = (nb1_ref, nb2_ref, nb3_ref)
    tbls = (tb1_ref, tb2_ref, tb3_ref)
    outs, lses = [], []
    for g in range(N_B_GROUPS):
        o, lse = attend([q_rows(A_WIDTH + g * B_WIDTH + p * LANES) for p in range(N_PAIRS)],
                        [cache_kv(caches[g], 0, p) for p in range(N_PAIRS)],
                        [cache_kv(caches[g], 1, p) for p in range(N_PAIRS)],
                        [news[g][0, :, p * LANES:(p + 1) * LANES] for p in range(N_PAIRS)],
                        [news[g][0, :, B_WIDTH + p * LANES:B_WIDTH + (p + 1) * LANES] for p in range(N_PAIRS)],
                        tbls[g][...], tn_ref[1 + g], None)
        outs.append(o)
        lses.append(jnp.broadcast_to(lse, o.shape))
    mx = jnp.maximum(jnp.maximum(lses[0], lses[1]), lses[2])
    es = [jnp.exp(x - mx) for x in lses]
    den = es[0] + es[1] + es[2]
    comb = (es[0] * outs[0] + es[1] * outs[1] + es[2] * outs[2]) / den
    for p in range(N_PAIRS):
        ob_ref[0, p] = fold(comb, p).astype(ob_ref.dtype)


def _decode_bucket_maps(dils, cache_lens, dec_seq):
    t_idx = np.arange(dec_seq)
    maps = []
    new = np.full((len(dils), 2 * dec_seq, LANES), -1, np.int32)
    for g, (d, ln) in enumerate(zip(dils, cache_lens)):
        back = ln + t_idx[:, None] - np.arange(ln)[None, :]
        ok = (back % d == 0) & (back // d >= 1) & (back // d < N_KEYS)
        maps.append(jnp.asarray(np.tile(np.where(ok, _t5_bucket_np(back), -1), (2, 1)).astype(np.int32)))
        backn = t_idx[:, None] - t_idx[None, :]
        okn = (backn >= 0) & (backn % d == 0) & (backn // d < N_KEYS)
        new[g, :, :dec_seq] = np.tile(np.where(okn, _t5_bucket_np(backn), -1), (2, 1))
    return maps, jnp.asarray(new)


def _decode_band_kernel(*refs, dec_seq, head_cols, sink_idx, dils, blocks_per_batch):
    n_g = len(dils)
    dec_in = refs[0:16]
    band_in = [refs[16 + 6 * g:16 + 6 * (g + 1)] for g in range(n_g)]
    outs = refs[16 + 6 * n_g:16 + 6 * n_g + 2 + 2 * n_g - 1]
    oa_ref, ob_ref = outs[0:2]
    o_refs = outs[2:2 + n_g]
    lse_refs = (None,) + tuple(outs[2 + n_g:])
    scratch = refs[16 + 6 * n_g + 2 + 2 * n_g - 1:]
    dec_scratch = scratch[0:6]
    bias_refs = scratch[6:6 + n_g]
    sink_tile = scratch[6 + n_g]
    blk = N_KEYS
    s = pl.program_id(0)
    decode = functools.partial(_decode_kernel, *dec_in, oa_ref, ob_ref, *dec_scratch, dec_seq=dec_seq,
                               head_cols=head_cols, sink_idx=sink_idx)

    @pl.when(s == 0)
    def _():
        top = lax.broadcasted_iota(jnp.int32, (2 * blk, blk), 0) < blk
        for g in range(n_g):
            bmap = band_in[g][5][...]
            for p in range(N_PAIRS):
                bias_refs[g][p] = _lookup_rows(bmap, top, dec_in[14], head_cols[g][2 * p], head_cols[g][2 * p + 1],
                                               0.0)
        for p in range(N_PAIRS):
            sink_tile[p] = jnp.where(top, dec_in[15][0, sink_idx[2 * p]], dec_in[15][0, sink_idx[2 * p + 1]])
        decode(init=True)

    r = s % blocks_per_batch

    for g in range(n_g):
        q_ref, k_ref, kp_ref, v_ref, vp_ref, _ = band_in[g]
        _band_block(q_ref, k_ref, kp_ref, v_ref, vp_ref, o_refs[g], lse_refs[g], bias_refs[g],
                    sink_tile if g == 0 else None, cls=r % dils[g], first=(r // dils[g]) == 0, dil=dils[g],
                    shared_kv=(g == 0))
    decode(init=False)


def _decode_band_attention(q_dec, news, caches, qs, ks, vs, rel_bias, sinks, *, dils, head_cols, sink_idx):
    n, t, _ = q_dec.shape
    nb = qs[0].shape[0]
    seq = qs[0].shape[1] * qs[0].shape[2]
    ppb = seq // N_KEYS
    assert n == nb * ppb, "one prompt block of every group per sample sequence"
    seq3 = lambda i: (i, 0, 0)
    seq4 = lambda i: (i, 0, 0, 0)
    seq5 = lambda i: (i, 0, 0, 0, 0)
    cache_lens = tuple(c.shape[-1] for c in caches)
    maps, map_new = _decode_bucket_maps(dils, cache_lens, t)
    smem = functools.partial(pl.BlockSpec, memory_space=pltpu.SMEM)
    in_specs = [pl.BlockSpec((1, t, q_dec.shape[2]), seq3)]
    in_specs += [pl.BlockSpec((1, t, a.shape[2]), seq3) for a in news]
    in_specs += [pl.BlockSpec((1,) + c.shape[1:], seq5) for c in caches]
    in_specs += [pl.BlockSpec(m.shape, lambda i: (0, 0)) for m in maps]
    in_specs += [pl.BlockSpec(map_new.shape, lambda i: (0, 0, 0)), smem(), smem()]
    band_args = []
    out_specs = [pl.BlockSpec((1, N_PAIRS, t, LANES), seq4)] * 2
    out_shape = [jax.ShapeDtypeStruct((n, N_PAIRS, t, LANES), jnp.float32)] * 2
    qi = np.arange(N_KEYS)[:, None]
    ci = np.arange(N_KEYS)[None, :]
    for g, d in enumerate(dils):
        cur = lambda s, d=d: (s // ppb, (s % ppb) % d, (s % ppb) // d, 0)
        prev = lambda s, d=d: (s // ppb, (s % ppb) % d, jnp.maximum((s % ppb) // d - 1, 0), 0)
        wq, wkv = qs[g].shape[3], ks[g].shape[3]
        bmap = jnp.asarray(np.tile(_t5_bucket_np(((qi - ci) % N_KEYS) * d), (2, 1)).astype(np.int32))
        in_specs += [pl.BlockSpec((1, 1, N_KEYS, wq), cur), pl.BlockSpec((1, 1, N_KEYS, wkv), cur),
                     pl.BlockSpec((1, 1, N_KEYS, wkv), prev), pl.BlockSpec((1, 1, N_KEYS, wkv), cur),
                     pl.BlockSpec((1, 1, N_KEYS, wkv), prev), pl.BlockSpec(bmap.shape, lambda s: (0, 0))]
        band_args += [qs[g], ks[g], ks[g], vs[g], vs[g], bmap]
    tile = lambda d: pl.BlockSpec((1, N_PAIRS, N_KEYS * d, LANES), lambda s, d=d: (s // ppb, 0, (s % ppb) // d, 0))
    slab_shape = jax.ShapeDtypeStruct((nb, N_PAIRS, seq, LANES), jnp.float32)
    out_specs += [tile(d) for d in dils] + [tile(d) for d in dils[1:]]
    out_shape += [slab_shape] * (2 * len(dils) - 1)
    n_rows = 2 * t * N_PAIRS
    tile_scr = pltpu.VMEM((N_PAIRS, 2 * N_KEYS, N_KEYS), jnp.float32)
    res = pl.pallas_call(
        functools.partial(_decode_band_kernel, dec_seq=t, head_cols=head_cols, sink_idx=sink_idx, dils=dils,
                          blocks_per_batch=ppb),
        grid=(n,),
        in_specs=in_specs,
        out_specs=out_specs,
        out_shape=out_shape,
        scratch_shapes=[pltpu.VMEM((n_rows, ln), jnp.float32) for ln in cache_lens]
        + [pltpu.VMEM((len(dils), n_rows, LANES), jnp.float32), pltpu.VMEM((n_rows, LANES), jnp.float32)]
        + [tile_scr] * (len(dils) + 1),
        compiler_params=pltpu.CompilerParams(
            dimension_semantics=("arbitrary",), vmem_limit_bytes=VMEM_LIMIT),
        name="decode_band_attn",
    )(q_dec, *news, *caches, *maps, map_new, rel_bias, sinks, *band_args)
    ng = len(dils)
    return res[0], res[1], res[2:2 + ng], res[2 + ng:]


def _out_kernel(*refs, n_groups):
    x_ref, ng_ref, wg_ref, oa_ref = refs[0:4]
    ob_refs = refs[4:4 + n_groups]
    lse_refs = refs[4 + n_groups:4 + 2 * n_groups] if n_groups > 1 else ()
    wa_ref, wb_ref, wo_ref, y_ref = refs[-4:]
    x = x_ref[0]
    h = _rmsnorm_bf16(x, ng_ref[...])

    def gate(c0, width, silu):
        a = jnp.dot(h, wg_ref[:, c0:c0 + width], preferred_element_type=jnp.float32)
        sg = 1.0 / (1.0 + jnp.exp(-a))
        return a * sg if silu else sg

    def slabs(ref):
        return jnp.concatenate([ref[0, p].astype(jnp.float32) for p in range(N_PAIRS)], axis=1)

    if n_groups > 1:
        parts = []
        for p in range(N_PAIRS):
            lses = [r[0, p] for r in lse_refs]
            mx = functools.reduce(jnp.maximum, lses)
            es = [jnp.exp(v - mx) for v in lses]
            den = functools.reduce(lambda a, b: a + b, es)
            num = functools.reduce(lambda a, b: a + b,
                                   [e * r[0, p].astype(jnp.float32) for e, r in zip(es, ob_refs)])
            parts.append(num / den)
        ob = jnp.concatenate(parts, axis=1)
    else:
        ob = slabs(ob_refs[0])
    oa = slabs(oa_ref)
    ya = jnp.dot((oa * gate(0, A_WIDTH, True)).astype(jnp.bfloat16), wa_ref[...],
                 preferred_element_type=jnp.float32)
    yb = jnp.dot((ob * gate(A_WIDTH, B_WIDTH, True)).astype(jnp.bfloat16), wb_ref[...],
                 preferred_element_type=jnp.float32)
    m0 = A_WIDTH + B_WIDTH
    merged = (gate(m0, D_MODEL, False) * ya + gate(m0 + D_MODEL, D_MODEL, False) * yb).astype(jnp.bfloat16)
    y_ref[0] = x + jnp.dot(merged, wo_ref[...], preferred_element_type=jnp.float32)


def _out_proj(x3d, ng, wg, oa, obs, lses, wa, wb, wo, *, tm):
    nb, seq, _ = x3d.shape
    row = lambda b, i: (b, i, 0)
    slab = pl.BlockSpec((1, N_PAIRS, tm, LANES), lambda b, i: (b, 0, i, 0))
    const = lambda b, i: (0, 0)
    once = dict(pipeline_mode=pl.Buffered(1))
    in_specs = [pl.BlockSpec((1, tm, D_MODEL), row), pl.BlockSpec((1, D_MODEL), const),
                pl.BlockSpec(wg.shape, const, **once), slab]
    in_specs += [slab for _ in obs] + [slab for _ in lses]
    in_specs += [pl.BlockSpec(wa.shape, const, **once), pl.BlockSpec(wb.shape, const, **once),
                 pl.BlockSpec(wo.shape, const, **once)]
    return pl.pallas_call(
        functools.partial(_out_kernel, n_groups=len(obs)),
        grid=(nb, seq // tm),
        in_specs=in_specs,
        out_specs=pl.BlockSpec((1, tm, D_MODEL), row),
        out_shape=jax.ShapeDtypeStruct((nb, seq, D_MODEL), jnp.float32),
        compiler_params=pltpu.CompilerParams(
            dimension_semantics=("arbitrary", "arbitrary"), vmem_limit_bytes=VMEM_LIMIT),
        name="out_proj",
    )(x3d, ng, wg, oa, *obs, *lses, wa, wb, wo)


def _prep_params(w_in, q_gain_a, k_gain_a, q_gain_b, k_gain_b, w_up_a, w_up_b):
    offs = np.cumsum((0, A_WIDTH, A_KV_WIDTH, A_KV_WIDTH, A_WIDTH, 3 * B_WIDTH, 3 * B_WIDTH, 3 * B_WIDTH,
                      B_WIDTH, D_MODEL, D_MODEL))
    seg = lambda s: w_in[:, int(offs[s]):int(offs[s + 1])]

    def perm_heads(w2d):
        return jnp.concatenate([w2d[:, h * HEAD_DIM:(h + 1) * HEAD_DIM] for h in A_HEAD_ORDER], axis=1)

    w_qkv = jnp.concatenate([perm_heads(seg(0)), seg(4), seg(1), seg(5), seg(2), seg(6)],
                            axis=1).astype(jnp.bfloat16)
    w_gate = jnp.concatenate([perm_heads(seg(3)), seg(7), seg(8), seg(9)], axis=1).astype(jnp.bfloat16)
    gq = jnp.broadcast_to(jnp.concatenate([q_gain_a[None], q_gain_b], axis=0)[:, None, :] * Q_SCALE,
                          (1 + N_B_GROUPS, A_Q_HEADS, HEAD_DIM)).reshape(1, Q_COLS)
    gk = jnp.concatenate([jnp.broadcast_to(k_gain_a, (A_KV_HEADS, HEAD_DIM)).reshape(A_KV_WIDTH),
                          jnp.broadcast_to(k_gain_b[:, None, :], (N_B_GROUPS, B_HEADS, HEAD_DIM)).reshape(-1)])[None]
    hd = np.arange(MXU_COLS) // HEAD_DIM
    ones = jnp.asarray((hd[:, None] == hd[None, :]).astype(np.float32) / HEAD_DIM, jnp.bfloat16)
    wa = jnp.concatenate([w_up_a[h * HEAD_DIM:(h + 1) * HEAD_DIM] for h in A_HEAD_ORDER], axis=0)
    return w_qkv, w_gate, gq, gk, ones, wa.astype(jnp.bfloat16), w_up_b.astype(jnp.bfloat16)


def kernel(x_prompt, x_sample, cache_a_kv, cache_b1_kv, cache_b2_kv, cache_b3_kv, rel_bias, norm_gain, w_in,
           q_gain_a, k_gain_a, sinks_a, q_gain_b, k_gain_b, w_up_a, w_up_b, w_out):
    assert norm_gain.shape[0] == 1, "single layer"
    nb, seq, _ = x_prompt.shape
    n_dec, dec_seq, _ = x_sample.shape
    w_qkv, w_gate, gq, gk, ones, wa, wb = _prep_params(
        w_in[0], q_gain_a[0], k_gain_a[0], q_gain_b[0], k_gain_b[0], w_up_a[0], w_up_b[0])
    wo = w_out[0].astype(jnp.bfloat16)
    ng = norm_gain
    windows = (A_WINDOW,) + tuple(w for w, _ in B_GROUPS)
    dils = (1,) + tuple(d for _, d in B_GROUPS)
    head_cols = (A_HEAD_ORDER,) + tuple(tuple(range(A_Q_HEADS + g * B_HEADS, A_Q_HEADS + (g + 1) * B_HEADS))
                                        for g in range(N_B_GROUPS))

    p_rows = tuple(min(w, seq) for w in windows)
    qs, ks, vs, states = _qkv_proj(x_prompt.reshape(nb * seq, D_MODEL), ng, w_qkv, gq, gk, ones,
                                   nb=nb, seq=seq, state_rows=p_rows, dils=dils, tm=512, states_t=True)
    t_dec = n_dec * dec_seq
    qd, _, _, news = _qkv_proj(x_sample.reshape(t_dec, D_MODEL), ng, w_qkv, gq, gk, ones,
                               nb=1, seq=t_dec, state_rows=(t_dec,) * 4, dils=(1, 1, 1, 1), tm=t_dec,
                               states_t=False)

    q_dec = jnp.concatenate([a.reshape(n_dec, dec_seq, A_WIDTH) for a in qd], axis=2)
    caches = [jnp.transpose(c[0], (0, 2, 3, 4, 1)) for c in (cache_a_kv, cache_b1_kv, cache_b2_kv, cache_b3_kv)]
    news3 = [a.reshape(n_dec, dec_seq, a.shape[1]) for a in news]
    oa_s, ob_s, o_prompt, lses = _decode_band_attention(
        q_dec, news3, caches, qs, ks, vs, rel_bias, sinks_a, dils=dils, head_cols=head_cols, sink_idx=A_HEAD_ORDER)

    y_prompt = _out_proj(x_prompt, ng, w_gate, o_prompt[0], list(o_prompt[1:]), list(lses), wa, wb, wo, tm=512)
    heads = (A_KV_HEADS, B_HEADS, B_HEADS, B_HEADS)
    new_prompt = tuple(jnp.transpose(states[g].reshape(1, nb, 2, heads[g], HEAD_DIM, p_rows[g]), (0, 1, 5, 2, 3, 4))
                       for g in range(4))
    to_slabs = lambda o: jnp.transpose(o, (1, 0, 2, 3)).reshape(1, N_PAIRS, t_dec, LANES)
    y_sample = _out_proj(x_sample.reshape(1, t_dec, D_MODEL), ng, w_gate, to_slabs(oa_s), [to_slabs(ob_s)], [],
                         wa, wb, wo, tm=t_dec)
    y_sample = y_sample.reshape(n_dec, dec_seq, D_MODEL)
    new_sample = tuple(news3[g].reshape(1, n_dec, dec_seq, 2, heads[g], HEAD_DIM) for g in range(4))
    return (y_prompt, y_sample) + new_prompt + new_sample
```

```python
import functools
import math

import numpy as np
import jax
import jax.numpy as jnp
from jax import lax
from jax.experimental import pallas as pl
from jax.experimental.pallas import tpu as pltpu

D_MODEL = 1024
HEAD_DIM = 64
A_Q_HEADS = 8
A_KV_HEADS = 2
A_WINDOW = 128
B_GROUPS = ((128, 1), (512, 4), (2048, 16))
N_B_GROUPS = 3
B_HEADS = 8
N_KEYS = 128
A_WIDTH = A_Q_HEADS * HEAD_DIM
A_KV_WIDTH = A_KV_HEADS * HEAD_DIM
B_WIDTH = B_HEADS * HEAD_DIM
N_BUCKETS = 32
MAX_DISTANCE = 2048
EPS = 1e-6
NEG_INF = -1e30
Q_SCALE = HEAD_DIM ** -0.5

LANES = 128
MXU_COLS = 256
VMEM_LIMIT = 56 * 1024 * 1024
N_PAIRS = A_WIDTH // LANES

A_HEAD_ORDER = (0, 4, 1, 5, 2, 6, 3, 7)

Q_COLS = A_WIDTH + N_B_GROUPS * B_WIDTH
KV_COLS = A_KV_WIDTH + N_B_GROUPS * B_WIDTH
QKV_COLS = Q_COLS + 2 * KV_COLS
G_COLS = A_WIDTH + B_WIDTH + 2 * D_MODEL
KV_WIDTHS = (A_KV_WIDTH, B_WIDTH, B_WIDTH, B_WIDTH)
NORM_BATCH = 4 * B_WIDTH // MXU_COLS


def _t5_bucket_np(dist):
    max_exact = N_BUCKETS // 2
    d = np.maximum(dist, 0)
    df = np.maximum(d, 1).astype(np.float32)
    large = max_exact + (np.log(df / np.float32(max_exact)) / np.float32(math.log(MAX_DISTANCE / max_exact))
                         * np.float32(N_BUCKETS - max_exact)).astype(np.int32)
    large = np.minimum(large, N_BUCKETS - 1)
    return np.where(d < max_exact, d, large)


def _rmsnorm_bf16(x, gain):
    ms = jnp.mean(x * x, axis=-1, keepdims=True)
    return (x * lax.rsqrt(ms + EPS) * gain).astype(jnp.bfloat16)


def _state_plan(n_rows, seq, tm):
    tpb = seq // tm
    r = min(n_rows, tm)
    nblk = max(n_rows // tm, 1)
    return tpb, r, nblk, tpb - nblk


def _qkv_kernel(x_ref, ng_ref, w_ref, gq_ref, gk_ref, ones_ref, *rest, tm, seq, state_rows, dils, states_t):
    q_refs = rest[0:4]
    k_refs = rest[4:8]
    v_refs = rest[8:12]
    state_refs = rest[12:16]
    tmp_ref, a_scr, sq_scr, ss_scr = rest[16:20]
    h = _rmsnorm_bf16(x_ref[...], ng_ref[...])
    ones = ones_ref[...]

    def proj(col0, w):
        return jnp.dot(h, w_ref[:, col0:col0 + w], preferred_element_type=jnp.float32)

    def emit(a, c, out_ref, d, state):
        w = a.shape[1]
        if d == 1:
            out_ref[0, 0, :, c:c + w] = a.astype(out_ref.dtype)
        else:
            for s in range(w // LANES):
                tmp_ref[c // LANES + s] = a[:, s * LANES:(s + 1) * LANES]
        if state is not None:
            sref, kv, r = state
            if states_t:
                sref[0, kv, c:c + w, :] = a[tm - r:, :].T
            else:
                wd = sref.shape[1] // 2
                sref[:, kv * wd + c:kv * wd + c + w] = a[tm - r:, :]

    def finish(out_ref, d, width):
        if d > 1:
            for cls in range(d):
                for s in range(width // LANES):
                    out_ref[0, cls, :, s * LANES:(s + 1) * LANES] = (
                        tmp_ref[s, pl.ds(cls, tm // d, stride=d), :].astype(out_ref.dtype))

    def normed_batch(arrays):
        chunks = [(ai, c) for ai in range(len(arrays)) for c in range(0, B_WIDTH, MXU_COLS)]
        for ci, (ai, c) in enumerate(chunks):
            a = proj(arrays[ai][0] + c, MXU_COLS)
            a_scr[ci] = a
            sq_scr[ci * tm:(ci + 1) * tm, :] = (a * a).astype(jnp.bfloat16)
        n = len(chunks) * tm
        ss_scr[0:n, :] = jnp.dot(sq_scr[0:n, :], ones, preferred_element_type=jnp.float32)
        for ci, (ai, c) in enumerate(chunks):
            _, gain_ref, gcol0, out_ref, d, state = arrays[ai]
            o = (a_scr[ci] * lax.rsqrt(ss_scr[ci * tm:(ci + 1) * tm, :] + EPS)
                 * gain_ref[:, gcol0 + c:gcol0 + c + MXU_COLS])
            emit(o, c, out_ref, d, state)
            if c + MXU_COLS == B_WIDTH:
                finish(out_ref, d, B_WIDTH)

    q_off = (0, A_WIDTH, A_WIDTH + B_WIDTH, A_WIDTH + 2 * B_WIDTH)
    kv_off = (0, A_KV_WIDTH, A_KV_WIDTH + B_WIDTH, A_KV_WIDTH + 2 * B_WIDTH)
    plan = [_state_plan(state_rows[g], seq, tm)[1] for g in range(4)]
    normed_batch([(q_off[g], gq_ref, q_off[g], q_refs[g], dils[g], None) for g in range(4)])
    normed_batch([(Q_COLS + kv_off[g], gk_ref, kv_off[g], k_refs[g], dils[g], (state_refs[g], 0, plan[g]))
                  for g in range(1, 4)])
    a = proj(Q_COLS, A_KV_WIDTH)
    ss = jnp.dot((a * a).astype(jnp.bfloat16), ones[:A_KV_WIDTH, :A_KV_WIDTH], preferred_element_type=jnp.float32)
    emit(a * lax.rsqrt(ss + EPS) * gk_ref[:, 0:A_KV_WIDTH], 0, k_refs[0], dils[0], (state_refs[0], 0, plan[0]))
    for g in range(4):
        wd = KV_WIDTHS[g]
        for c in range(0, wd, MXU_COLS):
            w = min(MXU_COLS, wd - c)
            emit(proj(Q_COLS + KV_COLS + kv_off[g] + c, w), c, v_refs[g], dils[g], (state_refs[g], 1, plan[g]))
        finish(v_refs[g], dils[g], wd)


def _qkv_proj(x2d, ng, w_qkv, gq, gk, ones, *, nb, seq, state_rows, dils, tm, states_t):
    t = x2d.shape[0]
    tpb = seq // tm

    def cls_shape(d, width):
        return jax.ShapeDtypeStruct((nb, d, seq // d, width), jnp.bfloat16)

    def cls_spec(d, width):
        return pl.BlockSpec((1, d, tm // d, width), lambda i: (i // tpb, 0, i % tpb, 0))

    def state_shape(g):
        if states_t:
            return jax.ShapeDtypeStruct((nb, 2, KV_WIDTHS[g], state_rows[g]), jnp.float32)
        return jax.ShapeDtypeStruct((nb * state_rows[g], 2 * KV_WIDTHS[g]), jnp.float32)

    def state_spec(g):
        _, r, nblk, j0 = _state_plan(state_rows[g], seq, tm)
        if states_t:
            return pl.BlockSpec((1, 2, KV_WIDTHS[g], r), lambda i: (i // tpb, 0, 0, jnp.maximum(i % tpb - j0, 0)))
        return pl.BlockSpec((r, 2 * KV_WIDTHS[g]),
                            lambda i: ((i // tpb) * nblk + jnp.maximum(i % tpb - j0, 0), 0))

    const = lambda i: (0, 0)
    out_shape = ([cls_shape(d, A_WIDTH) for d in dils]
                 + [cls_shape(d, w) for d, w in zip(dils, KV_WIDTHS)] * 2
                 + [state_shape(g) for g in range(4)])
    out_specs = ([cls_spec(d, A_WIDTH) for d in dils]
                 + [cls_spec(d, w) for d, w in zip(dils, KV_WIDTHS)] * 2
                 + [state_spec(g) for g in range(4)])
    res = pl.pallas_call(
        functools.partial(_qkv_kernel, tm=tm, seq=seq, state_rows=state_rows, dils=dils, states_t=states_t),
        grid=(t // tm,),
        in_specs=[
            pl.BlockSpec((tm, D_MODEL), lambda i: (i, 0)),
            pl.BlockSpec((1, D_MODEL), const),
            pl.BlockSpec((D_MODEL, QKV_COLS), const, pipeline_mode=pl.Buffered(1)),
            pl.BlockSpec((1, Q_COLS), const),
            pl.BlockSpec((1, KV_COLS), const),
            pl.BlockSpec((MXU_COLS, MXU_COLS), const),
        ],
        out_specs=out_specs,
        out_shape=out_shape,
        scratch_shapes=[pltpu.VMEM((N_PAIRS, tm, LANES), jnp.float32),
                        pltpu.VMEM((NORM_BATCH, tm, MXU_COLS), jnp.float32),
                        pltpu.VMEM((NORM_BATCH * tm, MXU_COLS), jnp.bfloat16),
                        pltpu.VMEM((NORM_BATCH * tm, MXU_COLS), jnp.float32)],
        compiler_params=pltpu.CompilerParams(
            dimension_semantics=("arbitrary",), vmem_limit_bytes=VMEM_LIMIT),
        name="qkv_proj",
    )(x2d, ng, w_qkv, gq, gk, ones)
    return res[0:4], res[4:8], res[8:12], res[12:16]


def _lookup_rows(bucket_map, top, table_ref, col_top, col_bottom, fill):
    acc = jnp.full(bucket_map.shape, fill, jnp.float32)
    for b in range(N_BUCKETS):
        val = jnp.where(top, table_ref[b, col_top], table_ref[b, col_bottom])
        acc = jnp.where(bucket_map == b, val, acc)
    return acc


def _band_block(q_ref, k_ref, kp_ref, v_ref, vp_ref, o_ref, lse_ref, bias_ref, sink_ref, *,
                cls, first, dil, shared_kv):
    blk = N_KEYS
    row = lax.broadcasted_iota(jnp.int32, (2 * blk, blk), 0)
    lane = lax.broadcasted_iota(jnp.int32, (2 * blk, blk), 1)
    ahead = lane - jnp.bitwise_and(row, blk - 1)
    tri = ahead <= 0
    valid = ahead <= jnp.where(first, 0, blk)
    cur_part = tri.astype(jnp.bfloat16)
    prev_part = jnp.logical_not(tri).astype(jnp.bfloat16)
    left = lax.broadcasted_iota(jnp.int32, (blk, LANES), 1) < HEAD_DIM
    mask_l = left.astype(jnp.bfloat16)
    mask_r = jnp.logical_not(left).astype(jnp.bfloat16)
    ones_kv = jnp.ones((2 * blk, LANES), jnp.bfloat16)
    nt = (((1,), (1,)), ((), ()))
    rows = pl.ds(0, blk) if dil == 1 else pl.ds(cls, blk, stride=dil)
    for p in range(N_PAIRS):
        kcol = 0 if shared_kv else p * LANES
        qp = q_ref[0, 0, :, p * LANES:(p + 1) * LANES]
        qs = jnp.concatenate([qp * mask_l, qp * mask_r], axis=0)
        kc = jnp.concatenate([kp_ref[0, 0, :, kcol:kcol + LANES], k_ref[0, 0, :, kcol:kcol + LANES]], axis=0)
        vc = jnp.concatenate([vp_ref[0, 0, :, kcol:kcol + LANES], v_ref[0, 0, :, kcol:kcol + LANES]], axis=0)
        s2 = lax.dot_general(qs, kc, nt, preferred_element_type=jnp.float32)
        s = jnp.where(tri, s2[:, blk:], s2[:, :blk]) + bias_ref[p]
        s = jnp.where(valid, s, NEG_INF)
        m = jnp.max(s, axis=-1, keepdims=True)
        if sink_ref is not None:
            sk = sink_ref[p]
            m = jnp.maximum(m, sk)
        pb = jnp.exp(s - m).astype(jnp.bfloat16)
        p2 = jnp.concatenate([pb * prev_part, pb * cur_part], axis=1)
        ov = jnp.dot(p2, jnp.concatenate([vc, ones_kv], axis=1), preferred_element_type=jnp.float32)
        l = ov[:, LANES:]
        if sink_ref is not None:
            l = l + jnp.exp(sk - m)
        num = jnp.where(left, ov[:blk, :LANES], ov[blk:, :LANES])
        den = jnp.where(left, l[:blk], l[blk:])
        o_ref[0, p, rows, :] = num / den
        if lse_ref is not None:
            m1 = jnp.where(left, m[:blk], m[blk:])
            lse_ref[0, p, rows, :] = m1 + jnp.log(den)


def _decode_kernel(q_ref, na_ref, nb1_ref, nb2_ref, nb3_ref, ca_ref, cb1_ref, cb2_ref, cb3_ref,
                   ma_ref, mb1_ref, mb2_ref, mb3_ref, mn_ref, relb_ref, sinks_ref, oa_ref, ob_ref,
                   ta_ref, tb1_ref, tb2_ref, tb3_ref, tn_ref, sink_ref, *, dec_seq, head_cols, sink_idx, dils, init):
    t = dec_seq
    if init:
        for g, (m_ref, t_ref) in enumerate(zip((ma_ref, mb1_ref, mb2_ref, mb3_ref),
                                               (ta_ref, tb1_ref, tb2_ref, tb3_ref))):
            top = lax.broadcasted_iota(jnp.int32, m_ref.shape, 0) < t
            top_n = lax.broadcasted_iota(jnp.int32, (2 * t, LANES), 0) < t
            for p in range(N_PAIRS):
                cols = (head_cols[g][2 * p], head_cols[g][2 * p + 1])
                rows = slice(p * 2 * t, (p + 1) * 2 * t)
                t_ref[rows, :] = _lookup_rows(m_ref[...], top, relb_ref, cols[0], cols[1], NEG_INF)
                tn_ref[g, rows, :] = _lookup_rows(mn_ref[g], top_n, relb_ref, cols[0], cols[1], NEG_INF)
                if g == 0:
                    sink_ref[rows, :] = jnp.where(top_n, sinks_ref[0, sink_idx[2 * p]],
                                                  sinks_ref[0, sink_idx[2 * p + 1]])
        return

    left8 = lax.broadcasted_iota(jnp.int32, (2 * t, LANES), 1) < HEAD_DIM
    top8 = lax.broadcasted_iota(jnp.int32, (2 * t, LANES), 0) < t
    own = (left8 == top8).astype(jnp.float32)
    left4 = lax.broadcasted_iota(jnp.int32, (t, LANES), 1) < HEAD_DIM
    nt = (((1,), (1,)), ((), ()))

    def q_rows(c0):
        qp = q_ref[0, :, c0:c0 + LANES].astype(jnp.float32)
        return jnp.concatenate([qp, qp], axis=0) * own

    def attend(qrs, kts, vts, knews, vnews, tbl, tbl_new, sink, dil):
        shared = len(kts) == 1
        rows8 = lambda a, u: jnp.broadcast_to(a[u:u + 1, :], (2 * t, LANES))
        qr = jnp.concatenate(qrs, axis=0)
        qb = qr.astype(jnp.bfloat16)
        if shared:
            s_c = jnp.dot(qb, kts[0], preferred_element_type=jnp.float32)
        else:
            s_c = jnp.concatenate([jnp.dot(qb[2 * t * p:2 * t * (p + 1)], kts[p], preferred_element_type=jnp.float32)
                                   for p in range(N_PAIRS)], axis=0)
        yield
        s_c = s_c + tbl
        kn = [jnp.concatenate([rows8(knews[0 if shared else p], u) for p in range(N_PAIRS)], axis=0)
              for u in range(t)]
        vn = [jnp.concatenate([rows8(vnews[0 if shared else p], u) for p in range(N_PAIRS)], axis=0)
              for u in range(t)]
        s_n = [jnp.sum(qr * kn[u], axis=-1, keepdims=True) + tbl_new[:, u:u + 1] for u in range(t)]
        m = functools.reduce(jnp.maximum, s_n, jnp.max(s_c, axis=-1, keepdims=True))
        if sink is not None:
            m = jnp.maximum(m, sink)
        pc = jnp.exp(s_c - m)
        pn = [jnp.exp(x - m) for x in s_n]
        l = functools.reduce(lambda a, b: a + b, pn, jnp.sum(pc, axis=-1, keepdims=True))
        if sink is not None:
            l = l + jnp.exp(sink - m)
        pb = pc.astype(jnp.bfloat16)
        yield
        if shared:
            o = lax.dot_general(pb, vts[0], nt, preferred_element_type=jnp.float32)
        elif dil >= t:
            lane8 = lax.broadcasted_iota(jnp.int32, (2 * t, LANES), 1)
            row8 = lax.broadcasted_iota(jnp.int32, (2 * t, LANES), 0)
            sel = (lane8 % dil == row8 % t).astype(jnp.bfloat16)
            parts = []
            for p in range(N_PAIRS):
                accs = []
                for side in range(2):
                    r0 = 2 * t * p + t * side
                    w = jnp.sum(pc[r0:r0 + t], axis=0, keepdims=True)
                    v = vts[p][side * HEAD_DIM:(side + 1) * HEAD_DIM]
                    accs.append(functools.reduce(
                        lambda a, b: a + b,
                        [v[:, c:c + LANES] * w[:, c:c + LANES] for c in range(0, v.shape[1], LANES)]))
                acc = jnp.concatenate(accs, axis=0)
                hi = acc.astype(jnp.bfloat16)
                lo = (acc - hi.astype(jnp.float32)).astype(jnp.bfloat16)
                parts.append(lax.dot_general(sel, hi, nt, preferred_element_type=jnp.float32)
                             + lax.dot_general(sel, lo, nt, preferred_element_type=jnp.float32))
            o = jnp.concatenate(parts, axis=0)
        else:
            o = jnp.concatenate([lax.dot_general(pb[2 * t * p:2 * t * (p + 1)], vts[p], nt,
                                                 preferred_element_type=jnp.float32)
                                 for p in range(N_PAIRS)], axis=0)
        for u in range(t):
            o = o + pn[u] * vn[u]
        yield o / l, m + jnp.log(l)

    def fold(x, p):
        return jnp.where(left4, x[2 * t * p:2 * t * p + t], x[2 * t * p + t:2 * t * (p + 1)])

    def cache_kv(c_ref, kv, p, dtype=jnp.bfloat16):
        return c_ref[0, kv, 2 * p:2 * p + 2].reshape(LANES, c_ref.shape[-1]).astype(dtype)

    caches = (cb1_ref, cb2_ref, cb3_ref)
    news = (nb1_ref, nb2_ref, nb3_ref)
    tbls = (tb1_ref, tb2_ref, tb3_ref)
    groups = [attend([q_rows(p * LANES) for p in range(N_PAIRS)],
                     [cache_kv(ca_ref, 0, 0)], [cache_kv(ca_ref, 1, 0)],
                     [na_ref[0, :, 0:LANES]], [na_ref[0, :, LANES:2 * LANES]],
                     ta_ref[...], tn_ref[0], sink_ref[:, 0:1], dils[0])]
    for g in range(N_B_GROUPS):
        groups.append(attend(
            [q_rows(A_WIDTH + g * B_WIDTH + p * LANES) for p in range(N_PAIRS)],
            [cache_kv(caches[g], 0, p) for p in range(N_PAIRS)],
            [cache_kv(caches[g], 1, p, jnp.float32 if dils[1 + g] >= t else jnp.bfloat16) for p in range(N_PAIRS)],
            [news[g][0, :, p * LANES:(p + 1) * LANES] for p in range(N_PAIRS)],
            [news[g][0, :, B_WIDTH + p * LANES:B_WIDTH + (p + 1) * LANES] for p in range(N_PAIRS)],
            tbls[g][...], tn_ref[1 + g], None, dils[1 + g]))
    for phase in range(2):
        for gen in groups:
            next(gen)
    results = [next(gen) for gen in groups]
    oa = results[0][0]
    for p in range(N_PAIRS):
        oa_ref[0, p] = fold(oa, p).astype(oa_ref.dtype)
    outs = [o for o, _ in results[1:]]
    lses = [jnp.broadcast_to(lse, o.shape) for o, lse in results[1:]]
    mx = jnp.maximum(jnp.maximum(lses[0], lses[1]), lses[2])
    es = [jnp.exp(x - mx) for x in lses]
    den = es[0] + es[1] + es[2]
    comb = (es[0] * outs[0] + es[1] * outs[1] + es[2] * outs[2]) / den
    for p in range(N_PAIRS):
        ob_ref[0, p] = fold(comb, p).astype(ob_ref.dtype)


def _decode_bucket_maps(dils, cache_lens, dec_seq):
    t_idx = np.arange(dec_seq)
    maps = []
    new = np.full((len(dils), 2 * dec_seq, LANES), -1, np.int32)
    for g, (d, ln) in enumerate(zip(dils, cache_lens)):
        back = ln + t_idx[:, None] - np.arange(ln)[None, :]
        ok = (back % d == 0) & (back // d >= 1) & (back // d < N_KEYS)
        maps.append(jnp.asarray(np.tile(np.where(ok, _t5_bucket_np(back), -1), (2, 1)).astype(np.int32)))
        backn = t_idx[:, None] - t_idx[None, :]
        okn = (backn >= 0) & (backn % d == 0) & (backn // d < N_KEYS)
        new[g, :, :dec_seq] = np.tile(np.where(okn, _t5_bucket_np(backn), -1), (2, 1))
    return maps, jnp.asarray(new)


def _decode_band_kernel(*refs, dec_seq, head_cols, sink_idx, dils, blocks_per_batch):
    n_g = len(dils)
    dec_in = refs[0:16]
    band_in = [refs[16 + 6 * g:16 + 6 * (g + 1)] for g in range(n_g)]
    outs = refs[16 + 6 * n_g:16 + 6 * n_g + 2 + 2 * n_g - 1]
    oa_ref, ob_ref = outs[0:2]
    o_refs = outs[2:2 + n_g]
    lse_refs = (None,) + tuple(outs[2 + n_g:])
    scratch = refs[16 + 6 * n_g + 2 + 2 * n_g - 1:]
    dec_scratch = scratch[0:6]
    bias_refs = scratch[6:6 + n_g]
    sink_tile = scratch[6 + n_g]
    blk = N_KEYS
    s = pl.program_id(0)
    decode = functools.partial(_decode_kernel, *dec_in, oa_ref, ob_ref, *dec_scratch, dec_seq=dec_seq,
                               head_cols=head_cols, sink_idx=sink_idx, dils=dils)

    @pl.when(s == 0)
    def _():
        top = lax.broadcasted_iota(jnp.int32, (2 * blk, blk), 0) < blk
        for g in range(n_g):
            bmap = band_in[g][5][...]
            for p in range(N_PAIRS):
                bias_refs[g][p] = _lookup_rows(bmap, top, dec_in[14], head_cols[g][2 * p], head_cols[g][2 * p + 1],
                                               0.0)
        for p in range(N_PAIRS):
            sink_tile[p] = jnp.where(top, dec_in[15][0, sink_idx[2 * p]], dec_in[15][0, sink_idx[2 * p + 1]])
        decode(init=True)

    r = s % blocks_per_batch

    for g in range(n_g):
        q_ref, k_ref, kp_ref, v_ref, vp_ref, _ = band_in[g]
        _band_block(q_ref, k_ref, kp_ref, v_ref, vp_ref, o_refs[g], lse_refs[g], bias_refs[g],
                    sink_tile if g == 0 else None, cls=r % dils[g], first=(r // dils[g]) == 0, dil=dils[g],
                    shared_kv=(g == 0))
    decode(init=False)


def _decode_band_attention(q_dec, news, caches, qs, ks, vs, rel_bias, sinks, *, dils, head_cols, sink_idx):
    n, t, _ = q_dec.shape
    nb = qs[0].shape[0]
    seq = qs[0].shape[1] * qs[0].shape[2]
    ppb = seq // N_KEYS
    assert n == nb * ppb, "one prompt block of every group per sample sequence"
    seq3 = lambda i: (i, 0, 0)
    seq4 = lambda i: (i, 0, 0, 0)
    seq5 = lambda i: (i, 0, 0, 0, 0)
    cache_lens = tuple(c.shape[-1] for c in caches)
    maps, map_new = _decode_bucket_maps(dils, cache_lens, t)
    smem = functools.partial(pl.BlockSpec, memory_space=pltpu.SMEM)
    in_specs = [pl.BlockSpec((1, t, q_dec.shape[2]), seq3)]
    in_specs += [pl.BlockSpec((1, t, a.shape[2]), seq3) for a in news]
    in_specs += [pl.BlockSpec((1,) + c.shape[1:], seq5) for c in caches]
    in_specs += [pl.BlockSpec(m.shape, lambda i: (0, 0)) for m in maps]
    in_specs += [pl.BlockSpec(map_new.shape, lambda i: (0, 0, 0)), smem(), smem()]
    band_args = []
    out_specs = [pl.BlockSpec((1, N_PAIRS, t, LANES), seq4)] * 2
    out_shape = [jax.ShapeDtypeStruct((n, N_PAIRS, t, LANES), jnp.float32)] * 2
    qi = np.arange(N_KEYS)[:, None]
    ci = np.arange(N_KEYS)[None, :]
    for g, d in enumerate(dils):
        cur = lambda s, d=d: (s // ppb, (s % ppb) % d, (s % ppb) // d, 0)
        prev = lambda s, d=d: (s // ppb, (s % ppb) % d, jnp.maximum((s % ppb) // d - 1, 0), 0)
        wq, wkv = qs[g].shape[3], ks[g].shape[3]
        bmap = jnp.asarray(np.tile(_t5_bucket_np(((qi - ci) % N_KEYS) * d), (2, 1)).astype(np.int32))
        in_specs += [pl.BlockSpec((1, 1, N_KEYS, wq), cur), pl.BlockSpec((1, 1, N_KEYS, wkv), cur),
                     pl.BlockSpec((1, 1, N_KEYS, wkv), prev), pl.BlockSpec((1, 1, N_KEYS, wkv), cur),
                     pl.BlockSpec((1, 1, N_KEYS, wkv), prev), pl.BlockSpec(bmap.shape, lambda s: (0, 0))]
        band_args += [qs[g], ks[g], ks[g], vs[g], vs[g], bmap]
    tile = lambda d: pl.BlockSpec((1, N_PAIRS, N_KEYS * d, LANES), lambda s, d=d: (s // ppb, 0, (s % ppb) // d, 0))
    slab_shape = jax.ShapeDtypeStruct((nb, N_PAIRS, seq, LANES), jnp.float32)
    out_specs += [tile(d) for d in dils] + [tile(d) for d in dils[1:]]
    out_shape += [slab_shape] * (2 * len(dils) - 1)
    n_rows = 2 * t * N_PAIRS
    tile_scr = pltpu.VMEM((N_PAIRS, 2 * N_KEYS, N_KEYS), jnp.float32)
    res = pl.pallas_call(
        functools.partial(_decode_band_kernel, dec_seq=t, head_cols=head_cols, sink_idx=sink_idx, dils=dils,
                          blocks_per_batch=ppb),
        grid=(n,),
        in_specs=in_specs,
        out_specs=out_specs,
        out_shape=out_shape,
        scratch_shapes=[pltpu.VMEM((n_rows, ln), jnp.float32) for ln in cache_lens]
        + [pltpu.VMEM((len(dils), n_rows, LANES), jnp.float32), pltpu.VMEM((n_rows, LANES), jnp.float32)]
        + [tile_scr] * (len(dils) + 1),
        compiler_params=pltpu.CompilerParams(
            dimension_semantics=("arbitrary",), vmem_limit_bytes=VMEM_LIMIT),
        name="decode_band_attn",
    )(q_dec, *news, *caches, *maps, map_new, rel_bias, sinks, *band_args)
    ng = len(dils)
    return res[0], res[1], res[2:2 + ng], res[2 + ng:]


def _out_kernel(*refs, n_groups):
    x_ref, ng_ref, wg_ref, oa_ref = refs[0:4]
    ob_refs = refs[4:4 + n_groups]
    lse_refs = refs[4 + n_groups:4 + 2 * n_groups] if n_groups > 1 else ()
    wa_ref, wb_ref, wo_ref, y_ref = refs[-4:]
    x = x_ref[0]
    h = _rmsnorm_bf16(x, ng_ref[...])

    def gate(c0, width, silu):
        a = jnp.dot(h, wg_ref[:, c0:c0 + width], preferred_element_type=jnp.float32)
        sg = 1.0 / (1.0 + jnp.exp(-a))
        return a * sg if silu else sg

    def slabs(ref):
        return jnp.concatenate([ref[0, p].astype(jnp.float32) for p in range(N_PAIRS)], axis=1)

    if n_groups > 1:
        parts = []
        for p in range(N_PAIRS):
            lses = [r[0, p] for r in lse_refs]
            mx = functools.reduce(jnp.maximum, lses)
            es = [jnp.exp(v - mx) for v in lses]
            den = functools.reduce(lambda a, b: a + b, es)
            num = functools.reduce(lambda a, b: a + b,
                                   [e * r[0, p].astype(jnp.float32) for e, r in zip(es, ob_refs)])
            parts.append(num / den)
        ob = jnp.concatenate(parts, axis=1)
    else:
        ob = slabs(ob_refs[0])
    oa = slabs(oa_ref)
    ya = jnp.dot((oa * gate(0, A_WIDTH, True)).astype(jnp.bfloat16), wa_ref[...],
                 preferred_element_type=jnp.float32)
    yb = jnp.dot((ob * gate(A_WIDTH, B_WIDTH, True)).astype(jnp.bfloat16), wb_ref[...],
                 preferred_element_type=jnp.float32)
    m0 = A_WIDTH + B_WIDTH
    merged = (gate(m0, D_MODEL, False) * ya + gate(m0 + D_MODEL, D_MODEL, False) * yb).astype(jnp.bfloat16)
    y_ref[0] = x + jnp.dot(merged, wo_ref[...], preferred_element_type=jnp.float32)


def _out_proj(x3d, ng, wg, oa, obs, lses, wa, wb, wo, *, tm):
    nb, seq, _ = x3d.shape
    row = lambda b, i: (b, i, 0)
    slab = pl.BlockSpec((1, N_PAIRS, tm, LANES), lambda b, i: (b, 0, i, 0))
    const = lambda b, i: (0, 0)
    once = dict(pipeline_mode=pl.Buffered(1))
    in_specs = [pl.BlockSpec((1, tm, D_MODEL), row), pl.BlockSpec((1, D_MODEL), const),
                pl.BlockSpec(wg.shape, const, **once), slab]
    in_specs += [slab for _ in obs] + [slab for _ in lses]
    in_specs += [pl.BlockSpec(wa.shape, const, **once), pl.BlockSpec(wb.shape, const, **once),
                 pl.BlockSpec(wo.shape, const, **once)]
    return pl.pallas_call(
        functools.partial(_out_kernel, n_groups=len(obs)),
        grid=(nb, seq // tm),
        in_specs=in_specs,
        out_specs=pl.BlockSpec((1, tm, D_MODEL), row),
        out_shape=jax.ShapeDtypeStruct((nb, seq, D_MODEL), jnp.float32),
        compiler_params=pltpu.CompilerParams(
            dimension_semantics=("arbitrary", "arbitrary"), vmem_limit_bytes=VMEM_LIMIT),
        name="out_proj",
    )(x3d, ng, wg, oa, *obs, *lses, wa, wb, wo)


def _prep_params(w_in, q_gain_a, k_gain_a, q_gain_b, k_gain_b, w_up_a, w_up_b):
    offs = np.cumsum((0, A_WIDTH, A_KV_WIDTH, A_KV_WIDTH, A_WIDTH, 3 * B_WIDTH, 3 * B_WIDTH, 3 * B_WIDTH,
                      B_WIDTH, D_MODEL, D_MODEL))
    seg = lambda s: w_in[:, int(offs[s]):int(offs[s + 1])]

    def perm_heads(w2d):
        return jnp.concatenate([w2d[:, h * HEAD_DIM:(h + 1) * HEAD_DIM] for h in A_HEAD_ORDER], axis=1)

    w_qkv = jnp.concatenate([perm_heads(seg(0)), seg(4), seg(1), seg(5), seg(2), seg(6)],
                            axis=1).astype(jnp.bfloat16)
    w_gate = jnp.concatenate([perm_heads(seg(3)), seg(7), seg(8), seg(9)], axis=1).astype(jnp.bfloat16)
    gq = jnp.broadcast_to(jnp.concatenate([q_gain_a[None], q_gain_b], axis=0)[:, None, :] * Q_SCALE,
                          (1 + N_B_GROUPS, A_Q_HEADS, HEAD_DIM)).reshape(1, Q_COLS)
    gk = jnp.concatenate([jnp.broadcast_to(k_gain_a, (A_KV_HEADS, HEAD_DIM)).reshape(A_KV_WIDTH),
                          jnp.broadcast_to(k_gain_b[:, None, :], (N_B_GROUPS, B_HEADS, HEAD_DIM)).reshape(-1)])[None]
    hd = np.arange(MXU_COLS) // HEAD_DIM
    ones = jnp.asarray((hd[:, None] == hd[None, :]).astype(np.float32) / HEAD_DIM, jnp.bfloat16)
    wa = jnp.concatenate([w_up_a[h * HEAD_DIM:(h + 1) * HEAD_DIM] for h in A_HEAD_ORDER], axis=0)
    return w_qkv, w_gate, gq, gk, ones, wa.astype(jnp.bfloat16), w_up_b.astype(jnp.bfloat16)


def kernel(x_prompt, x_sample, cache_a_kv, cache_b1_kv, cache_b2_kv, cache_b3_kv, rel_bias, norm_gain, w_in,
           q_gain_a, k_gain_a, sinks_a, q_gain_b, k_gain_b, w_up_a, w_up_b, w_out):
    assert norm_gain.shape[0] == 1, "single layer"
    nb, seq, _ = x_prompt.shape
    n_dec, dec_seq, _ = x_sample.shape
    w_qkv, w_gate, gq, gk, ones, wa, wb = _prep_params(
        w_in[0], q_gain_a[0], k_gain_a[0], q_gain_b[0], k_gain_b[0], w_up_a[0], w_up_b[0])
    wo = w_out[0].astype(jnp.bfloat16)
    ng = norm_gain
    windows = (A_WINDOW,) + tuple(w for w, _ in B_GROUPS)
    dils = (1,) + tuple(d for _, d in B_GROUPS)
    head_cols = (A_HEAD_ORDER,) + tuple(tuple(range(A_Q_HEADS + g * B_HEADS, A_Q_HEADS + (g + 1) * B_HEADS))
                                        for g in range(N_B_GROUPS))

    p_rows = tuple(min(w, seq) for w in windows)
    qs, ks, vs, states = _qkv_proj(x_prompt.reshape(nb * seq, D_MODEL), ng, w_qkv, gq, gk, ones,
                                   nb=nb, seq=seq, state_rows=p_rows, dils=dils, tm=512, states_t=True)
    t_dec = n_dec * dec_seq
    qd, _, _, news = _qkv_proj(x_sample.reshape(t_dec, D_MODEL), ng, w_qkv, gq, gk, ones,
                               nb=1, seq=t_dec, state_rows=(t_dec,) * 4, dils=(1, 1, 1, 1), tm=t_dec,
                               states_t=False)

    q_dec = jnp.concatenate([a.reshape(n_dec, dec_seq, A_WIDTH) for a in qd], axis=2)
    caches = [jnp.transpose(c[0], (0, 2, 3, 4, 1)) for c in (cache_a_kv, cache_b1_kv, cache_b2_kv, cache_b3_kv)]
    news3 = [a.reshape(n_dec, dec_seq, a.shape[1]) for a in news]
    oa_s, ob_s, o_prompt, lses = _decode_band_attention(
        q_dec, news3, caches, qs, ks, vs, rel_bias, sinks_a, dils=dils, head_cols=head_cols, sink_idx=A_HEAD_ORDER)

    y_prompt = _out_proj(x_prompt, ng, w_gate, o_prompt[0], list(o_prompt[1:]), list(lses), wa, wb, wo, tm=512)
    heads = (A_KV_HEADS, B_HEADS, B_HEADS, B_HEADS)
    new_prompt = tuple(jnp.transpose(states[g].reshape(1, nb, 2, heads[g], HEAD_DIM, p_rows[g]), (0, 1, 5, 2, 3, 4))
                       for g in range(4))
    to_slabs = lambda o: jnp.transpose(o, (1, 0, 2, 3)).reshape(1, N_PAIRS, t_dec, LANES)
    y_sample = _out_proj(x_sample.reshape(1, t_dec, D_MODEL), ng, w_gate, to_slabs(oa_s), [to_slabs(ob_s)], [],
                         wa, wb, wo, tm=t_dec)
    y_sample = y_sample.reshape(n_dec, dec_seq, D_MODEL)
    new_sample = tuple(news3[g].reshape(1, n_dec, dec_seq, 2, heads[g], HEAD_DIM) for g in range(4))
    return (y_prompt, y_sample) + new_prompt + new_sample
```

```python
import functools
import math

import numpy as np
import jax
import jax.numpy as jnp
from jax import lax
from jax.experimental import pallas as pl
from jax.experimental.pallas import tpu as pltpu

D_MODEL = 1024
HEAD_DIM = 64
A_Q_HEADS = 8
A_KV_HEADS = 2
A_WINDOW = 128
B_GROUPS = ((128, 1), (512, 4), (2048, 16))
N_B_GROUPS = 3
B_HEADS = 8
N_KEYS = 128
A_WIDTH = A_Q_HEADS * HEAD_DIM
A_KV_WIDTH = A_KV_HEADS * HEAD_DIM
B_WIDTH = B_HEADS * HEAD_DIM
N_BUCKETS = 32
MAX_DISTANCE = 2048
EPS = 1e-6
NEG_INF = -1e30
Q_SCALE = HEAD_DIM ** -0.5

LANES = 128
MXU_COLS = 256
VMEM_LIMIT = 56 * 1024 * 1024
N_PAIRS = A_WIDTH // LANES

A_HEAD_ORDER = (0, 4, 1, 5, 2, 6, 3, 7)

Q_COLS = A_WIDTH + N_B_GROUPS * B_WIDTH
KV_COLS = A_KV_WIDTH + N_B_GROUPS * B_WIDTH
QKV_COLS = Q_COLS + 2 * KV_COLS
G_COLS = A_WIDTH + B_WIDTH + 2 * D_MODEL
KV_WIDTHS = (A_KV_WIDTH, B_WIDTH, B_WIDTH, B_WIDTH)
NORM_BATCH = 4 * B_WIDTH // MXU_COLS


def _t5_bucket_np(dist):
    max_exact = N_BUCKETS // 2
    d = np.maximum(dist, 0)
    df = np.maximum(d, 1).astype(np.float32)
    large = max_exact + (np.log(df / np.float32(max_exact)) / np.float32(math.log(MAX_DISTANCE / max_exact))
                         * np.float32(N_BUCKETS - max_exact)).astype(np.int32)
    large = np.minimum(large, N_BUCKETS - 1)
    return np.where(d < max_exact, d, large)


def _rmsnorm_bf16(x, gain):
    ms = jnp.mean(x * x, axis=-1, keepdims=True)
    return (x * lax.rsqrt(ms + EPS) * gain).astype(jnp.bfloat16)


def _state_plan(n_rows, seq, tm):
    tpb = seq // tm
    r = min(n_rows, tm)
    nblk = max(n_rows // tm, 1)
    return tpb, r, nblk, tpb - nblk


def _qkv_kernel(x_ref, ng_ref, w_ref, gq_ref, gk_ref, ones_ref, *rest, tm, seq, state_rows, dils, states_t):
    q_refs = rest[0:4]
    k_refs = rest[4:8]
    v_refs = rest[8:12]
    state_refs = rest[12:16]
    tmp_ref, a_scr, sq_scr, ss_scr = rest[16:20]
    h = _rmsnorm_bf16(x_ref[...], ng_ref[...])
    ones = ones_ref[...]

    def proj(col0, w):
        return jnp.dot(h, w_ref[:, col0:col0 + w], preferred_element_type=jnp.float32)

    def emit(a, c, out_ref, d, state):
        w = a.shape[1]
        if d == 1:
            out_ref[0, 0, :, c:c + w] = a.astype(out_ref.dtype)
        else:
            for s in range(w // LANES):
                tmp_ref[c // LANES + s] = a[:, s * LANES:(s + 1) * LANES]
        if state is not None:
            sref, kv, r = state
            if states_t:
                sref[0, kv, c:c + w, :] = a[tm - r:, :].T
            else:
                wd = sref.shape[1] // 2
                sref[:, kv * wd + c:kv * wd + c + w] = a[tm - r:, :]

    def finish(out_ref, d, width):
        if d > 1:
            for cls in range(d):
                for s in range(width // LANES):
                    out_ref[0, cls, :, s * LANES:(s + 1) * LANES] = (
                        tmp_ref[s, pl.ds(cls, tm // d, stride=d), :].astype(out_ref.dtype))

    def normed_batch(arrays):
        chunks = [(ai, c) for ai in range(len(arrays)) for c in range(0, B_WIDTH, MXU_COLS)]
        for ci, (ai, c) in enumerate(chunks):
            a = proj(arrays[ai][0] + c, MXU_COLS)
            a_scr[ci] = a
            sq_scr[ci * tm:(ci + 1) * tm, :] = (a * a).astype(jnp.bfloat16)
        n = len(chunks) * tm
        ss_scr[0:n, :] = jnp.dot(sq_scr[0:n, :], ones, preferred_element_type=jnp.float32)
        for ci, (ai, c) in enumerate(chunks):
            _, gain_ref, gcol0, out_ref, d, state = arrays[ai]
            o = (a_scr[ci] * lax.rsqrt(ss_scr[ci * tm:(ci + 1) * tm, :] + EPS)
                 * gain_ref[:, gcol0 + c:gcol0 + c + MXU_COLS])
            emit(o, c, out_ref, d, state)
            if c + MXU_COLS == B_WIDTH:
                finish(out_ref, d, B_WIDTH)

    q_off = (0, A_WIDTH, A_WIDTH + B_WIDTH, A_WIDTH + 2 * B_WIDTH)
    kv_off = (0, A_KV_WIDTH, A_KV_WIDTH + B_WIDTH, A_KV_WIDTH + 2 * B_WIDTH)
    plan = [_state_plan(state_rows[g], seq, tm)[1] for g in range(4)]
    normed_batch([(q_off[g], gq_ref, q_off[g], q_refs[g], dils[g], None) for g in range(4)])
    normed_batch([(Q_COLS + kv_off[g], gk_ref, kv_off[g], k_refs[g], dils[g], (state_refs[g], 0, plan[g]))
                  for g in range(1, 4)])
    a = proj(Q_COLS, A_KV_WIDTH)
    ss = jnp.dot((a * a).astype(jnp.bfloat16), ones[:A_KV_WIDTH, :A_KV_WIDTH], preferred_element_type=jnp.float32)
    emit(a * lax.rsqrt(ss + EPS) * gk_ref[:, 0:A_KV_WIDTH], 0, k_refs[0], dils[0], (state_refs[0], 0, plan[0]))
    for g in range(4):
        wd = KV_WIDTHS[g]
        for c in range(0, wd, MXU_COLS):
            w = min(MXU_COLS, wd - c)
            emit(proj(Q_COLS + KV_COLS + kv_off[g] + c, w), c, v_refs[g], dils[g], (state_refs[g], 1, plan[g]))
        finish(v_refs[g], dils[g], wd)


def _qkv_proj(x2d, ng, w_qkv, gq, gk, ones, *, nb, seq, state_rows, dils, tm, states_t):
    t = x2d.shape[0]
    tpb = seq // tm

    def cls_shape(d, width):
        return jax.ShapeDtypeStruct((nb, d, seq // d, width), jnp.bfloat16)

    def cls_spec(d, width):
        return pl.BlockSpec((1, d, tm // d, width), lambda i: (i // tpb, 0, i % tpb, 0))

    def state_shape(g):
        if states_t:
            return jax.ShapeDtypeStruct((nb, 2, KV_WIDTHS[g], state_rows[g]), jnp.float32)
        return jax.ShapeDtypeStruct((nb * state_rows[g], 2 * KV_WIDTHS[g]), jnp.float32)

    def state_spec(g):
        _, r, nblk, j0 = _state_plan(state_rows[g], seq, tm)
        if states_t:
            return pl.BlockSpec((1, 2, KV_WIDTHS[g], r), lambda i: (i // tpb, 0, 0, jnp.maximum(i % tpb - j0, 0)))
        return pl.BlockSpec((r, 2 * KV_WIDTHS[g]),
                            lambda i: ((i // tpb) * nblk + jnp.maximum(i % tpb - j0, 0), 0))

    const = lambda i: (0, 0)
    out_shape = ([cls_shape(d, A_WIDTH) for d in dils]
                 + [cls_shape(d, w) for d, w in zip(dils, KV_WIDTHS)] * 2
                 + [state_shape(g) for g in range(4)])
    out_specs = ([cls_spec(d, A_WIDTH) for d in dils]
                 + [cls_spec(d, w) for d, w in zip(dils, KV_WIDTHS)] * 2
                 + [state_spec(g) for g in range(4)])
    res = pl.pallas_call(
        functools.partial(_qkv_kernel, tm=tm, seq=seq, state_rows=state_rows, dils=dils, states_t=states_t),
        grid=(t // tm,),
        in_specs=[
            pl.BlockSpec((tm, D_MODEL), lambda i: (i, 0)),
            pl.BlockSpec((1, D_MODEL), const),
            pl.BlockSpec((D_MODEL, QKV_COLS), const, pipeline_mode=pl.Buffered(1)),
            pl.BlockSpec((1, Q_COLS), const),
            pl.BlockSpec((1, KV_COLS), const),
            pl.BlockSpec((MXU_COLS, MXU_COLS), const),
        ],
        out_specs=out_specs,
        out_shape=out_shape,
        scratch_shapes=[pltpu.VMEM((N_PAIRS, tm, LANES), jnp.float32),
                        pltpu.VMEM((NORM_BATCH, tm, MXU_COLS), jnp.float32),
                        pltpu.VMEM((NORM_BATCH * tm, MXU_COLS), jnp.bfloat16),
                        pltpu.VMEM((NORM_BATCH * tm, MXU_COLS), jnp.float32)],
        compiler_params=pltpu.CompilerParams(
            dimension_semantics=("arbitrary",), vmem_limit_bytes=VMEM_LIMIT),
        name="qkv_proj",
    )(x2d, ng, w_qkv, gq, gk, ones)
    return res[0:4], res[4:8], res[8:12], res[12:16]


def _lookup_rows(bucket_map, top, table_ref, col_top, col_bottom, fill):
    acc = jnp.full(bucket_map.shape, fill, jnp.float32)
    for b in range(N_BUCKETS):
        val = jnp.where(top, table_ref[b, col_top], table_ref[b, col_bottom])
        acc = jnp.where(bucket_map == b, val, acc)
    return acc


def _band_block(q_ref, k_ref, kp_ref, v_ref, vp_ref, o_ref, o_stage, lse_ref, bias_ref, sink_ref, *,
                cls, first, dil, shared_kv):
    blk = N_KEYS
    row = lax.broadcasted_iota(jnp.int32, (2 * blk, blk), 0)
    lane = lax.broadcasted_iota(jnp.int32, (2 * blk, blk), 1)
    ahead = lane - jnp.bitwise_and(row, blk - 1)
    tri = ahead <= 0
    valid = ahead <= jnp.where(first, 0, blk)
    cur_part = tri.astype(jnp.bfloat16)
    prev_part = jnp.logical_not(tri).astype(jnp.bfloat16)
    left = lax.broadcasted_iota(jnp.int32, (blk, LANES), 1) < HEAD_DIM
    mask_l = left.astype(jnp.bfloat16)
    mask_r = jnp.logical_not(left).astype(jnp.bfloat16)
    ones_kv = jnp.ones((2 * blk, LANES), jnp.bfloat16)
    nt = (((1,), (1,)), ((), ()))
    rows = pl.ds(0, blk) if dil == 1 else pl.ds(cls, blk, stride=dil)
    for p in range(N_PAIRS):
        kcol = 0 if shared_kv else p * LANES
        qp = q_ref[0, 0, :, p * LANES:(p + 1) * LANES]
        qs = jnp.concatenate([qp * mask_l, qp * mask_r], axis=0)
        kc = jnp.concatenate([kp_ref[cls, :, kcol:kcol + LANES], k_ref[0, 0, :, kcol:kcol + LANES]], axis=0)
        vc = jnp.concatenate([vp_ref[cls, :, kcol:kcol + LANES], v_ref[0, 0, :, kcol:kcol + LANES]], axis=0)
        s2 = lax.dot_general(qs, kc, nt, preferred_element_type=jnp.float32)
        s = jnp.where(tri, s2[:, blk:], s2[:, :blk]) + bias_ref[p]
        s = jnp.where(valid, s, NEG_INF)
        m = jnp.max(s, axis=-1, keepdims=True)
        if sink_ref is not None:
            sk = sink_ref[p]
            m = jnp.maximum(m, sk)
        pb = jnp.exp(s - m).astype(jnp.bfloat16)
        p2 = jnp.concatenate([pb * prev_part, pb * cur_part], axis=1)
        ov = jnp.dot(p2, jnp.concatenate([vc, ones_kv], axis=1), preferred_element_type=jnp.float32)
        l = ov[:, LANES:]
        if sink_ref is not None:
            l = l + jnp.exp(sk - m)
        num = jnp.where(left, ov[:blk, :LANES], ov[blk:, :LANES])
        den = jnp.where(left, l[:blk], l[blk:])
        if dil == 1:
            o_ref[0, p, rows, :] = (num / den).astype(o_ref.dtype)
        else:
            o_stage[p, rows, :] = num / den
        if lse_ref is not None:
            m1 = jnp.where(left, m[:blk], m[blk:])
            lse_ref[0, p, rows, :] = m1 + jnp.log(den)
    kp_ref[cls] = k_ref[0, 0]
    vp_ref[cls] = v_ref[0, 0]


def _decode_kernel(q_ref, na_ref, nb1_ref, nb2_ref, nb3_ref, ca_ref, cb1_ref, cb2_ref, cb3_ref,
                   ma_ref, mb1_ref, mb2_ref, mb3_ref, mn_ref, relb_ref, sinks_ref, oa_ref, ob_ref,
                   ta_ref, tb1_ref, tb2_ref, tb3_ref, tn_ref, sink_ref, *, dec_seq, head_cols, sink_idx, dils, init):
    t = dec_seq
    if init:
        for g, (m_ref, t_ref) in enumerate(zip((ma_ref, mb1_ref, mb2_ref, mb3_ref),
                                               (ta_ref, tb1_ref, tb2_ref, tb3_ref))):
            top = lax.broadcasted_iota(jnp.int32, m_ref.shape, 0) < t
            top_n = lax.broadcasted_iota(jnp.int32, (2 * t, LANES), 0) < t
            for p in range(N_PAIRS):
                cols = (head_cols[g][2 * p], head_cols[g][2 * p + 1])
                rows = slice(p * 2 * t, (p + 1) * 2 * t)
                t_ref[rows, :] = _lookup_rows(m_ref[...], top, relb_ref, cols[0], cols[1], NEG_INF)
                tn_ref[g, rows, :] = _lookup_rows(mn_ref[g], top_n, relb_ref, cols[0], cols[1], NEG_INF)
                if g == 0:
                    sink_ref[rows, :] = jnp.where(top_n, sinks_ref[0, sink_idx[2 * p]],
                                                  sinks_ref[0, sink_idx[2 * p + 1]])
        return

    left8 = lax.broadcasted_iota(jnp.int32, (2 * t, LANES), 1) < HEAD_DIM
    top8 = lax.broadcasted_iota(jnp.int32, (2 * t, LANES), 0) < t
    own = (left8 == top8).astype(jnp.float32)
    left4 = lax.broadcasted_iota(jnp.int32, (t, LANES), 1) < HEAD_DIM
    nt = (((1,), (1,)), ((), ()))

    def q_rows(c0):
        qp = q_ref[0, :, c0:c0 + LANES].astype(jnp.float32)
        return jnp.concatenate([qp, qp], axis=0) * own

    def attend(qrs, kts, vts, knews, vnews, tbl, tbl_new, sink, dil):
        shared = len(kts) == 1
        rows8 = lambda a, u: jnp.broadcast_to(a[u:u + 1, :], (2 * t, LANES))
        qr = jnp.concatenate(qrs, axis=0)
        qb = qr.astype(jnp.bfloat16)
        if shared:
            s_c = jnp.dot(qb, kts[0], preferred_element_type=jnp.float32)
        else:
            s_c = jnp.concatenate([jnp.dot(qb[2 * t * p:2 * t * (p + 1)], kts[p], preferred_element_type=jnp.float32)
                                   for p in range(N_PAIRS)], axis=0)
        yield
        s_c = s_c + tbl
        kn = [jnp.concatenate([rows8(knews[0 if shared else p], u) for p in range(N_PAIRS)], axis=0)
              for u in range(t)]
        vn = [jnp.concatenate([rows8(vnews[0 if shared else p], u) for p in range(N_PAIRS)], axis=0)
              for u in range(t)]
        s_n = [jnp.sum(qr * kn[u], axis=-1, keepdims=True) + tbl_new[:, u:u + 1] for u in range(t)]
        m = functools.reduce(jnp.maximum, s_n, jnp.max(s_c, axis=-1, keepdims=True))
        if sink is not None:
            m = jnp.maximum(m, sink)
        pc = jnp.exp(s_c - m)
        pn = [jnp.exp(x - m) for x in s_n]
        l = functools.reduce(lambda a, b: a + b, pn, jnp.sum(pc, axis=-1, keepdims=True))
        if sink is not None:
            l = l + jnp.exp(sink - m)
        pb = pc.astype(jnp.bfloat16)
        yield
        if shared:
            o = lax.dot_general(pb, vts[0], nt, preferred_element_type=jnp.float32)
        elif dil >= t:
            lane8 = lax.broadcasted_iota(jnp.int32, (2 * t, LANES), 1)
            row8 = lax.broadcasted_iota(jnp.int32, (2 * t, LANES), 0)
            sel = (lane8 % dil == row8 % t).astype(jnp.bfloat16)
            parts = []
            for p in range(N_PAIRS):
                accs = []
                for side in range(2):
                    r0 = 2 * t * p + t * side
                    w = jnp.sum(pc[r0:r0 + t], axis=0, keepdims=True)
                    v = vts[p][side * HEAD_DIM:(side + 1) * HEAD_DIM]
                    accs.append(functools.reduce(
                        lambda a, b: a + b,
                        [v[:, c:c + LANES] * w[:, c:c + LANES] for c in range(0, v.shape[1], LANES)]))
                acc = jnp.concatenate(accs, axis=0)
                hi = acc.astype(jnp.bfloat16)
                lo = (acc - hi.astype(jnp.float32)).astype(jnp.bfloat16)
                parts.append(lax.dot_general(sel, hi, nt, preferred_element_type=jnp.float32)
                             + lax.dot_general(sel, lo, nt, preferred_element_type=jnp.float32))
            o = jnp.concatenate(parts, axis=0)
        else:
            o = jnp.concatenate([lax.dot_general(pb[2 * t * p:2 * t * (p + 1)], vts[p], nt,
                                                 preferred_element_type=jnp.float32)
                                 for p in range(N_PAIRS)], axis=0)
        for u in range(t):
            o = o + pn[u] * vn[u]
        yield o / l, m + jnp.log(l)

    def fold(x, p):
        return jnp.where(left4, x[2 * t * p:2 * t * p + t], x[2 * t * p + t:2 * t * (p + 1)])

    def cache_kv(c_ref, kv, p, dtype=jnp.bfloat16):
        return c_ref[0, kv, 2 * p:2 * p + 2].reshape(LANES, c_ref.shape[-1]).astype(dtype)

    caches = (cb1_ref, cb2_ref, cb3_ref)
    news = (nb1_ref, nb2_ref, nb3_ref)
    tbls = (tb1_ref, tb2_ref, tb3_ref)
    groups = [attend([q_rows(p * LANES) for p in range(N_PAIRS)],
                     [cache_kv(ca_ref, 0, 0)], [cache_kv(ca_ref, 1, 0)],
                     [na_ref[0, :, 0:LANES]], [na_ref[0, :, LANES:2 * LANES]],
                     ta_ref[...], tn_ref[0], sink_ref[:, 0:1], dils[0])]
    for g in range(N_B_GROUPS):
        groups.append(attend(
            [q_rows(A_WIDTH + g * B_WIDTH + p * LANES) for p in range(N_PAIRS)],
            [cache_kv(caches[g], 0, p) for p in range(N_PAIRS)],
            [cache_kv(caches[g], 1, p, jnp.float32 if dils[1 + g] >= t else jnp.bfloat16) for p in range(N_PAIRS)],
            [news[g][0, :, p * LANES:(p + 1) * LANES] for p in range(N_PAIRS)],
            [news[g][0, :, B_WIDTH + p * LANES:B_WIDTH + (p + 1) * LANES] for p in range(N_PAIRS)],
            tbls[g][...], tn_ref[1 + g], None, dils[1 + g]))
    for phase in range(2):
        for gen in groups:
            next(gen)
    results = [next(gen) for gen in groups]
    oa = results[0][0]
    for p in range(N_PAIRS):
        oa_ref[0, p] = fold(oa, p).astype(oa_ref.dtype)
    outs = [o for o, _ in results[1:]]
    lses = [jnp.broadcast_to(lse, o.shape) for o, lse in results[1:]]
    mx = jnp.maximum(jnp.maximum(lses[0], lses[1]), lses[2])
    es = [jnp.exp(x - mx) for x in lses]
    den = es[0] + es[1] + es[2]
    comb = (es[0] * outs[0] + es[1] * outs[1] + es[2] * outs[2]) / den
    for p in range(N_PAIRS):
        ob_ref[0, p] = fold(comb, p).astype(ob_ref.dtype)


def _decode_bucket_maps(dils, cache_lens, dec_seq):
    t_idx = np.arange(dec_seq)
    maps = []
    new = np.full((len(dils), 2 * dec_seq, LANES), -1, np.int32)
    for g, (d, ln) in enumerate(zip(dils, cache_lens)):
        back = ln + t_idx[:, None] - np.arange(ln)[None, :]
        ok = (back % d == 0) & (back // d >= 1) & (back // d < N_KEYS)
        maps.append(jnp.asarray(np.tile(np.where(ok, _t5_bucket_np(back), -1), (2, 1)).astype(np.int32)))
        backn = t_idx[:, None] - t_idx[None, :]
        okn = (backn >= 0) & (backn % d == 0) & (backn // d < N_KEYS)
        new[g, :, :dec_seq] = np.tile(np.where(okn, _t5_bucket_np(backn), -1), (2, 1))
    return maps, jnp.asarray(new)


def _decode_band_kernel(*refs, dec_seq, head_cols, sink_idx, dils, blocks_per_batch):
    n_g = len(dils)
    dec_in = refs[0:16]
    band_in = [refs[16 + 4 * g:16 + 4 * (g + 1)] for g in range(n_g)]
    outs = refs[16 + 4 * n_g:16 + 4 * n_g + 2 + 2 * n_g - 1]
    oa_ref, ob_ref = outs[0:2]
    o_refs = outs[2:2 + n_g]
    lse_refs = (None,) + tuple(outs[2 + n_g:])
    scratch = refs[16 + 4 * n_g + 2 + 2 * n_g - 1:]
    dec_scratch = scratch[0:6]
    bias_refs = scratch[6:6 + n_g]
    sink_tile = scratch[6 + n_g]
    kprev_refs = scratch[7 + n_g:7 + 2 * n_g]
    vprev_refs = scratch[7 + 2 * n_g:7 + 3 * n_g]
    stages = iter(scratch[7 + 3 * n_g:])
    stage_refs = [None if d == 1 else next(stages) for d in dils]
    blk = N_KEYS
    s = pl.program_id(0)
    decode = functools.partial(_decode_kernel, *dec_in, oa_ref, ob_ref, *dec_scratch, dec_seq=dec_seq,
                               head_cols=head_cols, sink_idx=sink_idx, dils=dils)

    @pl.when(s == 0)
    def _():
        top = lax.broadcasted_iota(jnp.int32, (2 * blk, blk), 0) < blk
        for g in range(n_g):
            bmap = band_in[g][3][...]
            for p in range(N_PAIRS):
                bias_refs[g][p] = _lookup_rows(bmap, top, dec_in[14], head_cols[g][2 * p], head_cols[g][2 * p + 1],
                                               0.0)
            kprev_refs[g][...] = jnp.zeros(kprev_refs[g].shape, kprev_refs[g].dtype)
            vprev_refs[g][...] = jnp.zeros(vprev_refs[g].shape, vprev_refs[g].dtype)
        for p in range(N_PAIRS):
            sink_tile[p] = jnp.where(top, dec_in[15][0, sink_idx[2 * p]], dec_in[15][0, sink_idx[2 * p + 1]])
        decode(init=True)

    r = s % blocks_per_batch

    for g in range(n_g):
        q_ref, k_ref, v_ref, _ = band_in[g]
        _band_block(q_ref, k_ref, kprev_refs[g], v_ref, vprev_refs[g], o_refs[g], stage_refs[g], lse_refs[g],
                    bias_refs[g], sink_tile if g == 0 else None, cls=r % dils[g], first=(r // dils[g]) == 0,
                    dil=dils[g], shared_kv=(g == 0))
    decode(init=False)

    for g in range(n_g):
        if dils[g] > 1:
            @pl.when(r % dils[g] == dils[g] - 1)
            def _():
                for p in range(N_PAIRS):
                    o_refs[g][0, p] = stage_refs[g][p].astype(o_refs[g].dtype)


def _decode_band_attention(q_dec, news, caches, qs, ks, vs, rel_bias, sinks, *, dils, head_cols, sink_idx):
    n, t, _ = q_dec.shape
    nb = qs[0].shape[0]
    seq = qs[0].shape[1] * qs[0].shape[2]
    ppb = seq // N_KEYS
    assert n == nb * ppb, "one prompt block of every group per sample sequence"
    seq3 = lambda i: (i, 0, 0)
    seq4 = lambda i: (i, 0, 0, 0)
    seq5 = lambda i: (i, 0, 0, 0, 0)
    cache_lens = tuple(c.shape[-1] for c in caches)
    maps, map_new = _decode_bucket_maps(dils, cache_lens, t)
    smem = functools.partial(pl.BlockSpec, memory_space=pltpu.SMEM)
    in_specs = [pl.BlockSpec((1, t, q_dec.shape[2]), seq3)]
    in_specs += [pl.BlockSpec((1, t, a.shape[2]), seq3) for a in news]
    in_specs += [pl.BlockSpec((1,) + c.shape[1:], seq5) for c in caches]
    in_specs += [pl.BlockSpec(m.shape, lambda i: (0, 0)) for m in maps]
    in_specs += [pl.BlockSpec(map_new.shape, lambda i: (0, 0, 0)), smem(), smem()]
    band_args = []
    out_specs = [pl.BlockSpec((1, N_PAIRS, t, LANES), seq4)] * 2
    out_shape = [jax.ShapeDtypeStruct((n, N_PAIRS, t, LANES), jnp.float32)] * 2
    qi = np.arange(N_KEYS)[:, None]
    ci = np.arange(N_KEYS)[None, :]
    for g, d in enumerate(dils):
        cur = lambda s, d=d: (s // ppb, (s % ppb) % d, (s % ppb) // d, 0)
        wq, wkv = qs[g].shape[3], ks[g].shape[3]
        bmap = jnp.asarray(np.tile(_t5_bucket_np(((qi - ci) % N_KEYS) * d), (2, 1)).astype(np.int32))
        in_specs += [pl.BlockSpec((1, 1, N_KEYS, wq), cur), pl.BlockSpec((1, 1, N_KEYS, wkv), cur),
                     pl.BlockSpec((1, 1, N_KEYS, wkv), cur), pl.BlockSpec(bmap.shape, lambda s: (0, 0))]
        band_args += [qs[g], ks[g], vs[g], bmap]
    tile = lambda d: pl.BlockSpec((1, N_PAIRS, N_KEYS * d, LANES), lambda s, d=d: (s // ppb, 0, (s % ppb) // d, 0))
    slab_shape = lambda dtype: jax.ShapeDtypeStruct((nb, N_PAIRS, seq, LANES), dtype)
    out_specs += [tile(d) for d in dils] + [tile(d) for d in dils[1:]]
    out_shape += [slab_shape(jnp.bfloat16)] * len(dils) + [slab_shape(jnp.float32)] * (len(dils) - 1)
    n_rows = 2 * t * N_PAIRS
    tile_scr = pltpu.VMEM((N_PAIRS, 2 * N_KEYS, N_KEYS), jnp.float32)
    res = pl.pallas_call(
        functools.partial(_decode_band_kernel, dec_seq=t, head_cols=head_cols, sink_idx=sink_idx, dils=dils,
                          blocks_per_batch=ppb),
        grid=(n,),
        in_specs=in_specs,
        out_specs=out_specs,
        out_shape=out_shape,
        scratch_shapes=[pltpu.VMEM((n_rows, ln), jnp.float32) for ln in cache_lens]
        + [pltpu.VMEM((len(dils), n_rows, LANES), jnp.float32), pltpu.VMEM((n_rows, LANES), jnp.float32)]
        + [tile_scr] * (len(dils) + 1)
        + [pltpu.VMEM((d, N_KEYS, k.shape[3]), jnp.bfloat16) for d, k in zip(dils, ks)] * 2
        + [pltpu.VMEM((N_PAIRS, N_KEYS * d, LANES), jnp.float32) for d in dils if d > 1],
        compiler_params=pltpu.CompilerParams(
            dimension_semantics=("arbitrary",), vmem_limit_bytes=VMEM_LIMIT),
        name="decode_band_attn",
    )(q_dec, *news, *caches, *maps, map_new, rel_bias, sinks, *band_args)
    ng = len(dils)
    return res[0], res[1], res[2:2 + ng], res[2 + ng:]


def _out_kernel(*refs, n_groups):
    x_ref, ng_ref, wg_ref, oa_ref = refs[0:4]
    ob_refs = refs[4:4 + n_groups]
    lse_refs = refs[4 + n_groups:4 + 2 * n_groups] if n_groups > 1 else ()
    wa_ref, wb_ref, wo_ref, y_ref = refs[-4:]
    x = x_ref[0]
    h = _rmsnorm_bf16(x, ng_ref[...])

    def gate(c0, width, silu):
        a = jnp.dot(h, wg_ref[:, c0:c0 + width], preferred_element_type=jnp.float32)
        sg = 1.0 / (1.0 + jnp.exp(-a))
        return a * sg if silu else sg

    def slabs(ref):
        return jnp.concatenate([ref[0, p].astype(jnp.float32) for p in range(N_PAIRS)], axis=1)

    if n_groups > 1:
        parts = []
        for p in range(N_PAIRS):
            lses = [r[0, p] for r in lse_refs]
            mx = functools.reduce(jnp.maximum, lses)
            es = [jnp.exp(v - mx) for v in lses]
            den = functools.reduce(lambda a, b: a + b, es)
            num = functools.reduce(lambda a, b: a + b,
                                   [e * r[0, p].astype(jnp.float32) for e, r in zip(es, ob_refs)])
            parts.append(num / den)
        ob = jnp.concatenate(parts, axis=1)
    else:
        ob = slabs(ob_refs[0])
    oa = slabs(oa_ref)
    ya = jnp.dot((oa * gate(0, A_WIDTH, True)).astype(jnp.bfloat16), wa_ref[...],
                 preferred_element_type=jnp.float32)
    yb = jnp.dot((ob * gate(A_WIDTH, B_WIDTH, True)).astype(jnp.bfloat16), wb_ref[...],
                 preferred_element_type=jnp.float32)
    m0 = A_WIDTH + B_WIDTH
    merged = (gate(m0, D_MODEL, False) * ya + gate(m0 + D_MODEL, D_MODEL, False) * yb).astype(jnp.bfloat16)
    y_ref[0] = x + jnp.dot(merged, wo_ref[...], preferred_element_type=jnp.float32)


def _out_proj(x3d, ng, wg, oa, obs, lses, wa, wb, wo, *, tm):
    nb, seq, _ = x3d.shape
    row = lambda b, i: (b, i, 0)
    slab = pl.BlockSpec((1, N_PAIRS, tm, LANES), lambda b, i: (b, 0, i, 0))
    const = lambda b, i: (0, 0)
    once = dict(pipeline_mode=pl.Buffered(1))
    in_specs = [pl.BlockSpec((1, tm, D_MODEL), row), pl.BlockSpec((1, D_MODEL), const),
                pl.BlockSpec(wg.shape, const, **once), slab]
    in_specs += [slab for _ in obs] + [slab for _ in lses]
    in_specs += [pl.BlockSpec(wa.shape, const, **once), pl.BlockSpec(wb.shape, const, **once),
                 pl.BlockSpec(wo.shape, const, **once)]
    return pl.pallas_call(
        functools.partial(_out_kernel, n_groups=len(obs)),
        grid=(nb, seq // tm),
        in_specs=in_specs,
        out_specs=pl.BlockSpec((1, tm, D_MODEL), row),
        out_shape=jax.ShapeDtypeStruct((nb, seq, D_MODEL), jnp.float32),
        compiler_params=pltpu.CompilerParams(
            dimension_semantics=("arbitrary", "arbitrary"), vmem_limit_bytes=VMEM_LIMIT),
        name="out_proj",
    )(x3d, ng, wg, oa, *obs, *lses, wa, wb, wo)


def _prep_params(w_in, q_gain_a, k_gain_a, q_gain_b, k_gain_b, w_up_a, w_up_b):
    offs = np.cumsum((0, A_WIDTH, A_KV_WIDTH, A_KV_WIDTH, A_WIDTH, 3 * B_WIDTH, 3 * B_WIDTH, 3 * B_WIDTH,
                      B_WIDTH, D_MODEL, D_MODEL))
    seg = lambda s: w_in[:, int(offs[s]):int(offs[s + 1])]

    def perm_heads(w2d):
        return jnp.concatenate([w2d[:, h * HEAD_DIM:(h + 1) * HEAD_DIM] for h in A_HEAD_ORDER], axis=1)

    w_qkv = jnp.concatenate([perm_heads(seg(0)), seg(4), seg(1), seg(5), seg(2), seg(6)],
                            axis=1).astype(jnp.bfloat16)
    w_gate = jnp.concatenate([perm_heads(seg(3)), seg(7), seg(8), seg(9)], axis=1).astype(jnp.bfloat16)
    gq = jnp.broadcast_to(jnp.concatenate([q_gain_a[None], q_gain_b], axis=0)[:, None, :] * Q_SCALE,
                          (1 + N_B_GROUPS, A_Q_HEADS, HEAD_DIM)).reshape(1, Q_COLS)
    gk = jnp.concatenate([jnp.broadcast_to(k_gain_a, (A_KV_HEADS, HEAD_DIM)).reshape(A_KV_WIDTH),
                          jnp.broadcast_to(k_gain_b[:, None, :], (N_B_GROUPS, B_HEADS, HEAD_DIM)).reshape(-1)])[None]
    hd = np.arange(MXU_COLS) // HEAD_DIM
    ones = jnp.asarray((hd[:, None] == hd[None, :]).astype(np.float32) / HEAD_DIM, jnp.bfloat16)
    wa = jnp.concatenate([w_up_a[h * HEAD_DIM:(h + 1) * HEAD_DIM] for h in A_HEAD_ORDER], axis=0)
    return w_qkv, w_gate, gq, gk, ones, wa.astype(jnp.bfloat16), w_up_b.astype(jnp.bfloat16)


def kernel(x_prompt, x_sample, cache_a_kv, cache_b1_kv, cache_b2_kv, cache_b3_kv, rel_bias, norm_gain, w_in,
           q_gain_a, k_gain_a, sinks_a, q_gain_b, k_gain_b, w_up_a, w_up_b, w_out):
    assert norm_gain.shape[0] == 1, "single layer"
    nb, seq, _ = x_prompt.shape
    n_dec, dec_seq, _ = x_sample.shape
    w_qkv, w_gate, gq, gk, ones, wa, wb = _prep_params(
        w_in[0], q_gain_a[0], k_gain_a[0], q_gain_b[0], k_gain_b[0], w_up_a[0], w_up_b[0])
    wo = w_out[0].astype(jnp.bfloat16)
    ng = norm_gain
    windows = (A_WINDOW,) + tuple(w for w, _ in B_GROUPS)
    dils = (1,) + tuple(d for _, d in B_GROUPS)
    head_cols = (A_HEAD_ORDER,) + tuple(tuple(range(A_Q_HEADS + g * B_HEADS, A_Q_HEADS + (g + 1) * B_HEADS))
                                        for g in range(N_B_GROUPS))

    p_rows = tuple(min(w, seq) for w in windows)
    qs, ks, vs, states = _qkv_proj(x_prompt.reshape(nb * seq, D_MODEL), ng, w_qkv, gq, gk, ones,
                                   nb=nb, seq=seq, state_rows=p_rows, dils=dils, tm=512, states_t=True)
    t_dec = n_dec * dec_seq
    qd, _, _, news = _qkv_proj(x_sample.reshape(t_dec, D_MODEL), ng, w_qkv, gq, gk, ones,
                               nb=1, seq=t_dec, state_rows=(t_dec,) * 4, dils=(1, 1, 1, 1), tm=t_dec,
                               states_t=False)

    q_dec = jnp.concatenate([a.reshape(n_dec, dec_seq, A_WIDTH) for a in qd], axis=2)
    caches = [jnp.transpose(c[0], (0, 2, 3, 4, 1)) for c in (cache_a_kv, cache_b1_kv, cache_b2_kv, cache_b3_kv)]
    news3 = [a.reshape(n_dec, dec_seq, a.shape[1]) for a in news]
    oa_s, ob_s, o_prompt, lses = _decode_band_attention(
        q_dec, news3, caches, qs, ks, vs, rel_bias, sinks_a, dils=dils, head_cols=head_cols, sink_idx=A_HEAD_ORDER)

    y_prompt = _out_proj(x_prompt, ng, w_gate, o_prompt[0], list(o_prompt[1:]), list(lses), wa, wb, wo, tm=512)
    heads = (A_KV_HEADS, B_HEADS, B_HEADS, B_HEADS)
    new_prompt = tuple(jnp.transpose(states[g].reshape(1, nb, 2, heads[g], HEAD_DIM, p_rows[g]), (0, 1, 5, 2, 3, 4))
                       for g in range(4))
    to_slabs = lambda o: jnp.transpose(o, (1, 0, 2, 3)).reshape(1, N_PAIRS, t_dec, LANES)
    y_sample = _out_proj(x_sample.reshape(1, t_dec, D_MODEL), ng, w_gate, to_slabs(oa_s), [to_slabs(ob_s)], [],
                         wa, wb, wo, tm=t_dec)
    y_sample = y_sample.reshape(n_dec, dec_seq, D_MODEL)
    new_sample = tuple(news3[g].reshape(1, n_dec, dec_seq, 2, heads[g], HEAD_DIM) for g in range(4))
    return (y_prompt, y_sample) + new_prompt + new_sample
```

```python
import functools
import math

import numpy as np
import jax
import jax.numpy as jnp
from jax import lax
from jax.experimental import pallas as pl
from jax.experimental.pallas import tpu as pltpu

D_MODEL = 1024
HEAD_DIM = 64
A_Q_HEADS = 8
A_KV_HEADS = 2
A_WINDOW = 128
B_GROUPS = ((128, 1), (512, 4), (2048, 16))
N_B_GROUPS = 3
B_HEADS = 8
N_KEYS = 128
A_WIDTH = A_Q_HEADS * HEAD_DIM
A_KV_WIDTH = A_KV_HEADS * HEAD_DIM
B_WIDTH = B_HEADS * HEAD_DIM
N_BUCKETS = 32
MAX_DISTANCE = 2048
EPS = 1e-6
NEG_INF = -1e30
Q_SCALE = HEAD_DIM ** -0.5

LANES = 128
MXU_COLS = 256
VMEM_LIMIT = 56 * 1024 * 1024
PROJ_TILE_ROWS = 512
N_PAIRS = A_WIDTH // LANES

A_HEAD_ORDER = (0, 4, 1, 5, 2, 6, 3, 7)

Q_COLS = A_WIDTH + N_B_GROUPS * B_WIDTH
KV_COLS = A_KV_WIDTH + N_B_GROUPS * B_WIDTH
QKV_COLS = Q_COLS + 2 * KV_COLS
G_COLS = A_WIDTH + B_WIDTH + 2 * D_MODEL
KV_WIDTHS = (A_KV_WIDTH, B_WIDTH, B_WIDTH, B_WIDTH)
NORM_BATCH = 4 * B_WIDTH // MXU_COLS


def _t5_bucket_np(dist):
    max_exact = N_BUCKETS // 2
    d = np.maximum(dist, 0)
    df = np.maximum(d, 1).astype(np.float32)
    large = max_exact + (np.log(df / np.float32(max_exact)) / np.float32(math.log(MAX_DISTANCE / max_exact))
                         * np.float32(N_BUCKETS - max_exact)).astype(np.int32)
    large = np.minimum(large, N_BUCKETS - 1)
    return np.where(d < max_exact, d, large)


def _rmsnorm_bf16(x, gain):
    ms = jnp.mean(x * x, axis=-1, keepdims=True)
    return (x * lax.rsqrt(ms + EPS) * gain).astype(jnp.bfloat16)


def _state_plan(n_rows, seq, tm):
    tpb = seq // tm
    r = min(n_rows, tm)
    nblk = max(n_rows // tm, 1)
    return tpb, r, nblk, tpb - nblk


def _qkv_kernel(x_ref, ng_ref, w_ref, gq_ref, gk_ref, ones_ref, *rest, tm, seq, state_rows, dils, states_t):
    q_refs = rest[0:4]
    k_refs = rest[4:8]
    v_refs = rest[8:12]
    state_refs = rest[12:16]
    tmp_ref, a_scr, sq_scr, ss_scr = rest[16:20]
    h = _rmsnorm_bf16(x_ref[...], ng_ref[...])
    ones = ones_ref[...]

    def proj(col0, w):
        return jnp.dot(h, w_ref[:, col0:col0 + w], preferred_element_type=jnp.float32)

    def emit(a, c, out_ref, d, state):
        w = a.shape[1]
        if d == 1:
            out_ref[0, 0, :, c:c + w] = a.astype(out_ref.dtype)
        else:
            for s in range(w // LANES):
                tmp_ref[c // LANES + s] = a[:, s * LANES:(s + 1) * LANES]
        if state is not None:
            sref, kv, r = state
            if states_t:
                sref[0, kv, c:c + w, :] = a[tm - r:, :].T
            else:
                wd = sref.shape[1] // 2
                sref[:, kv * wd + c:kv * wd + c + w] = a[tm - r:, :]

    def finish(out_ref, d, width):
        if d > 1:
            for cls in range(d):
                for s in range(width // LANES):
                    out_ref[0, cls, :, s * LANES:(s + 1) * LANES] = (
                        tmp_ref[s, pl.ds(cls, tm // d, stride=d), :].astype(out_ref.dtype))

    def normed_batch(arrays):
        chunks = [(ai, c) for ai in range(len(arrays)) for c in range(0, B_WIDTH, MXU_COLS)]
        for ci, (ai, c) in enumerate(chunks):
            a = proj(arrays[ai][0] + c, MXU_COLS)
            a_scr[ci] = a
            sq_scr[ci * tm:(ci + 1) * tm, :] = (a * a).astype(jnp.bfloat16)
        n = len(chunks) * tm
        ss_scr[0:n, :] = jnp.dot(sq_scr[0:n, :], ones, preferred_element_type=jnp.float32)
        for ci, (ai, c) in enumerate(chunks):
            _, gain_ref, gcol0, out_ref, d, state = arrays[ai]
            o = (a_scr[ci] * lax.rsqrt(ss_scr[ci * tm:(ci + 1) * tm, :] + EPS)
                 * gain_ref[:, gcol0 + c:gcol0 + c + MXU_COLS])
            emit(o, c, out_ref, d, state)
            if c + MXU_COLS == B_WIDTH:
                finish(out_ref, d, B_WIDTH)

    q_off = (0, A_WIDTH, A_WIDTH + B_WIDTH, A_WIDTH + 2 * B_WIDTH)
    kv_off = (0, A_KV_WIDTH, A_KV_WIDTH + B_WIDTH, A_KV_WIDTH + 2 * B_WIDTH)
    plan = [_state_plan(state_rows[g], seq, tm)[1] for g in range(4)]
    normed_batch([(q_off[g], gq_ref, q_off[g], q_refs[g], dils[g], None) for g in range(4)])
    normed_batch([(Q_COLS + kv_off[g], gk_ref, kv_off[g], k_refs[g], dils[g], (state_refs[g], 0, plan[g]))
                  for g in range(1, 4)])
    a = proj(Q_COLS, A_KV_WIDTH)
    ss = jnp.dot((a * a).astype(jnp.bfloat16), ones[:A_KV_WIDTH, :A_KV_WIDTH], preferred_element_type=jnp.float32)
    emit(a * lax.rsqrt(ss + EPS) * gk_ref[:, 0:A_KV_WIDTH], 0, k_refs[0], dils[0], (state_refs[0], 0, plan[0]))
    for g in range(4):
        wd = KV_WIDTHS[g]
        for c in range(0, wd, MXU_COLS):
            w = min(MXU_COLS, wd - c)
            emit(proj(Q_COLS + KV_COLS + kv_off[g] + c, w), c, v_refs[g], dils[g], (state_refs[g], 1, plan[g]))
        finish(v_refs[g], dils[g], wd)


def _qkv_proj(x2d, ng, w_qkv, gq, gk, ones, *, nb, seq, state_rows, dils, tm, states_t):
    t = x2d.shape[0]
    tpb = seq // tm

    def cls_shape(d, width):
        return jax.ShapeDtypeStruct((nb, d, seq // d, width), jnp.bfloat16)

    def cls_spec(d, width):
        return pl.BlockSpec((1, d, tm // d, width), lambda i: (i // tpb, 0, i % tpb, 0))

    def state_shape(g):
        if states_t:
            return jax.ShapeDtypeStruct((nb, 2, KV_WIDTHS[g], state_rows[g]), jnp.float32)
        return jax.ShapeDtypeStruct((nb * state_rows[g], 2 * KV_WIDTHS[g]), jnp.float32)

    def state_spec(g):
        _, r, nblk, j0 = _state_plan(state_rows[g], seq, tm)
        if states_t:
            return pl.BlockSpec((1, 2, KV_WIDTHS[g], r), lambda i: (i // tpb, 0, 0, jnp.maximum(i % tpb - j0, 0)))
        return pl.BlockSpec((r, 2 * KV_WIDTHS[g]),
                            lambda i: ((i // tpb) * nblk + jnp.maximum(i % tpb - j0, 0), 0))

    const = lambda i: (0, 0)
    out_shape = ([cls_shape(d, A_WIDTH) for d in dils]
                 + [cls_shape(d, w) for d, w in zip(dils, KV_WIDTHS)] * 2
                 + [state_shape(g) for g in range(4)])
    out_specs = ([cls_spec(d, A_WIDTH) for d in dils]
                 + [cls_spec(d, w) for d, w in zip(dils, KV_WIDTHS)] * 2
                 + [state_spec(g) for g in range(4)])
    res = pl.pallas_call(
        functools.partial(_qkv_kernel, tm=tm, seq=seq, state_rows=state_rows, dils=dils, states_t=states_t),
        grid=(t // tm,),
        in_specs=[
            pl.BlockSpec((tm, D_MODEL), lambda i: (i, 0)),
            pl.BlockSpec((1, D_MODEL), const),
            pl.BlockSpec((D_MODEL, QKV_COLS), const, pipeline_mode=pl.Buffered(1)),
            pl.BlockSpec((1, Q_COLS), const),
            pl.BlockSpec((1, KV_COLS), const),
            pl.BlockSpec((MXU_COLS, MXU_COLS), const),
        ],
        out_specs=out_specs,
        out_shape=out_shape,
        scratch_shapes=[pltpu.VMEM((N_PAIRS, tm, LANES), jnp.float32),
                        pltpu.VMEM((NORM_BATCH, tm, MXU_COLS), jnp.float32),
                        pltpu.VMEM((NORM_BATCH * tm, MXU_COLS), jnp.bfloat16),
                        pltpu.VMEM((NORM_BATCH * tm, MXU_COLS), jnp.float32)],
        compiler_params=pltpu.CompilerParams(
            dimension_semantics=("arbitrary",), vmem_limit_bytes=VMEM_LIMIT),
        name="qkv_proj",
    )(x2d, ng, w_qkv, gq, gk, ones)
    return res[0:4], res[4:8], res[8:12], res[12:16]


def _lookup_rows(bucket_map, top, table_ref, col_top, col_bottom, fill):
    acc = jnp.full(bucket_map.shape, fill, jnp.float32)
    for b in range(N_BUCKETS):
        val = jnp.where(top, table_ref[b, col_top], table_ref[b, col_bottom])
        acc = jnp.where(bucket_map == b, val, acc)
    return acc


def _band_block(q_ref, k_ref, kp_ref, v_ref, vp_ref, o_ref, o_stage, lse_ref, bias_ref, sink_ref, *,
                cls, first, dil, shared_kv):
    blk = N_KEYS
    row = lax.broadcasted_iota(jnp.int32, (2 * blk, blk), 0)
    lane = lax.broadcasted_iota(jnp.int32, (2 * blk, blk), 1)
    ahead = lane - jnp.bitwise_and(row, blk - 1)
    tri = ahead <= 0
    valid = ahead <= jnp.where(first, 0, blk)
    cur_part = tri.astype(jnp.bfloat16)
    prev_part = jnp.logical_not(tri).astype(jnp.bfloat16)
    left = lax.broadcasted_iota(jnp.int32, (blk, LANES), 1) < HEAD_DIM
    mask_l = left.astype(jnp.bfloat16)
    mask_r = jnp.logical_not(left).astype(jnp.bfloat16)
    ones_kv = jnp.ones((2 * blk, LANES), jnp.bfloat16)
    nt = (((1,), (1,)), ((), ()))
    rows = pl.ds(0, blk) if dil == 1 else pl.ds(cls, blk, stride=dil)
    for p in range(N_PAIRS):
        kcol = 0 if shared_kv else p * LANES
        qp = q_ref[0, 0, :, p * LANES:(p + 1) * LANES]
        qs = jnp.concatenate([qp * mask_l, qp * mask_r], axis=0)
        kc = jnp.concatenate([kp_ref[cls, :, kcol:kcol + LANES], k_ref[0, 0, :, kcol:kcol + LANES]], axis=0)
        vc = jnp.concatenate([vp_ref[cls, :, kcol:kcol + LANES], v_ref[0, 0, :, kcol:kcol + LANES]], axis=0)
        s2 = lax.dot_general(qs, kc, nt, preferred_element_type=jnp.float32)
        s = jnp.where(tri, s2[:, blk:], s2[:, :blk]) + bias_ref[p]
        s = jnp.where(valid, s, NEG_INF)
        m = jnp.max(s, axis=-1, keepdims=True)
        if sink_ref is not None:
            sk = sink_ref[p]
            m = jnp.maximum(m, sk)
        pb = jnp.exp(s - m).astype(jnp.bfloat16)
        p2 = jnp.concatenate([pb * prev_part, pb * cur_part], axis=1)
        ov = jnp.dot(p2, jnp.concatenate([vc, ones_kv], axis=1), preferred_element_type=jnp.float32)
        l = ov[:, LANES:]
        if sink_ref is not None:
            l = l + jnp.exp(sk - m)
        num = jnp.where(left, ov[:blk, :LANES], ov[blk:, :LANES])
        den = jnp.where(left, l[:blk], l[blk:])
        if dil == 1:
            o_ref[0, p, rows, :] = (num / den).astype(o_ref.dtype)
        else:
            o_stage[p, rows, :] = num / den
        if lse_ref is not None:
            m1 = jnp.where(left, m[:blk], m[blk:])
            lse_ref[0, p, rows, :] = m1 + jnp.log(den)
    kp_ref[cls] = k_ref[0, 0]
    vp_ref[cls] = v_ref[0, 0]


def _decode_kernel(q_ref, na_ref, nb1_ref, nb2_ref, nb3_ref, ca_ref, cb1_ref, cb2_ref, cb3_ref,
                   ma_ref, mb1_ref, mb2_ref, mb3_ref, mn_ref, relb_ref, sinks_ref, oa_ref, ob_ref,
                   ta_ref, tb1_ref, tb2_ref, tb3_ref, tn_ref, sink_ref, *, dec_seq, head_cols, sink_idx, dils, init):
    t = dec_seq
    if init:
        for g, (m_ref, t_ref) in enumerate(zip((ma_ref, mb1_ref, mb2_ref, mb3_ref),
                                               (ta_ref, tb1_ref, tb2_ref, tb3_ref))):
            top = lax.broadcasted_iota(jnp.int32, m_ref.shape, 0) < t
            top_n = lax.broadcasted_iota(jnp.int32, (2 * t, LANES), 0) < t
            for p in range(N_PAIRS):
                cols = (head_cols[g][2 * p], head_cols[g][2 * p + 1])
                rows = slice(p * 2 * t, (p + 1) * 2 * t)
                t_ref[rows, :] = _lookup_rows(m_ref[...], top, relb_ref, cols[0], cols[1], NEG_INF)
                tn_ref[g, rows, :] = _lookup_rows(mn_ref[g], top_n, relb_ref, cols[0], cols[1], NEG_INF)
                if g == 0:
                    sink_ref[rows, :] = jnp.where(top_n, sinks_ref[0, sink_idx[2 * p]],
                                                  sinks_ref[0, sink_idx[2 * p + 1]])
        return

    left8 = lax.broadcasted_iota(jnp.int32, (2 * t, LANES), 1) < HEAD_DIM
    top8 = lax.broadcasted_iota(jnp.int32, (2 * t, LANES), 0) < t
    own = (left8 == top8).astype(jnp.float32)
    left4 = lax.broadcasted_iota(jnp.int32, (t, LANES), 1) < HEAD_DIM
    nt = (((1,), (1,)), ((), ()))

    def q_rows(c0):
        qp = q_ref[0, :, c0:c0 + LANES].astype(jnp.float32)
        return jnp.concatenate([qp, qp], axis=0) * own

    def attend(qrs, kts, vts, knews, vnews, tbl, tbl_new, sink, dil):
        shared = len(kts) == 1
        rows8 = lambda a, u: jnp.broadcast_to(a[u:u + 1, :], (2 * t, LANES))
        qr = jnp.concatenate(qrs, axis=0)
        qb = qr.astype(jnp.bfloat16)
        if shared:
            s_c = jnp.dot(qb, kts[0], preferred_element_type=jnp.float32)
        else:
            s_c = jnp.concatenate([jnp.dot(qb[2 * t * p:2 * t * (p + 1)], kts[p], preferred_element_type=jnp.float32)
                                   for p in range(N_PAIRS)], axis=0)
        yield
        s_c = s_c + tbl
        kn = [jnp.concatenate([rows8(knews[0 if shared else p], u) for p in range(N_PAIRS)], axis=0)
              for u in range(t)]
        vn = [jnp.concatenate([rows8(vnews[0 if shared else p], u) for p in range(N_PAIRS)], axis=0)
              for u in range(t)]
        s_n = [jnp.sum(qr * kn[u], axis=-1, keepdims=True) + tbl_new[:, u:u + 1] for u in range(t)]
        m = functools.reduce(jnp.maximum, s_n, jnp.max(s_c, axis=-1, keepdims=True))
        if sink is not None:
            m = jnp.maximum(m, sink)
        pc = jnp.exp(s_c - m)
        pn = [jnp.exp(x - m) for x in s_n]
        l = functools.reduce(lambda a, b: a + b, pn, jnp.sum(pc, axis=-1, keepdims=True))
        if sink is not None:
            l = l + jnp.exp(sink - m)
        pb = pc.astype(jnp.bfloat16)
        yield
        if shared:
            o = lax.dot_general(pb, vts[0], nt, preferred_element_type=jnp.float32)
        elif dil >= t:
            lane8 = lax.broadcasted_iota(jnp.int32, (2 * t, LANES), 1)
            row8 = lax.broadcasted_iota(jnp.int32, (2 * t, LANES), 0)
            sel = (lane8 % dil == row8 % t).astype(jnp.bfloat16)
            parts = []
            for p in range(N_PAIRS):
                accs = []
                for side in range(2):
                    r0 = 2 * t * p + t * side
                    w = jnp.sum(pc[r0:r0 + t], axis=0, keepdims=True)
                    v = vts[p][side * HEAD_DIM:(side + 1) * HEAD_DIM]
                    accs.append(functools.reduce(
                        lambda a, b: a + b,
                        [v[:, c:c + LANES] * w[:, c:c + LANES] for c in range(0, v.shape[1], LANES)]))
                acc = jnp.concatenate(accs, axis=0)
                hi = acc.astype(jnp.bfloat16)
                lo = (acc - hi.astype(jnp.float32)).astype(jnp.bfloat16)
                parts.append(lax.dot_general(sel, hi, nt, preferred_element_type=jnp.float32)
                             + lax.dot_general(sel, lo, nt, preferred_element_type=jnp.float32))
            o = jnp.concatenate(parts, axis=0)
        else:
            o = jnp.concatenate([lax.dot_general(pb[2 * t * p:2 * t * (p + 1)], vts[p], nt,
                                                 preferred_element_type=jnp.float32)
                                 for p in range(N_PAIRS)], axis=0)
        for u in range(t):
            o = o + pn[u] * vn[u]
        yield o / l, m + jnp.log(l)

    def fold(x, p):
        return jnp.where(left4, x[2 * t * p:2 * t * p + t], x[2 * t * p + t:2 * t * (p + 1)])

    def cache_kv(c_ref, kv, p, dtype=jnp.bfloat16):
        return c_ref[0, kv, 2 * p:2 * p + 2].reshape(LANES, c_ref.shape[-1]).astype(dtype)

    caches = (cb1_ref, cb2_ref, cb3_ref)
    news = (nb1_ref, nb2_ref, nb3_ref)
    tbls = (tb1_ref, tb2_ref, tb3_ref)
    groups = [attend([q_rows(p * LANES) for p in range(N_PAIRS)],
                     [cache_kv(ca_ref, 0, 0)], [cache_kv(ca_ref, 1, 0)],
                     [na_ref[0, :, 0:LANES]], [na_ref[0, :, LANES:2 * LANES]],
                     ta_ref[...], tn_ref[0], sink_ref[:, 0:1], dils[0])]
    for g in range(N_B_GROUPS):
        groups.append(attend(
            [q_rows(A_WIDTH + g * B_WIDTH + p * LANES) for p in range(N_PAIRS)],
            [cache_kv(caches[g], 0, p) for p in range(N_PAIRS)],
            [cache_kv(caches[g], 1, p, jnp.float32 if dils[1 + g] >= t else jnp.bfloat16) for p in range(N_PAIRS)],
            [news[g][0, :, p * LANES:(p + 1) * LANES] for p in range(N_PAIRS)],
            [news[g][0, :, B_WIDTH + p * LANES:B_WIDTH + (p + 1) * LANES] for p in range(N_PAIRS)],
            tbls[g][...], tn_ref[1 + g], None, dils[1 + g]))
    for phase in range(2):
        for gen in groups:
            next(gen)
    results = [next(gen) for gen in groups]
    oa = results[0][0]
    for p in range(N_PAIRS):
        oa_ref[0, p] = fold(oa, p).astype(oa_ref.dtype)
    outs = [o for o, _ in results[1:]]
    lses = [jnp.broadcast_to(lse, o.shape) for o, lse in results[1:]]
    mx = jnp.maximum(jnp.maximum(lses[0], lses[1]), lses[2])
    es = [jnp.exp(x - mx) for x in lses]
    den = es[0] + es[1] + es[2]
    comb = (es[0] * outs[0] + es[1] * outs[1] + es[2] * outs[2]) / den
    for p in range(N_PAIRS):
        ob_ref[0, p] = fold(comb, p).astype(ob_ref.dtype)


def _decode_bucket_maps(dils, cache_lens, dec_seq):
    t_idx = np.arange(dec_seq)
    maps = []
    new = np.full((len(dils), 2 * dec_seq, LANES), -1, np.int32)
    for g, (d, ln) in enumerate(zip(dils, cache_lens)):
        back = ln + t_idx[:, None] - np.arange(ln)[None, :]
        ok = (back % d == 0) & (back // d >= 1) & (back // d < N_KEYS)
        maps.append(jnp.asarray(np.tile(np.where(ok, _t5_bucket_np(back), -1), (2, 1)).astype(np.int32)))
        backn = t_idx[:, None] - t_idx[None, :]
        okn = (backn >= 0) & (backn % d == 0) & (backn // d < N_KEYS)
        new[g, :, :dec_seq] = np.tile(np.where(okn, _t5_bucket_np(backn), -1), (2, 1))
    return maps, jnp.asarray(new)


def _decode_band_kernel(*refs, dec_seq, head_cols, sink_idx, dils, blocks_per_batch):
    n_g = len(dils)
    dec_in = refs[0:16]
    band_in = [refs[16 + 4 * g:16 + 4 * (g + 1)] for g in range(n_g)]
    outs = refs[16 + 4 * n_g:16 + 4 * n_g + 2 + 2 * n_g - 1]
    oa_ref, ob_ref = outs[0:2]
    o_refs = outs[2:2 + n_g]
    lse_refs = (None,) + tuple(outs[2 + n_g:])
    scratch = refs[16 + 4 * n_g + 2 + 2 * n_g - 1:]
    dec_scratch = scratch[0:6]
    bias_refs = scratch[6:6 + n_g]
    sink_tile = scratch[6 + n_g]
    kprev_refs = scratch[7 + n_g:7 + 2 * n_g]
    vprev_refs = scratch[7 + 2 * n_g:7 + 3 * n_g]
    stages = iter(scratch[7 + 3 * n_g:])
    stage_refs = [None if d == 1 else next(stages) for d in dils]
    blk = N_KEYS
    s = pl.program_id(0)
    decode = functools.partial(_decode_kernel, *dec_in, oa_ref, ob_ref, *dec_scratch, dec_seq=dec_seq,
                               head_cols=head_cols, sink_idx=sink_idx, dils=dils)

    @pl.when(s == 0)
    def _():
        top = lax.broadcasted_iota(jnp.int32, (2 * blk, blk), 0) < blk
        for g in range(n_g):
            bmap = band_in[g][3][...]
            for p in range(N_PAIRS):
                bias_refs[g][p] = _lookup_rows(bmap, top, dec_in[14], head_cols[g][2 * p], head_cols[g][2 * p + 1],
                                               0.0)
            kprev_refs[g][...] = jnp.zeros(kprev_refs[g].shape, kprev_refs[g].dtype)
            vprev_refs[g][...] = jnp.zeros(vprev_refs[g].shape, vprev_refs[g].dtype)
        for p in range(N_PAIRS):
            sink_tile[p] = jnp.where(top, dec_in[15][0, sink_idx[2 * p]], dec_in[15][0, sink_idx[2 * p + 1]])
        decode(init=True)

    r = s % blocks_per_batch

    decode(init=False)
    for g in range(n_g):
        q_ref, k_ref, v_ref, _ = band_in[g]
        _band_block(q_ref, k_ref, kprev_refs[g], v_ref, vprev_refs[g], o_refs[g], stage_refs[g], lse_refs[g],
                    bias_refs[g], sink_tile if g == 0 else None, cls=r % dils[g], first=(r // dils[g]) == 0,
                    dil=dils[g], shared_kv=(g == 0))

    for g in range(n_g):
        if dils[g] > 1:
            @pl.when(r % dils[g] == dils[g] - 1)
            def _():
                for p in range(N_PAIRS):
                    o_refs[g][0, p] = stage_refs[g][p].astype(o_refs[g].dtype)


def _decode_band_attention(q_dec, news, caches, qs, ks, vs, rel_bias, sinks, *, dils, head_cols, sink_idx):
    n, t, _ = q_dec.shape
    nb = qs[0].shape[0]
    seq = qs[0].shape[1] * qs[0].shape[2]
    ppb = seq // N_KEYS
    assert n == nb * ppb, "one prompt block of every group per sample sequence"
    seq3 = lambda i: (i, 0, 0)
    seq4 = lambda i: (i, 0, 0, 0)
    seq5 = lambda i: (i, 0, 0, 0, 0)
    cache_lens = tuple(c.shape[-1] for c in caches)
    maps, map_new = _decode_bucket_maps(dils, cache_lens, t)
    smem = functools.partial(pl.BlockSpec, memory_space=pltpu.SMEM)
    in_specs = [pl.BlockSpec((1, t, q_dec.shape[2]), seq3)]
    in_specs += [pl.BlockSpec((1, t, a.shape[2]), seq3) for a in news]
    in_specs += [pl.BlockSpec((1,) + c.shape[1:], seq5) for c in caches]
    in_specs += [pl.BlockSpec(m.shape, lambda i: (0, 0)) for m in maps]
    in_specs += [pl.BlockSpec(map_new.shape, lambda i: (0, 0, 0)), smem(), smem()]
    band_args = []
    out_specs = [pl.BlockSpec((1, N_PAIRS, t, LANES), seq4)] * 2
    out_shape = [jax.ShapeDtypeStruct((n, N_PAIRS, t, LANES), jnp.float32)] * 2
    qi = np.arange(N_KEYS)[:, None]
    ci = np.arange(N_KEYS)[None, :]
    for g, d in enumerate(dils):
        cur = lambda s, d=d: (s // ppb, (s % ppb) % d, (s % ppb) // d, 0)
        wq, wkv = qs[g].shape[3], ks[g].shape[3]
        bmap = jnp.asarray(np.tile(_t5_bucket_np(((qi - ci) % N_KEYS) * d), (2, 1)).astype(np.int32))
        in_specs += [pl.BlockSpec((1, 1, N_KEYS, wq), cur), pl.BlockSpec((1, 1, N_KEYS, wkv), cur),
                     pl.BlockSpec((1, 1, N_KEYS, wkv), cur), pl.BlockSpec(bmap.shape, lambda s: (0, 0))]
        band_args += [qs[g], ks[g], vs[g], bmap]
    tile = lambda d: pl.BlockSpec((1, N_PAIRS, N_KEYS * d, LANES), lambda s, d=d: (s // ppb, 0, (s % ppb) // d, 0))
    slab_shape = lambda dtype: jax.ShapeDtypeStruct((nb, N_PAIRS, seq, LANES), dtype)
    out_specs += [tile(d) for d in dils] + [tile(d) for d in dils[1:]]
    out_shape += [slab_shape(jnp.bfloat16)] * len(dils) + [slab_shape(jnp.float32)] * (len(dils) - 1)
    n_rows = 2 * t * N_PAIRS
    tile_scr = pltpu.VMEM((N_PAIRS, 2 * N_KEYS, N_KEYS), jnp.float32)
    res = pl.pallas_call(
        functools.partial(_decode_band_kernel, dec_seq=t, head_cols=head_cols, sink_idx=sink_idx, dils=dils,
                          blocks_per_batch=ppb),
        grid=(n,),
        in_specs=in_specs,
        out_specs=out_specs,
        out_shape=out_shape,
        scratch_shapes=[pltpu.VMEM((n_rows, ln), jnp.float32) for ln in cache_lens]
        + [pltpu.VMEM((len(dils), n_rows, LANES), jnp.float32), pltpu.VMEM((n_rows, LANES), jnp.float32)]
        + [tile_scr] * (len(dils) + 1)
        + [pltpu.VMEM((d, N_KEYS, k.shape[3]), jnp.bfloat16) for d, k in zip(dils, ks)] * 2
        + [pltpu.VMEM((N_PAIRS, N_KEYS * d, LANES), jnp.float32) for d in dils if d > 1],
        compiler_params=pltpu.CompilerParams(
            dimension_semantics=("arbitrary",), vmem_limit_bytes=VMEM_LIMIT),
        name="decode_band_attn",
    )(q_dec, *news, *caches, *maps, map_new, rel_bias, sinks, *band_args)
    ng = len(dils)
    return res[0], res[1], res[2:2 + ng], res[2 + ng:]


def _out_kernel(*refs, n_groups):
    x_ref, ng_ref, wg_ref, oa_ref = refs[0:4]
    ob_refs = refs[4:4 + n_groups]
    lse_refs = refs[4 + n_groups:4 + 2 * n_groups] if n_groups > 1 else ()
    wa_ref, wb_ref, wo_ref, y_ref = refs[-4:]
    x = x_ref[0]
    h = _rmsnorm_bf16(x, ng_ref[...])

    def gate(c0, width, silu):
        a = jnp.dot(h, wg_ref[:, c0:c0 + width], preferred_element_type=jnp.float32)
        sg = 1.0 / (1.0 + jnp.exp(-a))
        return a * sg if silu else sg

    def slabs(ref):
        return jnp.concatenate([ref[0, p].astype(jnp.float32) for p in range(N_PAIRS)], axis=1)

    m0 = A_WIDTH + B_WIDTH
    ga = gate(0, A_WIDTH, True)
    gb = gate(A_WIDTH, B_WIDTH, True)
    ma = gate(m0, D_MODEL, False)
    mb = gate(m0 + D_MODEL, D_MODEL, False)
    if n_groups > 1:
        parts = []
        for p in range(N_PAIRS):
            lses = [r[0, p] for r in lse_refs]
            mx = functools.reduce(jnp.maximum, lses)
            es = [jnp.exp(v - mx) for v in lses]
            den = functools.reduce(lambda a, b: a + b, es)
            num = functools.reduce(lambda a, b: a + b,
                                   [e * r[0, p].astype(jnp.float32) for e, r in zip(es, ob_refs)])
            parts.append(num / den)
        ob = jnp.concatenate(parts, axis=1)
    else:
        ob = slabs(ob_refs[0])
    oa = slabs(oa_ref)
    ya = jnp.dot((oa * ga).astype(jnp.bfloat16), wa_ref[...], preferred_element_type=jnp.float32)
    yb = jnp.dot((ob * gb).astype(jnp.bfloat16), wb_ref[...], preferred_element_type=jnp.float32)
    merged = (ma * ya + mb * yb).astype(jnp.bfloat16)
    y_ref[0] = x + jnp.dot(merged, wo_ref[...], preferred_element_type=jnp.float32)


def _out_proj(x3d, ng, wg, oa, obs, lses, wa, wb, wo, *, tm):
    nb, seq, _ = x3d.shape
    row = lambda b, i: (b, i, 0)
    slab = pl.BlockSpec((1, N_PAIRS, tm, LANES), lambda b, i: (b, 0, i, 0))
    const = lambda b, i: (0, 0)
    once = dict(pipeline_mode=pl.Buffered(1))
    in_specs = [pl.BlockSpec((1, tm, D_MODEL), row), pl.BlockSpec((1, D_MODEL), const),
                pl.BlockSpec(wg.shape, const, **once), slab]
    in_specs += [slab for _ in obs] + [slab for _ in lses]
    in_specs += [pl.BlockSpec(wa.shape, const, **once), pl.BlockSpec(wb.shape, const, **once),
                 pl.BlockSpec(wo.shape, const, **once)]
    return pl.pallas_call(
        functools.partial(_out_kernel, n_groups=len(obs)),
        grid=(nb, seq // tm),
        in_specs=in_specs,
        out_specs=pl.BlockSpec((1, tm, D_MODEL), row),
        out_shape=jax.ShapeDtypeStruct((nb, seq, D_MODEL), jnp.float32),
        compiler_params=pltpu.CompilerParams(
            dimension_semantics=("arbitrary", "arbitrary"), vmem_limit_bytes=VMEM_LIMIT),
        name="out_proj",
    )(x3d, ng, wg, oa, *obs, *lses, wa, wb, wo)


def _prep_params(w_in, q_gain_a, k_gain_a, q_gain_b, k_gain_b, w_up_a, w_up_b):
    offs = np.cumsum((0, A_WIDTH, A_KV_WIDTH, A_KV_WIDTH, A_WIDTH, 3 * B_WIDTH, 3 * B_WIDTH, 3 * B_WIDTH,
                      B_WIDTH, D_MODEL, D_MODEL))
    seg = lambda s: w_in[:, int(offs[s]):int(offs[s + 1])]

    def perm_heads(w2d):
        return jnp.concatenate([w2d[:, h * HEAD_DIM:(h + 1) * HEAD_DIM] for h in A_HEAD_ORDER], axis=1)

    w_qkv = jnp.concatenate([perm_heads(seg(0)), seg(4), seg(1), seg(5), seg(2), seg(6)],
                            axis=1).astype(jnp.bfloat16)
    w_gate = jnp.concatenate([perm_heads(seg(3)), seg(7), seg(8), seg(9)], axis=1).astype(jnp.bfloat16)
    gq = jnp.broadcast_to(jnp.concatenate([q_gain_a[None], q_gain_b], axis=0)[:, None, :] * Q_SCALE,
                          (1 + N_B_GROUPS, A_Q_HEADS, HEAD_DIM)).reshape(1, Q_COLS)
    gk = jnp.concatenate([jnp.broadcast_to(k_gain_a, (A_KV_HEADS, HEAD_DIM)).reshape(A_KV_WIDTH),
                          jnp.broadcast_to(k_gain_b[:, None, :], (N_B_GROUPS, B_HEADS, HEAD_DIM)).reshape(-1)])[None]
    hd = np.arange(MXU_COLS) // HEAD_DIM
    ones = jnp.asarray((hd[:, None] == hd[None, :]).astype(np.float32) / HEAD_DIM, jnp.bfloat16)
    wa = jnp.concatenate([w_up_a[h * HEAD_DIM:(h + 1) * HEAD_DIM] for h in A_HEAD_ORDER], axis=0)
    return w_qkv, w_gate, gq, gk, ones, wa.astype(jnp.bfloat16), w_up_b.astype(jnp.bfloat16)


def kernel(x_prompt, x_sample, cache_a_kv, cache_b1_kv, cache_b2_kv, cache_b3_kv, rel_bias, norm_gain, w_in,
           q_gain_a, k_gain_a, sinks_a, q_gain_b, k_gain_b, w_up_a, w_up_b, w_out):
    assert norm_gain.shape[0] == 1, "single layer"
    nb, seq, _ = x_prompt.shape
    n_dec, dec_seq, _ = x_sample.shape
    w_qkv, w_gate, gq, gk, ones, wa, wb = _prep_params(
        w_in[0], q_gain_a[0], k_gain_a[0], q_gain_b[0], k_gain_b[0], w_up_a[0], w_up_b[0])
    wo = w_out[0].astype(jnp.bfloat16)
    ng = norm_gain
    windows = (A_WINDOW,) + tuple(w for w, _ in B_GROUPS)
    dils = (1,) + tuple(d for _, d in B_GROUPS)
    head_cols = (A_HEAD_ORDER,) + tuple(tuple(range(A_Q_HEADS + g * B_HEADS, A_Q_HEADS + (g + 1) * B_HEADS))
                                        for g in range(N_B_GROUPS))

    p_rows = tuple(min(w, seq) for w in windows)
    qs, ks, vs, states = _qkv_proj(x_prompt.reshape(nb * seq, D_MODEL), ng, w_qkv, gq, gk, ones,
                                   nb=nb, seq=seq, state_rows=p_rows, dils=dils, tm=min(seq, PROJ_TILE_ROWS),
                                   states_t=True)
    t_dec = n_dec * dec_seq
    qd, _, _, news = _qkv_proj(x_sample.reshape(t_dec, D_MODEL), ng, w_qkv, gq, gk, ones,
                               nb=1, seq=t_dec, state_rows=(t_dec,) * 4, dils=(1, 1, 1, 1),
                               tm=min(t_dec, PROJ_TILE_ROWS),
                               states_t=False)

    q_dec = jnp.concatenate([a.reshape(n_dec, dec_seq, A_WIDTH) for a in qd], axis=2)
    caches = [jnp.transpose(c[0], (0, 2, 3, 4, 1)) for c in (cache_a_kv, cache_b1_kv, cache_b2_kv, cache_b3_kv)]
    news3 = [a.reshape(n_dec, dec_seq, a.shape[1]) for a in news]
    oa_s, ob_s, o_prompt, lses = _decode_band_attention(
        q_dec, news3, caches, qs, ks, vs, rel_bias, sinks_a, dils=dils, head_cols=head_cols, sink_idx=A_HEAD_ORDER)

    y_prompt = _out_proj(x_prompt, ng, w_gate, o_prompt[0], list(o_prompt[1:]), list(lses), wa, wb, wo,
                         tm=min(seq, PROJ_TILE_ROWS))
    heads = (A_KV_HEADS, B_HEADS, B_HEADS, B_HEADS)
    new_prompt = tuple(jnp.transpose(states[g].reshape(1, nb, 2, heads[g], HEAD_DIM, p_rows[g]), (0, 1, 5, 2, 3, 4))
                       for g in range(4))
    to_slabs = lambda o: jnp.transpose(o, (1, 0, 2, 3)).reshape(1, N_PAIRS, t_dec, LANES)
    y_sample = _out_proj(x_sample.reshape(1, t_dec, D_MODEL), ng, w_gate, to_slabs(oa_s), [to_slabs(ob_s)], [],
                         wa, wb, wo, tm=min(t_dec, PROJ_TILE_ROWS))
    y_sample = y_sample.reshape(n_dec, dec_seq, D_MODEL)
    new_sample = tuple(news3[g].reshape(1, n_dec, dec_seq, 2, heads[g], HEAD_DIM) for g in range(4))
    return (y_prompt, y_sample) + new_prompt + new_sample
```

```python
import functools
import math

import numpy as np
import jax
import jax.numpy as jnp
from jax import lax
from jax.experimental import pallas as pl
from jax.experimental.pallas import tpu as pltpu

D_MODEL = 1024
HEAD_DIM = 64
A_Q_HEADS = 8
A_KV_HEADS = 2
A_WINDOW = 128
B_GROUPS = ((128, 1), (512, 4), (2048, 16))
N_B_GROUPS = 3
B_HEADS = 8
N_KEYS = 128
A_WIDTH = A_Q_HEADS * HEAD_DIM
A_KV_WIDTH = A_KV_HEADS * HEAD_DIM
B_WIDTH = B_HEADS * HEAD_DIM
N_BUCKETS = 32
MAX_DISTANCE = 2048
EPS = 1e-6
NEG_INF = -1e30
Q_SCALE = HEAD_DIM ** -0.5

LANES = 128
MXU_COLS = 256
VMEM_LIMIT = 56 * 1024 * 1024
PROJ_TILE_ROWS = 512
N_PAIRS = A_WIDTH // LANES

A_HEAD_ORDER = (0, 4, 1, 5, 2, 6, 3, 7)

Q_COLS = A_WIDTH + N_B_GROUPS * B_WIDTH
KV_COLS = A_KV_WIDTH + N_B_GROUPS * B_WIDTH
QKV_COLS = Q_COLS + 2 * KV_COLS
G_COLS = A_WIDTH + B_WIDTH + 2 * D_MODEL
KV_WIDTHS = (A_KV_WIDTH, B_WIDTH, B_WIDTH, B_WIDTH)
NORM_BATCH = 4 * B_WIDTH // MXU_COLS


def _t5_bucket_np(dist):
    max_exact = N_BUCKETS // 2
    d = np.maximum(dist, 0)
    df = np.maximum(d, 1).astype(np.float32)
    large = max_exact + (np.log(df / np.float32(max_exact)) / np.float32(math.log(MAX_DISTANCE / max_exact))
                         * np.float32(N_BUCKETS - max_exact)).astype(np.int32)
    large = np.minimum(large, N_BUCKETS - 1)
    return np.where(d < max_exact, d, large)


def _rmsnorm_bf16(x, gain):
    ms = jnp.mean(x * x, axis=-1, keepdims=True)
    return (x * lax.rsqrt(ms + EPS) * gain).astype(jnp.bfloat16)


def _state_plan(n_rows, seq, tm):
    tpb = seq // tm
    r = min(n_rows, tm)
    nblk = max(n_rows // tm, 1)
    return tpb, r, nblk, tpb - nblk


def _qkv_kernel(x_ref, ng_ref, w_ref, gq_ref, gk_ref, ones_ref, *rest, tm, seq, state_rows, dils, states_t):
    q_refs = rest[0:4]
    k_refs = rest[4:8]
    v_refs = rest[8:12]
    state_refs = rest[12:16]
    tmp_ref, a_scr, sq_scr, ss_scr = rest[16:20]
    h = _rmsnorm_bf16(x_ref[...], ng_ref[...])
    ones = ones_ref[...]

    def proj(col0, w):
        return jnp.dot(h, w_ref[:, col0:col0 + w], preferred_element_type=jnp.float32)

    def emit(a, c, out_ref, d, state):
        w = a.shape[1]
        if d == 1:
            out_ref[0, 0, :, c:c + w] = a.astype(out_ref.dtype)
        else:
            for s in range(w // LANES):
                tmp_ref[c // LANES + s] = a[:, s * LANES:(s + 1) * LANES]
        if state is not None:
            sref, kv, r = state
            if states_t:
                sref[0, kv, c:c + w, :] = a[tm - r:, :].T
            else:
                wd = sref.shape[1] // 2
                sref[:, kv * wd + c:kv * wd + c + w] = a[tm - r:, :]

    def finish(out_ref, d, width):
        if d > 1:
            for cls in range(d):
                for s in range(width // LANES):
                    out_ref[0, cls, :, s * LANES:(s + 1) * LANES] = (
                        tmp_ref[s, pl.ds(cls, tm // d, stride=d), :].astype(out_ref.dtype))

    def normed_batch(arrays):
        chunks = [(ai, c) for ai in range(len(arrays)) for c in range(0, B_WIDTH, MXU_COLS)]
        for ci, (ai, c) in enumerate(chunks):
            a = proj(arrays[ai][0] + c, MXU_COLS)
            a_scr[ci] = a
            sq_scr[ci * tm:(ci + 1) * tm, :] = (a * a).astype(jnp.bfloat16)
        n = len(chunks) * tm
        ss_scr[0:n, :] = jnp.dot(sq_scr[0:n, :], ones, preferred_element_type=jnp.float32)
        for ci, (ai, c) in enumerate(chunks):
            _, gain_ref, gcol0, out_ref, d, state = arrays[ai]
            o = (a_scr[ci] * lax.rsqrt(ss_scr[ci * tm:(ci + 1) * tm, :] + EPS)
                 * gain_ref[:, gcol0 + c:gcol0 + c + MXU_COLS])
            emit(o, c, out_ref, d, state)
            if c + MXU_COLS == B_WIDTH:
                finish(out_ref, d, B_WIDTH)

    q_off = (0, A_WIDTH, A_WIDTH + B_WIDTH, A_WIDTH + 2 * B_WIDTH)
    kv_off = (0, A_KV_WIDTH, A_KV_WIDTH + B_WIDTH, A_KV_WIDTH + 2 * B_WIDTH)
    plan = [_state_plan(state_rows[g], seq, tm)[1] for g in range(4)]
    normed_batch([(q_off[g], gq_ref, q_off[g], q_refs[g], dils[g], None) for g in range(4)])
    normed_batch([(Q_COLS + kv_off[g], gk_ref, kv_off[g], k_refs[g], dils[g], (state_refs[g], 0, plan[g]))
                  for g in range(1, 4)])
    a = proj(Q_COLS, A_KV_WIDTH)
    ss = jnp.dot((a * a).astype(jnp.bfloat16), ones[:A_KV_WIDTH, :A_KV_WIDTH], preferred_element_type=jnp.float32)
    emit(a * lax.rsqrt(ss + EPS) * gk_ref[:, 0:A_KV_WIDTH], 0, k_refs[0], dils[0], (state_refs[0], 0, plan[0]))
    for g in range(4):
        wd = KV_WIDTHS[g]
        for c in range(0, wd, MXU_COLS):
            w = min(MXU_COLS, wd - c)
            emit(proj(Q_COLS + KV_COLS + kv_off[g] + c, w), c, v_refs[g], dils[g], (state_refs[g], 1, plan[g]))
        finish(v_refs[g], dils[g], wd)


def _qkv_proj(x2d, ng, w_qkv, gq, gk, ones, *, nb, seq, state_rows, dils, tm, states_t):
    t = x2d.shape[0]
    tpb = seq // tm

    def cls_shape(d, width):
        return jax.ShapeDtypeStruct((nb, d, seq // d, width), jnp.bfloat16)

    def cls_spec(d, width):
        return pl.BlockSpec((1, d, tm // d, width), lambda i: (i // tpb, 0, i % tpb, 0))

    def state_shape(g):
        if states_t:
            return jax.ShapeDtypeStruct((nb, 2, KV_WIDTHS[g], state_rows[g]), jnp.float32)
        return jax.ShapeDtypeStruct((nb * state_rows[g], 2 * KV_WIDTHS[g]), jnp.float32)

    def state_spec(g):
        _, r, nblk, j0 = _state_plan(state_rows[g], seq, tm)
        if states_t:
            return pl.BlockSpec((1, 2, KV_WIDTHS[g], r), lambda i: (i // tpb, 0, 0, jnp.maximum(i % tpb - j0, 0)))
        return pl.BlockSpec((r, 2 * KV_WIDTHS[g]),
                            lambda i: ((i // tpb) * nblk + jnp.maximum(i % tpb - j0, 0), 0))

    const = lambda i: (0, 0)
    out_shape = ([cls_shape(d, A_WIDTH) for d in dils]
                 + [cls_shape(d, w) for d, w in zip(dils, KV_WIDTHS)] * 2
                 + [state_shape(g) for g in range(4)])
    out_specs = ([cls_spec(d, A_WIDTH) for d in dils]
                 + [cls_spec(d, w) for d, w in zip(dils, KV_WIDTHS)] * 2
                 + [state_spec(g) for g in range(4)])
    res = pl.pallas_call(
        functools.partial(_qkv_kernel, tm=tm, seq=seq, state_rows=state_rows, dils=dils, states_t=states_t),
        grid=(t // tm,),
        in_specs=[
            pl.BlockSpec((tm, D_MODEL), lambda i: (i, 0)),
            pl.BlockSpec((1, D_MODEL), const),
            pl.BlockSpec((D_MODEL, QKV_COLS), const, pipeline_mode=pl.Buffered(1)),
            pl.BlockSpec((1, Q_COLS), const),
            pl.BlockSpec((1, KV_COLS), const),
            pl.BlockSpec((MXU_COLS, MXU_COLS), const),
        ],
        out_specs=out_specs,
        out_shape=out_shape,
        scratch_shapes=[pltpu.VMEM((N_PAIRS, tm, LANES), jnp.float32),
                        pltpu.VMEM((NORM_BATCH, tm, MXU_COLS), jnp.float32),
                        pltpu.VMEM((NORM_BATCH * tm, MXU_COLS), jnp.bfloat16),
                        pltpu.VMEM((NORM_BATCH * tm, MXU_COLS), jnp.float32)],
        compiler_params=pltpu.CompilerParams(
            dimension_semantics=("arbitrary",), vmem_limit_bytes=VMEM_LIMIT),
        name="qkv_proj",
    )(x2d, ng, w_qkv, gq, gk, ones)
    return res[0:4], res[4:8], res[8:12], res[12:16]


def _lookup_rows(bucket_map, top, table_ref, col_top, col_bottom, fill):
    acc = jnp.full(bucket_map.shape, fill, jnp.float32)
    for b in range(N_BUCKETS):
        val = jnp.where(top, table_ref[b, col_top], table_ref[b, col_bottom])
        acc = jnp.where(bucket_map == b, val, acc)
    return acc


def _band_block(q_ref, k_ref, kp_ref, v_ref, vp_ref, o_ref, o_stage, lse_ref, bias_ref, sink_ref, *,
                cls, first, dil, shared_kv):
    blk = N_KEYS
    row = lax.broadcasted_iota(jnp.int32, (2 * blk, blk), 0)
    lane = lax.broadcasted_iota(jnp.int32, (2 * blk, blk), 1)
    ahead = lane - jnp.bitwise_and(row, blk - 1)
    tri = ahead <= 0
    valid = ahead <= jnp.where(first, 0, blk)
    cur_part = tri.astype(jnp.bfloat16)
    prev_part = jnp.logical_not(tri).astype(jnp.bfloat16)
    left = lax.broadcasted_iota(jnp.int32, (blk, LANES), 1) < HEAD_DIM
    mask_l = left.astype(jnp.bfloat16)
    mask_r = jnp.logical_not(left).astype(jnp.bfloat16)
    ones_kv = jnp.ones((2 * blk, LANES), jnp.bfloat16)
    nt = (((1,), (1,)), ((), ()))
    rows = pl.ds(0, blk) if dil == 1 else pl.ds(cls, blk, stride=dil)
    for p in range(N_PAIRS):
        kcol = 0 if shared_kv else p * LANES
        qp = q_ref[0, 0, :, p * LANES:(p + 1) * LANES]
        qs = jnp.concatenate([qp * mask_l, qp * mask_r], axis=0)
        kc = jnp.concatenate([kp_ref[cls, :, kcol:kcol + LANES], k_ref[0, 0, :, kcol:kcol + LANES]], axis=0)
        vc = jnp.concatenate([vp_ref[cls, :, kcol:kcol + LANES], v_ref[0, 0, :, kcol:kcol + LANES]], axis=0)
        s2 = lax.dot_general(qs, kc, nt, preferred_element_type=jnp.float32)
        s = jnp.where(tri, s2[:, blk:], s2[:, :blk]) + bias_ref[p]
        s = jnp.where(valid, s, NEG_INF)
        m = jnp.max(s, axis=-1, keepdims=True)
        if sink_ref is not None:
            sk = sink_ref[p]
            m = jnp.maximum(m, sk)
        pb = jnp.exp(s - m).astype(jnp.bfloat16)
        p2 = jnp.concatenate([pb * prev_part, pb * cur_part], axis=1)
        ov = jnp.dot(p2, jnp.concatenate([vc, ones_kv], axis=1), preferred_element_type=jnp.float32)
        l = ov[:, LANES:]
        if sink_ref is not None:
            l = l + jnp.exp(sk - m)
        num = jnp.where(left, ov[:blk, :LANES], ov[blk:, :LANES])
        den = jnp.where(left, l[:blk], l[blk:])
        if dil == 1:
            o_ref[0, p, rows, :] = (num / den).astype(o_ref.dtype)
        else:
            o_stage[p, rows, :] = num / den
        if lse_ref is not None:
            m1 = jnp.where(left, m[:blk], m[blk:])
            lse_ref[0, p, rows, :] = m1 + jnp.log(den)
    kp_ref[cls] = k_ref[0, 0]
    vp_ref[cls] = v_ref[0, 0]


def _decode_kernel(q_ref, na_ref, nb1_ref, nb2_ref, nb3_ref, ca_ref, cb1_ref, cb2_ref, cb3_ref,
                   ma_ref, mb1_ref, mb2_ref, mb3_ref, mn_ref, relb_ref, sinks_ref, oa_ref, ob_ref,
                   ta_ref, tb1_ref, tb2_ref, tb3_ref, tn_ref, sink_ref, *, dec_seq, head_cols, sink_idx, dils, init):
    t = dec_seq
    if init:
        for g, (m_ref, t_ref) in enumerate(zip((ma_ref, mb1_ref, mb2_ref, mb3_ref),
                                               (ta_ref, tb1_ref, tb2_ref, tb3_ref))):
            top = lax.broadcasted_iota(jnp.int32, m_ref.shape, 0) < t
            top_n = lax.broadcasted_iota(jnp.int32, (2 * t, LANES), 0) < t
            for p in range(N_PAIRS):
                cols = (head_cols[g][2 * p], head_cols[g][2 * p + 1])
                rows = slice(p * 2 * t, (p + 1) * 2 * t)
                t_ref[rows, :] = _lookup_rows(m_ref[...], top, relb_ref, cols[0], cols[1], NEG_INF)
                tn_ref[g, rows, :] = _lookup_rows(mn_ref[g], top_n, relb_ref, cols[0], cols[1], NEG_INF)
                if g == 0:
                    sink_ref[rows, :] = jnp.where(top_n, sinks_ref[0, sink_idx[2 * p]],
                                                  sinks_ref[0, sink_idx[2 * p + 1]])
        return

    left8 = lax.broadcasted_iota(jnp.int32, (2 * t, LANES), 1) < HEAD_DIM
    top8 = lax.broadcasted_iota(jnp.int32, (2 * t, LANES), 0) < t
    own = (left8 == top8).astype(jnp.float32)
    left4 = lax.broadcasted_iota(jnp.int32, (t, LANES), 1) < HEAD_DIM
    nt = (((1,), (1,)), ((), ()))

    def q_rows(c0):
        qp = q_ref[0, :, c0:c0 + LANES].astype(jnp.float32)
        return jnp.concatenate([qp, qp], axis=0) * own

    def attend(qrs, kts, vts, knews, vnews, tbl, tbl_new, sink, dil):
        shared = len(kts) == 1
        rows8 = lambda a, u: jnp.broadcast_to(a[u:u + 1, :], (2 * t, LANES))
        qr = jnp.concatenate(qrs, axis=0)
        qb = qr.astype(jnp.bfloat16)
        if shared:
            s_c = jnp.dot(qb, kts[0], preferred_element_type=jnp.float32)
        else:
            s_c = jnp.concatenate([jnp.dot(qb[2 * t * p:2 * t * (p + 1)], kts[p], preferred_element_type=jnp.float32)
                                   for p in range(N_PAIRS)], axis=0)
        yield
        s_c = s_c + tbl
        kn = [jnp.concatenate([rows8(knews[0 if shared else p], u) for p in range(N_PAIRS)], axis=0)
              for u in range(t)]
        vn = [jnp.concatenate([rows8(vnews[0 if shared else p], u) for p in range(N_PAIRS)], axis=0)
              for u in range(t)]
        s_n = [jnp.sum(qr * kn[u], axis=-1, keepdims=True) + tbl_new[:, u:u + 1] for u in range(t)]
        m = functools.reduce(jnp.maximum, s_n, jnp.max(s_c, axis=-1, keepdims=True))
        if sink is not None:
            m = jnp.maximum(m, sink)
        pc = jnp.exp(s_c - m)
        pn = [jnp.exp(x - m) for x in s_n]
        l = functools.reduce(lambda a, b: a + b, pn, jnp.sum(pc, axis=-1, keepdims=True))
        if sink is not None:
            l = l + jnp.exp(sink - m)
        pb = pc.astype(jnp.bfloat16)
        yield
        if shared:
            o = lax.dot_general(pb, vts[0], nt, preferred_element_type=jnp.float32)
        elif dil >= t:
            lane8 = lax.broadcasted_iota(jnp.int32, (2 * t, LANES), 1)
            row8 = lax.broadcasted_iota(jnp.int32, (2 * t, LANES), 0)
            sel = (lane8 % dil == row8 % t).astype(jnp.bfloat16)
            parts = []
            for p in range(N_PAIRS):
                accs = []
                for side in range(2):
                    r0 = 2 * t * p + t * side
                    w = jnp.sum(pc[r0:r0 + t], axis=0, keepdims=True)
                    v = vts[p][side * HEAD_DIM:(side + 1) * HEAD_DIM]
                    accs.append(functools.reduce(
                        lambda a, b: a + b,
                        [v[:, c:c + LANES] * w[:, c:c + LANES] for c in range(0, v.shape[1], LANES)]))
                acc = jnp.concatenate(accs, axis=0)
                hi = acc.astype(jnp.bfloat16)
                lo = (acc - hi.astype(jnp.float32)).astype(jnp.bfloat16)
                parts.append(lax.dot_general(sel, hi, nt, preferred_element_type=jnp.float32)
                             + lax.dot_general(sel, lo, nt, preferred_element_type=jnp.float32))
            o = jnp.concatenate(parts, axis=0)
        else:
            o = jnp.concatenate([lax.dot_general(pb[2 * t * p:2 * t * (p + 1)], vts[p], nt,
                                                 preferred_element_type=jnp.float32)
                                 for p in range(N_PAIRS)], axis=0)
        for u in range(t):
            o = o + pn[u] * vn[u]
        yield o / l, m + jnp.log(l)

    def fold(x, p):
        return jnp.where(left4, x[2 * t * p:2 * t * p + t], x[2 * t * p + t:2 * t * (p + 1)])

    def cache_kv(c_ref, kv, p, dtype=jnp.bfloat16):
        return c_ref[0, kv, 2 * p:2 * p + 2].reshape(LANES, c_ref.shape[-1]).astype(dtype)

    caches = (cb1_ref, cb2_ref, cb3_ref)
    news = (nb1_ref, nb2_ref, nb3_ref)
    tbls = (tb1_ref, tb2_ref, tb3_ref)
    groups = [attend([q_rows(p * LANES) for p in range(N_PAIRS)],
                     [cache_kv(ca_ref, 0, 0)], [cache_kv(ca_ref, 1, 0)],
                     [na_ref[0, :, 0:LANES]], [na_ref[0, :, LANES:2 * LANES]],
                     ta_ref[...], tn_ref[0], sink_ref[:, 0:1], dils[0])]
    for g in range(N_B_GROUPS):
        groups.append(attend(
            [q_rows(A_WIDTH + g * B_WIDTH + p * LANES) for p in range(N_PAIRS)],
            [cache_kv(caches[g], 0, p) for p in range(N_PAIRS)],
            [cache_kv(caches[g], 1, p, jnp.float32 if dils[1 + g] >= t else jnp.bfloat16) for p in range(N_PAIRS)],
            [news[g][0, :, p * LANES:(p + 1) * LANES] for p in range(N_PAIRS)],
            [news[g][0, :, B_WIDTH + p * LANES:B_WIDTH + (p + 1) * LANES] for p in range(N_PAIRS)],
            tbls[g][...], tn_ref[1 + g], None, dils[1 + g]))
    for phase in range(2):
        for gen in groups:
            next(gen)
    results = [next(gen) for gen in groups]
    oa = results[0][0]
    for p in range(N_PAIRS):
        oa_ref[0, p] = fold(oa, p).astype(oa_ref.dtype)
    outs = [o for o, _ in results[1:]]
    lses = [jnp.broadcast_to(lse, o.shape) for o, lse in results[1:]]
    mx = jnp.maximum(jnp.maximum(lses[0], lses[1]), lses[2])
    es = [jnp.exp(x - mx) for x in lses]
    den = es[0] + es[1] + es[2]
    comb = (es[0] * outs[0] + es[1] * outs[1] + es[2] * outs[2]) / den
    for p in range(N_PAIRS):
        ob_ref[0, p] = fold(comb, p).astype(ob_ref.dtype)


def _decode_bucket_maps(dils, cache_lens, dec_seq):
    t_idx = np.arange(dec_seq)
    maps = []
    new = np.full((len(dils), 2 * dec_seq, LANES), -1, np.int32)
    for g, (d, ln) in enumerate(zip(dils, cache_lens)):
        back = ln + t_idx[:, None] - np.arange(ln)[None, :]
        ok = (back % d == 0) & (back // d >= 1) & (back // d < N_KEYS)
        maps.append(jnp.asarray(np.tile(np.where(ok, _t5_bucket_np(back), -1), (2, 1)).astype(np.int32)))
        backn = t_idx[:, None] - t_idx[None, :]
        okn = (backn >= 0) & (backn % d == 0) & (backn // d < N_KEYS)
        new[g, :, :dec_seq] = np.tile(np.where(okn, _t5_bucket_np(backn), -1), (2, 1))
    return maps, jnp.asarray(new)


def _decode_band_kernel(*refs, dec_seq, head_cols, sink_idx, dils, blocks_per_batch):
    n_g = len(dils)
    dec_in = refs[0:16]
    band_in = [refs[16 + 4 * g:16 + 4 * (g + 1)] for g in range(n_g)]
    outs = refs[16 + 4 * n_g:16 + 4 * n_g + 2 + 2 * n_g - 1]
    oa_ref, ob_ref = outs[0:2]
    o_refs = outs[2:2 + n_g]
    lse_refs = (None,) + tuple(outs[2 + n_g:])
    scratch = refs[16 + 4 * n_g + 2 + 2 * n_g - 1:]
    dec_scratch = scratch[0:6]
    bias_refs = scratch[6:6 + n_g]
    sink_tile = scratch[6 + n_g]
    kprev_refs = scratch[7 + n_g:7 + 2 * n_g]
    vprev_refs = scratch[7 + 2 * n_g:7 + 3 * n_g]
    stages = iter(scratch[7 + 3 * n_g:])
    stage_refs = [None if d == 1 else next(stages) for d in dils]
    blk = N_KEYS
    s = pl.program_id(0)
    decode = functools.partial(_decode_kernel, *dec_in, oa_ref, ob_ref, *dec_scratch, dec_seq=dec_seq,
                               head_cols=head_cols, sink_idx=sink_idx, dils=dils)

    @pl.when(s == 0)
    def _():
        top = lax.broadcasted_iota(jnp.int32, (2 * blk, blk), 0) < blk
        for g in range(n_g):
            bmap = band_in[g][3][...]
            for p in range(N_PAIRS):
                bias_refs[g][p] = _lookup_rows(bmap, top, dec_in[14], head_cols[g][2 * p], head_cols[g][2 * p + 1],
                                               0.0)
            kprev_refs[g][...] = jnp.zeros(kprev_refs[g].shape, kprev_refs[g].dtype)
            vprev_refs[g][...] = jnp.zeros(vprev_refs[g].shape, vprev_refs[g].dtype)
        for p in range(N_PAIRS):
            sink_tile[p] = jnp.where(top, dec_in[15][0, sink_idx[2 * p]], dec_in[15][0, sink_idx[2 * p + 1]])
        decode(init=True)

    r = s % blocks_per_batch

    decode(init=False)
    for g in range(n_g):
        q_ref, k_ref, v_ref, _ = band_in[g]
        _band_block(q_ref, k_ref, kprev_refs[g], v_ref, vprev_refs[g], o_refs[g], stage_refs[g], lse_refs[g],
                    bias_refs[g], sink_tile if g == 0 else None, cls=r % dils[g], first=(r // dils[g]) == 0,
                    dil=dils[g], shared_kv=(g == 0))

    for g in range(n_g):
        if dils[g] > 1:
            @pl.when(r % dils[g] == dils[g] - 1)
            def _():
                for p in range(N_PAIRS):
                    o_refs[g][0, p] = stage_refs[g][p].astype(o_refs[g].dtype)


def _decode_band_attention(q_dec, news, caches, qs, ks, vs, rel_bias, sinks, *, dils, head_cols, sink_idx):
    n, t, _ = q_dec.shape
    nb = qs[0].shape[0]
    seq = qs[0].shape[1] * qs[0].shape[2]
    ppb = seq // N_KEYS
    assert n == nb * ppb, "one prompt block of every group per sample sequence"
    seq3 = lambda i: (i, 0, 0)
    seq4 = lambda i: (i, 0, 0, 0)
    seq5 = lambda i: (i, 0, 0, 0, 0)
    cache_lens = tuple(c.shape[-1] for c in caches)
    maps, map_new = _decode_bucket_maps(dils, cache_lens, t)
    smem = functools.partial(pl.BlockSpec, memory_space=pltpu.SMEM)
    in_specs = [pl.BlockSpec((1, t, q_dec.shape[2]), seq3)]
    in_specs += [pl.BlockSpec((1, t, a.shape[2]), seq3) for a in news]
    in_specs += [pl.BlockSpec((1,) + c.shape[1:], seq5) for c in caches]
    in_specs += [pl.BlockSpec(m.shape, lambda i: (0, 0)) for m in maps]
    in_specs += [pl.BlockSpec(map_new.shape, lambda i: (0, 0, 0)), smem(), smem()]
    band_args = []
    out_specs = [pl.BlockSpec((1, N_PAIRS, t, LANES), seq4)] * 2
    out_shape = [jax.ShapeDtypeStruct((n, N_PAIRS, t, LANES), jnp.float32)] * 2
    qi = np.arange(N_KEYS)[:, None]
    ci = np.arange(N_KEYS)[None, :]
    for g, d in enumerate(dils):
        cur = lambda s, d=d: (s // ppb, (s % ppb) % d, (s % ppb) // d, 0)
        wq, wkv = qs[g].shape[3], ks[g].shape[3]
        bmap = jnp.asarray(np.tile(_t5_bucket_np(((qi - ci) % N_KEYS) * d), (2, 1)).astype(np.int32))
        in_specs += [pl.BlockSpec((1, 1, N_KEYS, wq), cur), pl.BlockSpec((1, 1, N_KEYS, wkv), cur),
                     pl.BlockSpec((1, 1, N_KEYS, wkv), cur), pl.BlockSpec(bmap.shape, lambda s: (0, 0))]
        band_args += [qs[g], ks[g], vs[g], bmap]
    tile = lambda d: pl.BlockSpec((1, N_PAIRS, N_KEYS * d, LANES), lambda s, d=d: (s // ppb, 0, (s % ppb) // d, 0))
    slab_shape = lambda dtype: jax.ShapeDtypeStruct((nb, N_PAIRS, seq, LANES), dtype)
    out_specs += [tile(d) for d in dils] + [tile(d) for d in dils[1:]]
    out_shape += [slab_shape(jnp.bfloat16)] * len(dils) + [slab_shape(jnp.float32)] * (len(dils) - 1)
    n_rows = 2 * t * N_PAIRS
    tile_scr = pltpu.VMEM((N_PAIRS, 2 * N_KEYS, N_KEYS), jnp.float32)
    res = pl.pallas_call(
        functools.partial(_decode_band_kernel, dec_seq=t, head_cols=head_cols, sink_idx=sink_idx, dils=dils,
                          blocks_per_batch=ppb),
        grid=(n,),
        in_specs=in_specs,
        out_specs=out_specs,
        out_shape=out_shape,
        scratch_shapes=[pltpu.VMEM((n_rows, ln), jnp.float32) for ln in cache_lens]
        + [pltpu.VMEM((len(dils), n_rows, LANES), jnp.float32), pltpu.VMEM((n_rows, LANES), jnp.float32)]
        + [tile_scr] * (len(dils) + 1)
        + [pltpu.VMEM((d, N_KEYS, k.shape[3]), jnp.bfloat16) for d, k in zip(dils, ks)] * 2
        + [pltpu.VMEM((N_PAIRS, N_KEYS * d, LANES), jnp.float32) for d in dils if d > 1],
        compiler_params=pltpu.CompilerParams(
            dimension_semantics=("arbitrary",), vmem_limit_bytes=VMEM_LIMIT),
        name="decode_band_attn",
    )(q_dec, *news, *caches, *maps, map_new, rel_bias, sinks, *band_args)
    ng = len(dils)
    return res[0], res[1], res[2:2 + ng], res[2 + ng:]


def _out_kernel(*refs, n_groups):
    x_ref, ng_ref, wg_ref, oa_ref = refs[0:4]
    ob_refs = refs[4:4 + n_groups]
    lse_refs = refs[4 + n_groups:4 + 2 * n_groups] if n_groups > 1 else ()
    wa_ref, wb_ref, wo_ref, y_ref = refs[-4:]
    x = x_ref[0]
    h = _rmsnorm_bf16(x, ng_ref[...])

    def gate(c0, width, silu):
        a = jnp.dot(h, wg_ref[:, c0:c0 + width], preferred_element_type=jnp.float32)
        sg = 1.0 / (1.0 + jnp.exp(-a))
        return a * sg if silu else sg

    def slabs(ref):
        return jnp.concatenate([ref[0, p].astype(jnp.float32) for p in range(N_PAIRS)], axis=1)

    m0 = A_WIDTH + B_WIDTH
    ga = gate(0, A_WIDTH, True)
    gb = gate(A_WIDTH, B_WIDTH, True)
    ma = gate(m0, D_MODEL, False)
    mb = gate(m0 + D_MODEL, D_MODEL, False)
    if n_groups > 1:
        parts = []
        for p in range(N_PAIRS):
            lses = [r[0, p] for r in lse_refs]
            mx = functools.reduce(jnp.maximum, lses)
            es = [jnp.exp(v - mx) for v in lses]
            den = functools.reduce(lambda a, b: a + b, es)
            num = functools.reduce(lambda a, b: a + b,
                                   [e * r[0, p].astype(jnp.float32) for e, r in zip(es, ob_refs)])
            parts.append(num / den)
        ob = jnp.concatenate(parts, axis=1)
    else:
        ob = slabs(ob_refs[0])
    oa = slabs(oa_ref)
    ya = jnp.dot((oa * ga).astype(jnp.bfloat16), wa_ref[...], preferred_element_type=jnp.float32)
    yb = jnp.dot((ob * gb).astype(jnp.bfloat16), wb_ref[...], preferred_element_type=jnp.float32)
    merged = (ma * ya + mb * yb).astype(jnp.bfloat16)
    y_ref[0] = x + jnp.dot(merged, wo_ref[...], preferred_element_type=jnp.float32)


def _out_proj(x3d, ng, wg, oa, obs, lses, wa, wb, wo, *, tm):
    nb, seq, _ = x3d.shape
    row = lambda b, i: (b, i, 0)
    slab = pl.BlockSpec((1, N_PAIRS, tm, LANES), lambda b, i: (b, 0, i, 0))
    const = lambda b, i: (0, 0)
    once = dict(pipeline_mode=pl.Buffered(1))
    in_specs = [pl.BlockSpec((1, tm, D_MODEL), row), pl.BlockSpec((1, D_MODEL), const),
                pl.BlockSpec(wg.shape, const, **once), slab]
    in_specs += [slab for _ in obs] + [slab for _ in lses]
    in_specs += [pl.BlockSpec(wa.shape, const, **once), pl.BlockSpec(wb.shape, const, **once),
                 pl.BlockSpec(wo.shape, const, **once)]
    return pl.pallas_call(
        functools.partial(_out_kernel, n_groups=len(obs)),
        grid=(nb, seq // tm),
        in_specs=in_specs,
        out_specs=pl.BlockSpec((1, tm, D_MODEL), row),
        out_shape=jax.ShapeDtypeStruct((nb, seq, D_MODEL), jnp.float32),
        compiler_params=pltpu.CompilerParams(
            dimension_semantics=("arbitrary", "arbitrary"), vmem_limit_bytes=VMEM_LIMIT),
        name="out_proj",
    )(x3d, ng, wg, oa, *obs, *lses, wa, wb, wo)


def _w_in_kernel(w_ref, qkv_ref, gate_ref):
    offs = np.cumsum((0, A_WIDTH, A_KV_WIDTH, A_KV_WIDTH, A_WIDTH, 3 * B_WIDTH, 3 * B_WIDTH, 3 * B_WIDTH,
                      B_WIDTH, D_MODEL, D_MODEL))

    def copy(out_ref, dst, seg, perm):
        src, width = int(offs[seg]), int(offs[seg + 1] - offs[seg])
        if perm:
            for j, h in enumerate(A_HEAD_ORDER):
                out_ref[:, dst + j * HEAD_DIM:dst + (j + 1) * HEAD_DIM] = (
                    w_ref[0, :, src + h * HEAD_DIM:src + (h + 1) * HEAD_DIM].astype(out_ref.dtype))
        else:
            out_ref[:, dst:dst + width] = w_ref[0, :, src:src + width].astype(out_ref.dtype)
        return dst + width

    dst = 0
    for seg, perm in ((0, True), (4, False), (1, False), (5, False), (2, False), (6, False)):
        dst = copy(qkv_ref, dst, seg, perm)
    dst = 0
    for seg, perm in ((3, True), (7, False), (8, False), (9, False)):
        dst = copy(gate_ref, dst, seg, perm)


def _prep_w_in(w_in):
    rows = LANES
    return pl.pallas_call(
        _w_in_kernel,
        grid=(D_MODEL // rows,),
        in_specs=[pl.BlockSpec((1, rows, w_in.shape[2]), lambda i: (0, i, 0))],
        out_specs=(pl.BlockSpec((rows, QKV_COLS), lambda i: (i, 0)), pl.BlockSpec((rows, G_COLS), lambda i: (i, 0))),
        out_shape=(jax.ShapeDtypeStruct((D_MODEL, QKV_COLS), jnp.bfloat16),
                   jax.ShapeDtypeStruct((D_MODEL, G_COLS), jnp.bfloat16)),
        compiler_params=pltpu.CompilerParams(dimension_semantics=("arbitrary",), vmem_limit_bytes=VMEM_LIMIT),
        name="prep_w_in",
    )(w_in)


def _prep_params(w_in, q_gain_a, k_gain_a, q_gain_b, k_gain_b, w_up_a, w_up_b):
    w_qkv, w_gate = _prep_w_in(w_in)
    gq = jnp.broadcast_to(jnp.concatenate([q_gain_a[None], q_gain_b], axis=0)[:, None, :] * Q_SCALE,
                          (1 + N_B_GROUPS, A_Q_HEADS, HEAD_DIM)).reshape(1, Q_COLS)
    gk = jnp.concatenate([jnp.broadcast_to(k_gain_a, (A_KV_HEADS, HEAD_DIM)).reshape(A_KV_WIDTH),
                          jnp.broadcast_to(k_gain_b[:, None, :], (N_B_GROUPS, B_HEADS, HEAD_DIM)).reshape(-1)])[None]
    hd = np.arange(MXU_COLS) // HEAD_DIM
    ones = jnp.asarray((hd[:, None] == hd[None, :]).astype(np.float32) / HEAD_DIM, jnp.bfloat16)
    wa = jnp.concatenate([w_up_a[h * HEAD_DIM:(h + 1) * HEAD_DIM] for h in A_HEAD_ORDER], axis=0)
    return w_qkv, w_gate, gq, gk, ones, wa.astype(jnp.bfloat16), w_up_b.astype(jnp.bfloat16)


def kernel(x_prompt, x_sample, cache_a_kv, cache_b1_kv, cache_b2_kv, cache_b3_kv, rel_bias, norm_gain, w_in,
           q_gain_a, k_gain_a, sinks_a, q_gain_b, k_gain_b, w_up_a, w_up_b, w_out):
    assert norm_gain.shape[0] == 1, "single layer"
    nb, seq, _ = x_prompt.shape
    n_dec, dec_seq, _ = x_sample.shape
    w_qkv, w_gate, gq, gk, ones, wa, wb = _prep_params(
        w_in, q_gain_a[0], k_gain_a[0], q_gain_b[0], k_gain_b[0], w_up_a[0], w_up_b[0])
    wo = w_out[0].astype(jnp.bfloat16)
    ng = norm_gain
    windows = (A_WINDOW,) + tuple(w for w, _ in B_GROUPS)
    dils = (1,) + tuple(d for _, d in B_GROUPS)
    head_cols = (A_HEAD_ORDER,) + tuple(tuple(range(A_Q_HEADS + g * B_HEADS, A_Q_HEADS + (g + 1) * B_HEADS))
                                        for g in range(N_B_GROUPS))

    p_rows = tuple(min(w, seq) for w in windows)
    qs, ks, vs, states = _qkv_proj(x_prompt.reshape(nb * seq, D_MODEL), ng, w_qkv, gq, gk, ones,
                                   nb=nb, seq=seq, state_rows=p_rows, dils=dils, tm=min(seq, PROJ_TILE_ROWS),
                                   states_t=True)
    t_dec = n_dec * dec_seq
    qd, _, _, news = _qkv_proj(x_sample.reshape(t_dec, D_MODEL), ng, w_qkv, gq, gk, ones,
                               nb=1, seq=t_dec, state_rows=(t_dec,) * 4, dils=(1, 1, 1, 1),
                               tm=min(t_dec, PROJ_TILE_ROWS),
                               states_t=False)

    q_dec = jnp.concatenate([a.reshape(n_dec, dec_seq, A_WIDTH) for a in qd], axis=2)
    caches = [jnp.transpose(c[0], (0, 2, 3, 4, 1)) for c in (cache_a_kv, cache_b1_kv, cache_b2_kv, cache_b3_kv)]
    news3 = [a.reshape(n_dec, dec_seq, a.shape[1]) for a in news]
    oa_s, ob_s, o_prompt, lses = _decode_band_attention(
        q_dec, news3, caches, qs, ks, vs, rel_bias, sinks_a, dils=dils, head_cols=head_cols, sink_idx=A_HEAD_ORDER)

    y_prompt = _out_proj(x_prompt, ng, w_gate, o_prompt[0], list(o_prompt[1:]), list(lses), wa, wb, wo,
                         tm=min(seq, PROJ_TILE_ROWS))
    heads = (A_KV_HEADS, B_HEADS, B_HEADS, B_HEADS)
    new_prompt = tuple(jnp.transpose(states[g].reshape(1, nb, 2, heads[g], HEAD_DIM, p_rows[g]), (0, 1, 5, 2, 3, 4))
                       for g in range(4))
    to_slabs = lambda o: jnp.transpose(o, (1, 0, 2, 3)).reshape(1, N_PAIRS, t_dec, LANES)
    y_sample = _out_proj(x_sample.reshape(1, t_dec, D_MODEL), ng, w_gate, to_slabs(oa_s), [to_slabs(ob_s)], [],
                         wa, wb, wo, tm=min(t_dec, PROJ_TILE_ROWS))
    y_sample = y_sample.reshape(n_dec, dec_seq, D_MODEL)
    new_sample = tuple(news3[g].reshape(1, n_dec, dec_seq, 2, heads[g], HEAD_DIM) for g in range(4))
    return (y_prompt, y_sample) + new_prompt + new_sample
```

```python
import functools
import math

import numpy as np
import jax
import jax.numpy as jnp
from jax import lax
from jax.experimental import pallas as pl
from jax.experimental.pallas import tpu as pltpu

D_MODEL = 1024
HEAD_DIM = 64
A_Q_HEADS = 8
A_KV_HEADS = 2
A_WINDOW = 128
B_GROUPS = ((128, 1), (512, 4), (2048, 16))
N_B_GROUPS = 3
B_HEADS = 8
N_KEYS = 128
A_WIDTH = A_Q_HEADS * HEAD_DIM
A_KV_WIDTH = A_KV_HEADS * HEAD_DIM
B_WIDTH = B_HEADS * HEAD_DIM
N_BUCKETS = 32
MAX_DISTANCE = 2048
EPS = 1e-6
NEG_INF = -1e30
Q_SCALE = HEAD_DIM ** -0.5

LANES = 128
MXU_COLS = 256
VMEM_LIMIT = 56 * 1024 * 1024
PROJ_TILE_ROWS = 512
N_PAIRS = A_WIDTH // LANES

A_HEAD_ORDER = (0, 4, 1, 5, 2, 6, 3, 7)

Q_COLS = A_WIDTH + N_B_GROUPS * B_WIDTH
KV_COLS = A_KV_WIDTH + N_B_GROUPS * B_WIDTH
QKV_COLS = Q_COLS + 2 * KV_COLS
G_COLS = A_WIDTH + B_WIDTH + 2 * D_MODEL
KV_WIDTHS = (A_KV_WIDTH, B_WIDTH, B_WIDTH, B_WIDTH)
NORM_BATCH = 4 * B_WIDTH // MXU_COLS


def _t5_bucket_np(dist):
    max_exact = N_BUCKETS // 2
    d = np.maximum(dist, 0)
    df = np.maximum(d, 1).astype(np.float32)
    large = max_exact + (np.log(df / np.float32(max_exact)) / np.float32(math.log(MAX_DISTANCE / max_exact))
                         * np.float32(N_BUCKETS - max_exact)).astype(np.int32)
    large = np.minimum(large, N_BUCKETS - 1)
    return np.where(d < max_exact, d, large)


def _rmsnorm_bf16(x, gain):
    ms = jnp.mean(x * x, axis=-1, keepdims=True)
    return (x * lax.rsqrt(ms + EPS) * gain).astype(jnp.bfloat16)


def _state_plan(n_rows, seq, tm):
    tpb = seq // tm
    r = min(n_rows, tm)
    nblk = max(n_rows // tm, 1)
    return tpb, r, nblk, tpb - nblk


def _qkv_kernel(x_ref, ng_ref, w_ref, gq_ref, gk_ref, ones_ref, *rest, tm, seq, state_rows, dils, states_t):
    q_refs = rest[0:4]
    k_refs = rest[4:8]
    v_refs = rest[8:12]
    state_refs = rest[12:16]
    tmp_ref, a_scr, sq_scr, ss_scr = rest[16:20]
    h = _rmsnorm_bf16(x_ref[...], ng_ref[...])
    ones = ones_ref[...]

    def proj(col0, w):
        return jnp.dot(h, w_ref[:, col0:col0 + w], preferred_element_type=jnp.float32)

    def emit(a, c, out_ref, d, state):
        w = a.shape[1]
        if d == 1:
            out_ref[0, 0, :, c:c + w] = a.astype(out_ref.dtype)
        else:
            for s in range(w // LANES):
                tmp_ref[c // LANES + s] = a[:, s * LANES:(s + 1) * LANES]
        if state is not None:
            sref, kv, r = state
            if states_t:
                sref[0, kv, c:c + w, :] = a[tm - r:, :].T
            else:
                wd = sref.shape[1] // 2
                sref[:, kv * wd + c:kv * wd + c + w] = a[tm - r:, :]

    def finish(out_ref, d, width):
        if d > 1:
            for cls in range(d):
                for s in range(width // LANES):
                    out_ref[0, cls, :, s * LANES:(s + 1) * LANES] = (
                        tmp_ref[s, pl.ds(cls, tm // d, stride=d), :].astype(out_ref.dtype))

    def normed_batch(arrays):
        chunks = [(ai, c) for ai in range(len(arrays)) for c in range(0, B_WIDTH, MXU_COLS)]
        for ci, (ai, c) in enumerate(chunks):
            a = proj(arrays[ai][0] + c, MXU_COLS)
            a_scr[ci] = a
            sq_scr[ci * tm:(ci + 1) * tm, :] = (a * a).astype(jnp.bfloat16)
        n = len(chunks) * tm
        ss_scr[0:n, :] = jnp.dot(sq_scr[0:n, :], ones, preferred_element_type=jnp.float32)
        for ci, (ai, c) in enumerate(chunks):
            _, gain_ref, gcol0, out_ref, d, state = arrays[ai]
            o = (a_scr[ci] * lax.rsqrt(ss_scr[ci * tm:(ci + 1) * tm, :] + EPS)
                 * gain_ref[:, gcol0 + c:gcol0 + c + MXU_COLS])
            emit(o, c, out_ref, d, state)
            if c + MXU_COLS == B_WIDTH:
                finish(out_ref, d, B_WIDTH)

    q_off = (0, A_WIDTH, A_WIDTH + B_WIDTH, A_WIDTH + 2 * B_WIDTH)
    kv_off = (0, A_KV_WIDTH, A_KV_WIDTH + B_WIDTH, A_KV_WIDTH + 2 * B_WIDTH)
    plan = [_state_plan(state_rows[g], seq, tm)[1] for g in range(4)]
    normed_batch([(q_off[g], gq_ref, q_off[g], q_refs[g], dils[g], None) for g in range(4)])
    normed_batch([(Q_COLS + kv_off[g], gk_ref, kv_off[g], k_refs[g], dils[g], (state_refs[g], 0, plan[g]))
                  for g in range(1, 4)])
    a = proj(Q_COLS, A_KV_WIDTH)
    ss = jnp.dot((a * a).astype(jnp.bfloat16), ones[:A_KV_WIDTH, :A_KV_WIDTH], preferred_element_type=jnp.float32)
    emit(a * lax.rsqrt(ss + EPS) * gk_ref[:, 0:A_KV_WIDTH], 0, k_refs[0], dils[0], (state_refs[0], 0, plan[0]))
    for g in range(4):
        wd = KV_WIDTHS[g]
        for c in range(0, wd, MXU_COLS):
            w = min(MXU_COLS, wd - c)
            emit(proj(Q_COLS + KV_COLS + kv_off[g] + c, w), c, v_refs[g], dils[g], (state_refs[g], 1, plan[g]))
        finish(v_refs[g], dils[g], wd)


def _qkv_proj(x2d, ng, w_qkv, gq, gk, ones, *, nb, seq, state_rows, dils, tm, states_t):
    t = x2d.shape[0]
    tpb = seq // tm

    def cls_shape(d, width):
        return jax.ShapeDtypeStruct((nb, d, seq // d, width), jnp.bfloat16)

    def cls_spec(d, width):
        return pl.BlockSpec((1, d, tm // d, width), lambda i: (i // tpb, 0, i % tpb, 0))

    def state_shape(g):
        if states_t:
            return jax.ShapeDtypeStruct((nb, 2, KV_WIDTHS[g], state_rows[g]), jnp.float32)
        return jax.ShapeDtypeStruct((nb * state_rows[g], 2 * KV_WIDTHS[g]), jnp.float32)

    def state_spec(g):
        _, r, nblk, j0 = _state_plan(state_rows[g], seq, tm)
        if states_t:
            return pl.BlockSpec((1, 2, KV_WIDTHS[g], r), lambda i: (i // tpb, 0, 0, jnp.maximum(i % tpb - j0, 0)))
        return pl.BlockSpec((r, 2 * KV_WIDTHS[g]),
                            lambda i: ((i // tpb) * nblk + jnp.maximum(i % tpb - j0, 0), 0))

    const = lambda i: (0, 0)
    out_shape = ([cls_shape(d, A_WIDTH) for d in dils]
                 + [cls_shape(d, w) for d, w in zip(dils, KV_WIDTHS)] * 2
                 + [state_shape(g) for g in range(4)])
    out_specs = ([cls_spec(d, A_WIDTH) for d in dils]
                 + [cls_spec(d, w) for d, w in zip(dils, KV_WIDTHS)] * 2
                 + [state_spec(g) for g in range(4)])
    res = pl.pallas_call(
        functools.partial(_qkv_kernel, tm=tm, seq=seq, state_rows=state_rows, dils=dils, states_t=states_t),
        grid=(t // tm,),
        in_specs=[
            pl.BlockSpec((tm, D_MODEL), lambda i: (i, 0)),
            pl.BlockSpec((1, D_MODEL), const),
            pl.BlockSpec((D_MODEL, QKV_COLS), const, pipeline_mode=pl.Buffered(1)),
            pl.BlockSpec((1, Q_COLS), const),
            pl.BlockSpec((1, KV_COLS), const),
            pl.BlockSpec((MXU_COLS, MXU_COLS), const),
        ],
        out_specs=out_specs,
        out_shape=out_shape,
        scratch_shapes=[pltpu.VMEM((N_PAIRS, tm, LANES), jnp.float32),
                        pltpu.VMEM((NORM_BATCH, tm, MXU_COLS), jnp.float32),
                        pltpu.VMEM((NORM_BATCH * tm, MXU_COLS), jnp.bfloat16),
                        pltpu.VMEM((NORM_BATCH * tm, MXU_COLS), jnp.float32)],
        compiler_params=pltpu.CompilerParams(
            dimension_semantics=("arbitrary",), vmem_limit_bytes=VMEM_LIMIT),
        name="qkv_proj",
    )(x2d, ng, w_qkv, gq, gk, ones)
    return res[0:4], res[4:8], res[8:12], res[12:16]


def _lookup_rows(bucket_map, top, table_ref, col_top, col_bottom, fill):
    acc = jnp.full(bucket_map.shape, fill, jnp.float32)
    for b in range(N_BUCKETS):
        val = jnp.where(top, table_ref[b, col_top], table_ref[b, col_bottom])
        acc = jnp.where(bucket_map == b, val, acc)
    return acc


def _band_block(q_ref, k_ref, kp_ref, v_ref, vp_ref, o_ref, o_stage, lse_ref, bias_ref, sink_ref, *,
                cls, first, dil, shared_kv):
    blk = N_KEYS
    row = lax.broadcasted_iota(jnp.int32, (2 * blk, blk), 0)
    lane = lax.broadcasted_iota(jnp.int32, (2 * blk, blk), 1)
    ahead = lane - jnp.bitwise_and(row, blk - 1)
    tri = ahead <= 0
    valid = ahead <= jnp.where(first, 0, blk)
    cur_part = tri.astype(jnp.bfloat16)
    prev_part = jnp.logical_not(tri).astype(jnp.bfloat16)
    left = lax.broadcasted_iota(jnp.int32, (blk, LANES), 1) < HEAD_DIM
    mask_l = left.astype(jnp.bfloat16)
    mask_r = jnp.logical_not(left).astype(jnp.bfloat16)
    ones_kv = jnp.ones((2 * blk, LANES), jnp.bfloat16)
    nt = (((1,), (1,)), ((), ()))
    rows = pl.ds(0, blk) if dil == 1 else pl.ds(cls, blk, stride=dil)
    for p in range(N_PAIRS):
        kcol = 0 if shared_kv else p * LANES
        qp = q_ref[0, 0, :, p * LANES:(p + 1) * LANES]
        qs = jnp.concatenate([qp * mask_l, qp * mask_r], axis=0)
        kc = jnp.concatenate([kp_ref[cls, :, kcol:kcol + LANES], k_ref[0, 0, :, kcol:kcol + LANES]], axis=0)
        vc = jnp.concatenate([vp_ref[cls, :, kcol:kcol + LANES], v_ref[0, 0, :, kcol:kcol + LANES]], axis=0)
        s2 = lax.dot_general(qs, kc, nt, preferred_element_type=jnp.float32)
        s = jnp.where(tri, s2[:, blk:], s2[:, :blk]) + bias_ref[p]
        s = jnp.where(valid, s, NEG_INF)
        m = jnp.max(s, axis=-1, keepdims=True)
        if sink_ref is not None:
            sk = sink_ref[p]
            m = jnp.maximum(m, sk)
        pb = jnp.exp(s - m).astype(jnp.bfloat16)
        p2 = jnp.concatenate([pb * prev_part, pb * cur_part], axis=1)
        ov = jnp.dot(p2, jnp.concatenate([vc, ones_kv], axis=1), preferred_element_type=jnp.float32)
        l = ov[:, LANES:]
        if sink_ref is not None:
            l = l + jnp.exp(sk - m)
        num = jnp.where(left, ov[:blk, :LANES], ov[blk:, :LANES])
        den = jnp.where(left, l[:blk], l[blk:])
        if dil == 1:
            o_ref[0, p, rows, :] = (num / den).astype(o_ref.dtype)
        else:
            o_stage[p, rows, :] = num / den
        if lse_ref is not None:
            m1 = jnp.where(left, m[:blk], m[blk:])
            lse_ref[0, p, rows, :] = m1 + jnp.log(den)
    kp_ref[cls] = k_ref[0, 0]
    vp_ref[cls] = v_ref[0, 0]


def _decode_kernel(qa_ref, qb1_ref, qb2_ref, qb3_ref, na_ref, nb1_ref, nb2_ref, nb3_ref,
                   ca_ref, cb1_ref, cb2_ref, cb3_ref, ma_ref, mb1_ref, mb2_ref, mb3_ref, mn_ref, relb_ref, sinks_ref,
                   oa_ref, ob_ref, ta_ref, tb1_ref, tb2_ref, tb3_ref, tn_ref, sink_ref,
                   *, dec_seq, head_cols, sink_idx, dils, init):
    t = dec_seq
    if init:
        for g, (m_ref, t_ref) in enumerate(zip((ma_ref, mb1_ref, mb2_ref, mb3_ref),
                                               (ta_ref, tb1_ref, tb2_ref, tb3_ref))):
            top = lax.broadcasted_iota(jnp.int32, m_ref.shape, 0) < t
            top_n = lax.broadcasted_iota(jnp.int32, (2 * t, LANES), 0) < t
            for p in range(N_PAIRS):
                cols = (head_cols[g][2 * p], head_cols[g][2 * p + 1])
                rows = slice(p * 2 * t, (p + 1) * 2 * t)
                t_ref[rows, :] = _lookup_rows(m_ref[...], top, relb_ref, cols[0], cols[1], NEG_INF)
                tn_ref[g, rows, :] = _lookup_rows(mn_ref[g], top_n, relb_ref, cols[0], cols[1], NEG_INF)
                if g == 0:
                    sink_ref[rows, :] = jnp.where(top_n, sinks_ref[0, sink_idx[2 * p]],
                                                  sinks_ref[0, sink_idx[2 * p + 1]])
        return

    left8 = lax.broadcasted_iota(jnp.int32, (2 * t, LANES), 1) < HEAD_DIM
    top8 = lax.broadcasted_iota(jnp.int32, (2 * t, LANES), 0) < t
    own = (left8 == top8).astype(jnp.float32)
    left4 = lax.broadcasted_iota(jnp.int32, (t, LANES), 1) < HEAD_DIM
    nt = (((1,), (1,)), ((), ()))

    seq_id = pl.program_id(0)

    def seq_rows(block):
        k = block.shape[0] // t
        which = seq_id % k
        rows = block[0:t]
        for j in range(1, k):
            rows = jnp.where(which == j, block[j * t:(j + 1) * t], rows)
        return rows

    q_refs = (qa_ref, qb1_ref, qb2_ref, qb3_ref)

    def q_rows(g, p):
        qp = seq_rows(q_refs[g][:, p * LANES:(p + 1) * LANES].astype(jnp.float32))
        return jnp.concatenate([qp, qp], axis=0) * own

    def attend(qrs, kts, vts, knews, vnews, tbl, tbl_new, sink, dil):
        shared = len(kts) == 1
        rows8 = lambda a, u: jnp.broadcast_to(a[u:u + 1, :], (2 * t, LANES))
        qr = jnp.concatenate(qrs, axis=0)
        qb = qr.astype(jnp.bfloat16)
        if shared:
            s_c = jnp.dot(qb, kts[0], preferred_element_type=jnp.float32)
        else:
            s_c = jnp.concatenate([jnp.dot(qb[2 * t * p:2 * t * (p + 1)], kts[p], preferred_element_type=jnp.float32)
                                   for p in range(N_PAIRS)], axis=0)
        yield
        s_c = s_c + tbl
        kn = [jnp.concatenate([rows8(knews[0 if shared else p], u) for p in range(N_PAIRS)], axis=0)
              for u in range(t)]
        vn = [jnp.concatenate([rows8(vnews[0 if shared else p], u) for p in range(N_PAIRS)], axis=0)
              for u in range(t)]
        s_n = [jnp.sum(qr * kn[u], axis=-1, keepdims=True) + tbl_new[:, u:u + 1] for u in range(t)]
        m = functools.reduce(jnp.maximum, s_n, jnp.max(s_c, axis=-1, keepdims=True))
        if sink is not None:
            m = jnp.maximum(m, sink)
        pc = jnp.exp(s_c - m)
        pn = [jnp.exp(x - m) for x in s_n]
        l = functools.reduce(lambda a, b: a + b, pn, jnp.sum(pc, axis=-1, keepdims=True))
        if sink is not None:
            l = l + jnp.exp(sink - m)
        pb = pc.astype(jnp.bfloat16)
        yield
        if shared:
            o = lax.dot_general(pb, vts[0], nt, preferred_element_type=jnp.float32)
        elif dil >= t:
            lane8 = lax.broadcasted_iota(jnp.int32, (2 * t, LANES), 1)
            row8 = lax.broadcasted_iota(jnp.int32, (2 * t, LANES), 0)
            sel = (lane8 % dil == row8 % t).astype(jnp.bfloat16)
            parts = []
            for p in range(N_PAIRS):
                accs = []
                for side in range(2):
                    r0 = 2 * t * p + t * side
                    w = jnp.sum(pc[r0:r0 + t], axis=0, keepdims=True)
                    v = vts[p][side * HEAD_DIM:(side + 1) * HEAD_DIM]
                    accs.append(functools.reduce(
                        lambda a, b: a + b,
                        [v[:, c:c + LANES] * w[:, c:c + LANES] for c in range(0, v.shape[1], LANES)]))
                acc = jnp.concatenate(accs, axis=0)
                hi = acc.astype(jnp.bfloat16)
                lo = (acc - hi.astype(jnp.float32)).astype(jnp.bfloat16)
                parts.append(lax.dot_general(sel, hi, nt, preferred_element_type=jnp.float32)
                             + lax.dot_general(sel, lo, nt, preferred_element_type=jnp.float32))
            o = jnp.concatenate(parts, axis=0)
        else:
            o = jnp.concatenate([lax.dot_general(pb[2 * t * p:2 * t * (p + 1)], vts[p], nt,
                                                 preferred_element_type=jnp.float32)
                                 for p in range(N_PAIRS)], axis=0)
        for u in range(t):
            o = o + pn[u] * vn[u]
        yield o / l, m + jnp.log(l)

    def fold(x, p):
        return jnp.where(left4, x[2 * t * p:2 * t * p + t], x[2 * t * p + t:2 * t * (p + 1)])

    def cache_kv(c_ref, kv, p, dtype=jnp.bfloat16):
        return c_ref[0, kv, 2 * p:2 * p + 2].reshape(LANES, c_ref.shape[-1]).astype(dtype)

    caches = (cb1_ref, cb2_ref, cb3_ref)
    news = (nb1_ref, nb2_ref, nb3_ref)
    tbls = (tb1_ref, tb2_ref, tb3_ref)
    new_a = seq_rows(na_ref[...])
    groups = [attend([q_rows(0, p) for p in range(N_PAIRS)],
                     [cache_kv(ca_ref, 0, 0)], [cache_kv(ca_ref, 1, 0)],
                     [new_a[:, 0:LANES]], [new_a[:, LANES:2 * LANES]],
                     ta_ref[...], tn_ref[0], sink_ref[:, 0:1], dils[0])]
    for g in range(N_B_GROUPS):
        new_g = seq_rows(news[g][...])
        groups.append(attend(
            [q_rows(1 + g, p) for p in range(N_PAIRS)],
            [cache_kv(caches[g], 0, p) for p in range(N_PAIRS)],
            [cache_kv(caches[g], 1, p, jnp.float32 if dils[1 + g] >= t else jnp.bfloat16) for p in range(N_PAIRS)],
            [new_g[:, p * LANES:(p + 1) * LANES] for p in range(N_PAIRS)],
            [new_g[:, B_WIDTH + p * LANES:B_WIDTH + (p + 1) * LANES] for p in range(N_PAIRS)],
            tbls[g][...], tn_ref[1 + g], None, dils[1 + g]))
    for phase in range(2):
        for gen in groups:
            next(gen)
    results = [next(gen) for gen in groups]
    oa = results[0][0]
    for p in range(N_PAIRS):
        oa_ref[0, p] = fold(oa, p).astype(oa_ref.dtype)
    outs = [o for o, _ in results[1:]]
    lses = [jnp.broadcast_to(lse, o.shape) for o, lse in results[1:]]
    mx = jnp.maximum(jnp.maximum(lses[0], lses[1]), lses[2])
    es = [jnp.exp(x - mx) for x in lses]
    den = es[0] + es[1] + es[2]
    comb = (es[0] * outs[0] + es[1] * outs[1] + es[2] * outs[2]) / den
    for p in range(N_PAIRS):
        ob_ref[0, p] = fold(comb, p).astype(ob_ref.dtype)


def _decode_bucket_maps(dils, cache_lens, dec_seq):
    t_idx = np.arange(dec_seq)
    maps = []
    new = np.full((len(dils), 2 * dec_seq, LANES), -1, np.int32)
    for g, (d, ln) in enumerate(zip(dils, cache_lens)):
        back = ln + t_idx[:, None] - np.arange(ln)[None, :]
        ok = (back % d == 0) & (back // d >= 1) & (back // d < N_KEYS)
        maps.append(jnp.asarray(np.tile(np.where(ok, _t5_bucket_np(back), -1), (2, 1)).astype(np.int32)))
        backn = t_idx[:, None] - t_idx[None, :]
        okn = (backn >= 0) & (backn % d == 0) & (backn // d < N_KEYS)
        new[g, :, :dec_seq] = np.tile(np.where(okn, _t5_bucket_np(backn), -1), (2, 1))
    return maps, jnp.asarray(new)


def _decode_band_kernel(*refs, dec_seq, head_cols, sink_idx, dils, blocks_per_batch):
    n_g = len(dils)
    n_dec_in = 4 * n_g + 3
    dec_in = refs[0:n_dec_in]
    relb_ref, sinks_ref = dec_in[-2:]
    band_in = [refs[n_dec_in + 4 * g:n_dec_in + 4 * (g + 1)] for g in range(n_g)]
    outs = refs[n_dec_in + 4 * n_g:n_dec_in + 4 * n_g + 2 + 2 * n_g - 1]
    oa_ref, ob_ref = outs[0:2]
    o_refs = outs[2:2 + n_g]
    lse_refs = (None,) + tuple(outs[2 + n_g:])
    scratch = refs[n_dec_in + 4 * n_g + 2 + 2 * n_g - 1:]
    dec_scratch = scratch[0:6]
    bias_refs = scratch[6:6 + n_g]
    sink_tile = scratch[6 + n_g]
    kprev_refs = scratch[7 + n_g:7 + 2 * n_g]
    vprev_refs = scratch[7 + 2 * n_g:7 + 3 * n_g]
    stages = iter(scratch[7 + 3 * n_g:])
    stage_refs = [None if d == 1 else next(stages) for d in dils]
    blk = N_KEYS
    s = pl.program_id(0)
    decode = functools.partial(_decode_kernel, *dec_in, oa_ref, ob_ref, *dec_scratch, dec_seq=dec_seq,
                               head_cols=head_cols, sink_idx=sink_idx, dils=dils)

    @pl.when(s == 0)
    def _():
        top = lax.broadcasted_iota(jnp.int32, (2 * blk, blk), 0) < blk
        for g in range(n_g):
            bmap = band_in[g][3][...]
            for p in range(N_PAIRS):
                bias_refs[g][p] = _lookup_rows(bmap, top, relb_ref, head_cols[g][2 * p], head_cols[g][2 * p + 1],
                                               0.0)
            kprev_refs[g][...] = jnp.zeros(kprev_refs[g].shape, kprev_refs[g].dtype)
            vprev_refs[g][...] = jnp.zeros(vprev_refs[g].shape, vprev_refs[g].dtype)
        for p in range(N_PAIRS):
            sink_tile[p] = jnp.where(top, sinks_ref[0, sink_idx[2 * p]], sinks_ref[0, sink_idx[2 * p + 1]])
        decode(init=True)

    r = s % blocks_per_batch

    decode(init=False)
    for g in range(n_g):
        q_ref, k_ref, v_ref, _ = band_in[g]
        _band_block(q_ref, k_ref, kprev_refs[g], v_ref, vprev_refs[g], o_refs[g], stage_refs[g], lse_refs[g],
                    bias_refs[g], sink_tile if g == 0 else None, cls=r % dils[g], first=(r // dils[g]) == 0,
                    dil=dils[g], shared_kv=(g == 0))

    for g in range(n_g):
        if dils[g] > 1:
            @pl.when(r % dils[g] == dils[g] - 1)
            def _():
                for p in range(N_PAIRS):
                    o_refs[g][0, p] = stage_refs[g][p].astype(o_refs[g].dtype)


def _decode_band_attention(q_dec, news, caches, qs, ks, vs, rel_bias, sinks, *, dec_seq, dils, head_cols,
                           sink_idx):
    t = dec_seq
    n = q_dec[0].shape[0] // t
    nb = qs[0].shape[0]
    seq = qs[0].shape[1] * qs[0].shape[2]
    ppb = seq // N_KEYS
    assert n == nb * ppb, "one prompt block of every group per sample sequence"
    seq3 = lambda i: (i, 0, 0)
    seq4 = lambda i: (i, 0, 0, 0)
    seq5 = lambda i: (i, 0, 0, 0, 0)
    cache_lens = tuple(c.shape[-1] for c in caches)
    maps, map_new = _decode_bucket_maps(dils, cache_lens, t)
    smem = functools.partial(pl.BlockSpec, memory_space=pltpu.SMEM)
    q_rows, new_rows = 16, 8
    in_specs = [pl.BlockSpec((q_rows, a.shape[1]), lambda i: (i // (q_rows // t), 0)) for a in q_dec]
    in_specs += [pl.BlockSpec((new_rows, a.shape[1]), lambda i: (i // (new_rows // t), 0)) for a in news]
    in_specs += [pl.BlockSpec((1,) + c.shape[1:], seq5) for c in caches]
    in_specs += [pl.BlockSpec(m.shape, lambda i: (0, 0)) for m in maps]
    in_specs += [pl.BlockSpec(map_new.shape, lambda i: (0, 0, 0)), smem(), smem()]
    band_args = []
    out_specs = [pl.BlockSpec((1, N_PAIRS, t, LANES), seq4)] * 2
    out_shape = [jax.ShapeDtypeStruct((n, N_PAIRS, t, LANES), jnp.float32)] * 2
    qi = np.arange(N_KEYS)[:, None]
    ci = np.arange(N_KEYS)[None, :]
    for g, d in enumerate(dils):
        cur = lambda s, d=d: (s // ppb, (s % ppb) % d, (s % ppb) // d, 0)
        wq, wkv = qs[g].shape[3], ks[g].shape[3]
        bmap = jnp.asarray(np.tile(_t5_bucket_np(((qi - ci) % N_KEYS) * d), (2, 1)).astype(np.int32))
        in_specs += [pl.BlockSpec((1, 1, N_KEYS, wq), cur), pl.BlockSpec((1, 1, N_KEYS, wkv), cur),
                     pl.BlockSpec((1, 1, N_KEYS, wkv), cur), pl.BlockSpec(bmap.shape, lambda s: (0, 0))]
        band_args += [qs[g], ks[g], vs[g], bmap]
    tile = lambda d: pl.BlockSpec((1, N_PAIRS, N_KEYS * d, LANES), lambda s, d=d: (s // ppb, 0, (s % ppb) // d, 0))
    slab_shape = lambda dtype: jax.ShapeDtypeStruct((nb, N_PAIRS, seq, LANES), dtype)
    out_specs += [tile(d) for d in dils] + [tile(d) for d in dils[1:]]
    out_shape += [slab_shape(jnp.bfloat16)] * len(dils) + [slab_shape(jnp.float32)] * (len(dils) - 1)
    n_rows = 2 * t * N_PAIRS
    tile_scr = pltpu.VMEM((N_PAIRS, 2 * N_KEYS, N_KEYS), jnp.float32)
    res = pl.pallas_call(
        functools.partial(_decode_band_kernel, dec_seq=t, head_cols=head_cols, sink_idx=sink_idx, dils=dils,
                          blocks_per_batch=ppb),
        grid=(n,),
        in_specs=in_specs,
        out_specs=out_specs,
        out_shape=out_shape,
        scratch_shapes=[pltpu.VMEM((n_rows, ln), jnp.float32) for ln in cache_lens]
        + [pltpu.VMEM((len(dils), n_rows, LANES), jnp.float32), pltpu.VMEM((n_rows, LANES), jnp.float32)]
        + [tile_scr] * (len(dils) + 1)
        + [pltpu.VMEM((d, N_KEYS, k.shape[3]), jnp.bfloat16) for d, k in zip(dils, ks)] * 2
        + [pltpu.VMEM((N_PAIRS, N_KEYS * d, LANES), jnp.float32) for d in dils if d > 1],
        compiler_params=pltpu.CompilerParams(
            dimension_semantics=("arbitrary",), vmem_limit_bytes=VMEM_LIMIT),
        name="decode_band_attn",
    )(*q_dec, *news, *caches, *maps, map_new, rel_bias, sinks, *band_args)
    ng = len(dils)
    return res[0], res[1], res[2:2 + ng], res[2 + ng:]


def _out_kernel(*refs, n_groups):
    x_ref, ng_ref, wg_ref, oa_ref = refs[0:4]
    ob_refs = refs[4:4 + n_groups]
    lse_refs = refs[4 + n_groups:4 + 2 * n_groups] if n_groups > 1 else ()
    wa_ref, wb_ref, wo_ref, y_ref = refs[-4:]
    x = x_ref[0]
    h = _rmsnorm_bf16(x, ng_ref[...])

    def gate(c0, width, silu):
        a = jnp.dot(h, wg_ref[:, c0:c0 + width], preferred_element_type=jnp.float32)
        sg = 1.0 / (1.0 + jnp.exp(-a))
        return a * sg if silu else sg

    def slabs(ref):
        return jnp.concatenate([ref[0, p].astype(jnp.float32) for p in range(N_PAIRS)], axis=1)

    m0 = A_WIDTH + B_WIDTH
    ga = gate(0, A_WIDTH, True)
    gb = gate(A_WIDTH, B_WIDTH, True)
    ma = gate(m0, D_MODEL, False)
    mb = gate(m0 + D_MODEL, D_MODEL, False)
    if n_groups > 1:
        parts = []
        for p in range(N_PAIRS):
            lses = [r[0, p] for r in lse_refs]
            mx = functools.reduce(jnp.maximum, lses)
            es = [jnp.exp(v - mx) for v in lses]
            den = functools.reduce(lambda a, b: a + b, es)
            num = functools.reduce(lambda a, b: a + b,
                                   [e * r[0, p].astype(jnp.float32) for e, r in zip(es, ob_refs)])
            parts.append(num / den)
        ob = jnp.concatenate(parts, axis=1)
    else:
        ob = slabs(ob_refs[0])
    oa = slabs(oa_ref)
    ya = jnp.dot((oa * ga).astype(jnp.bfloat16), wa_ref[...], preferred_element_type=jnp.float32)
    yb = jnp.dot((ob * gb).astype(jnp.bfloat16), wb_ref[...], preferred_element_type=jnp.float32)
    merged = (ma * ya + mb * yb).astype(jnp.bfloat16)
    y_ref[0] = x + jnp.dot(merged, wo_ref[...], preferred_element_type=jnp.float32)


def _out_proj(x3d, ng, wg, oa, obs, lses, wa, wb, wo, *, tm):
    nb, seq, _ = x3d.shape
    row = lambda b, i: (b, i, 0)
    slab = pl.BlockSpec((1, N_PAIRS, tm, LANES), lambda b, i: (b, 0, i, 0))
    const = lambda b, i: (0, 0)
    once = dict(pipeline_mode=pl.Buffered(1))
    in_specs = [pl.BlockSpec((1, tm, D_MODEL), row), pl.BlockSpec((1, D_MODEL), const),
                pl.BlockSpec(wg.shape, const, **once), slab]
    in_specs += [slab for _ in obs] + [slab for _ in lses]
    in_specs += [pl.BlockSpec(wa.shape, const, **once), pl.BlockSpec(wb.shape, const, **once),
                 pl.BlockSpec(wo.shape, const, **once)]
    return pl.pallas_call(
        functools.partial(_out_kernel, n_groups=len(obs)),
        grid=(nb, seq // tm),
        in_specs=in_specs,
        out_specs=pl.BlockSpec((1, tm, D_MODEL), row),
        out_shape=jax.ShapeDtypeStruct((nb, seq, D_MODEL), jnp.float32),
        compiler_params=pltpu.CompilerParams(
            dimension_semantics=("arbitrary", "arbitrary"), vmem_limit_bytes=VMEM_LIMIT),
        name="out_proj",
    )(x3d, ng, wg, oa, *obs, *lses, wa, wb, wo)


def _w_in_kernel(w_ref, qkv_ref, gate_ref):
    offs = np.cumsum((0, A_WIDTH, A_KV_WIDTH, A_KV_WIDTH, A_WIDTH, 3 * B_WIDTH, 3 * B_WIDTH, 3 * B_WIDTH,
                      B_WIDTH, D_MODEL, D_MODEL))

    def copy(out_ref, dst, seg, perm):
        src, width = int(offs[seg]), int(offs[seg + 1] - offs[seg])
        if perm:
            for j, h in enumerate(A_HEAD_ORDER):
                out_ref[:, dst + j * HEAD_DIM:dst + (j + 1) * HEAD_DIM] = (
                    w_ref[0, :, src + h * HEAD_DIM:src + (h + 1) * HEAD_DIM].astype(out_ref.dtype))
        else:
            out_ref[:, dst:dst + width] = w_ref[0, :, src:src + width].astype(out_ref.dtype)
        return dst + width

    dst = 0
    for seg, perm in ((0, True), (4, False), (1, False), (5, False), (2, False), (6, False)):
        dst = copy(qkv_ref, dst, seg, perm)
    dst = 0
    for seg, perm in ((3, True), (7, False), (8, False), (9, False)):
        dst = copy(gate_ref, dst, seg, perm)


def _prep_w_in(w_in):
    rows = LANES
    return pl.pallas_call(
        _w_in_kernel,
        grid=(D_MODEL // rows,),
        in_specs=[pl.BlockSpec((1, rows, w_in.shape[2]), lambda i: (0, i, 0))],
        out_specs=(pl.BlockSpec((rows, QKV_COLS), lambda i: (i, 0)), pl.BlockSpec((rows, G_COLS), lambda i: (i, 0))),
        out_shape=(jax.ShapeDtypeStruct((D_MODEL, QKV_COLS), jnp.bfloat16),
                   jax.ShapeDtypeStruct((D_MODEL, G_COLS), jnp.bfloat16)),
        compiler_params=pltpu.CompilerParams(dimension_semantics=("arbitrary",), vmem_limit_bytes=VMEM_LIMIT),
        name="prep_w_in",
    )(w_in)


def _prep_params(w_in, q_gain_a, k_gain_a, q_gain_b, k_gain_b, w_up_a, w_up_b):
    w_qkv, w_gate = _prep_w_in(w_in)
    gq = jnp.broadcast_to(jnp.concatenate([q_gain_a[None], q_gain_b], axis=0)[:, None, :] * Q_SCALE,
                          (1 + N_B_GROUPS, A_Q_HEADS, HEAD_DIM)).reshape(1, Q_COLS)
    gk = jnp.concatenate([jnp.broadcast_to(k_gain_a, (A_KV_HEADS, HEAD_DIM)).reshape(A_KV_WIDTH),
                          jnp.broadcast_to(k_gain_b[:, None, :], (N_B_GROUPS, B_HEADS, HEAD_DIM)).reshape(-1)])[None]
    hd = np.arange(MXU_COLS) // HEAD_DIM
    ones = jnp.asarray((hd[:, None] == hd[None, :]).astype(np.float32) / HEAD_DIM, jnp.bfloat16)
    wa = jnp.concatenate([w_up_a[h * HEAD_DIM:(h + 1) * HEAD_DIM] for h in A_HEAD_ORDER], axis=0)
    return w_qkv, w_gate, gq, gk, ones, wa.astype(jnp.bfloat16), w_up_b.astype(jnp.bfloat16)


def kernel(x_prompt, x_sample, cache_a_kv, cache_b1_kv, cache_b2_kv, cache_b3_kv, rel_bias, norm_gain, w_in,
           q_gain_a, k_gain_a, sinks_a, q_gain_b, k_gain_b, w_up_a, w_up_b, w_out):
    assert norm_gain.shape[0] == 1, "single layer"
    nb, seq, _ = x_prompt.shape
    n_dec, dec_seq, _ = x_sample.shape
    w_qkv, w_gate, gq, gk, ones, wa, wb = _prep_params(
        w_in, q_gain_a[0], k_gain_a[0], q_gain_b[0], k_gain_b[0], w_up_a[0], w_up_b[0])
    wo = w_out[0].astype(jnp.bfloat16)
    ng = norm_gain
    windows = (A_WINDOW,) + tuple(w for w, _ in B_GROUPS)
    dils = (1,) + tuple(d for _, d in B_GROUPS)
    head_cols = (A_HEAD_ORDER,) + tuple(tuple(range(A_Q_HEADS + g * B_HEADS, A_Q_HEADS + (g + 1) * B_HEADS))
                                        for g in range(N_B_GROUPS))

    p_rows = tuple(min(w, seq) for w in windows)
    qs, ks, vs, states = _qkv_proj(x_prompt.reshape(nb * seq, D_MODEL), ng, w_qkv, gq, gk, ones,
                                   nb=nb, seq=seq, state_rows=p_rows, dils=dils, tm=min(seq, PROJ_TILE_ROWS),
                                   states_t=True)
    t_dec = n_dec * dec_seq
    qd, _, _, news = _qkv_proj(x_sample.reshape(t_dec, D_MODEL), ng, w_qkv, gq, gk, ones,
                               nb=1, seq=t_dec, state_rows=(t_dec,) * 4, dils=(1, 1, 1, 1),
                               tm=min(t_dec, PROJ_TILE_ROWS),
                               states_t=False)

    q_dec = [a.reshape(t_dec, A_WIDTH) for a in qd]
    caches = [jnp.transpose(c[0], (0, 2, 3, 4, 1)) for c in (cache_a_kv, cache_b1_kv, cache_b2_kv, cache_b3_kv)]
    oa_s, ob_s, o_prompt, lses = _decode_band_attention(
        q_dec, news, caches, qs, ks, vs, rel_bias, sinks_a, dec_seq=dec_seq, dils=dils, head_cols=head_cols,
        sink_idx=A_HEAD_ORDER)

    y_prompt = _out_proj(x_prompt, ng, w_gate, o_prompt[0], list(o_prompt[1:]), list(lses), wa, wb, wo,
                         tm=min(seq, PROJ_TILE_ROWS))
    heads = (A_KV_HEADS, B_HEADS, B_HEADS, B_HEADS)
    new_prompt = tuple(jnp.transpose(states[g].reshape(1, nb, 2, heads[g], HEAD_DIM, p_rows[g]), (0, 1, 5, 2, 3, 4))
                       for g in range(4))
    to_slabs = lambda o: jnp.transpose(o, (1, 0, 2, 3)).reshape(1, N_PAIRS, t_dec, LANES)
    y_sample = _out_proj(x_sample.reshape(1, t_dec, D_MODEL), ng, w_gate, to_slabs(oa_s), [to_slabs(ob_s)], [],
                         wa, wb, wo, tm=min(t_dec, PROJ_TILE_ROWS))
    y_sample = y_sample.reshape(n_dec, dec_seq, D_MODEL)
    new_sample = tuple(news[g].reshape(1, n_dec, dec_seq, 2, heads[g], HEAD_DIM) for g in range(4))
    return (y_prompt, y_sample) + new_prompt + new_sample
```

```python
import functools
import math

import numpy as np
import jax
import jax.numpy as jnp
from jax import lax
from jax.experimental import pallas as pl
from jax.experimental.pallas import tpu as pltpu

D_MODEL = 1024
HEAD_DIM = 64
A_Q_HEADS = 8
A_KV_HEADS = 2
A_WINDOW = 128
B_GROUPS = ((128, 1), (512, 4), (2048, 16))
N_B_GROUPS = 3
B_HEADS = 8
N_KEYS = 128
A_WIDTH = A_Q_HEADS * HEAD_DIM
A_KV_WIDTH = A_KV_HEADS * HEAD_DIM
B_WIDTH = B_HEADS * HEAD_DIM
N_BUCKETS = 32
MAX_DISTANCE = 2048
EPS = 1e-6
NEG_INF = -1e30
Q_SCALE = HEAD_DIM ** -0.5

LANES = 128
MXU_COLS = 256
VMEM_LIMIT = 56 * 1024 * 1024
PROJ_TILE_ROWS = 512
N_PAIRS = A_WIDTH // LANES

A_HEAD_ORDER = (0, 4, 1, 5, 2, 6, 3, 7)

Q_COLS = A_WIDTH + N_B_GROUPS * B_WIDTH
KV_COLS = A_KV_WIDTH + N_B_GROUPS * B_WIDTH
QKV_COLS = Q_COLS + 2 * KV_COLS
G_COLS = A_WIDTH + B_WIDTH + 2 * D_MODEL
KV_WIDTHS = (A_KV_WIDTH, B_WIDTH, B_WIDTH, B_WIDTH)
NORM_BATCH = 4 * B_WIDTH // MXU_COLS


def _t5_bucket_np(dist):
    max_exact = N_BUCKETS // 2
    d = np.maximum(dist, 0)
    df = np.maximum(d, 1).astype(np.float32)
    large = max_exact + (np.log(df / np.float32(max_exact)) / np.float32(math.log(MAX_DISTANCE / max_exact))
                         * np.float32(N_BUCKETS - max_exact)).astype(np.int32)
    large = np.minimum(large, N_BUCKETS - 1)
    return np.where(d < max_exact, d, large)


def _rmsnorm_bf16(x, gain):
    ms = jnp.mean(x * x, axis=-1, keepdims=True)
    return (x * lax.rsqrt(ms + EPS) * gain).astype(jnp.bfloat16)


def _state_plan(n_rows, seq, tm):
    tpb = seq // tm
    r = min(n_rows, tm)
    nblk = max(n_rows // tm, 1)
    return tpb, r, nblk, tpb - nblk


def _qkv_kernel(x_ref, ng_ref, w_ref, gq_ref, gk_ref, ones_ref, *rest, tm, seq, state_rows, dils, states_t):
    qkv_refs = rest[0:4]
    state_refs = rest[4:8]
    tmp_ref, a_scr, sq_scr, ss_scr = rest[8:12]
    h = _rmsnorm_bf16(x_ref[...], ng_ref[...])
    ones = ones_ref[...]

    def proj(col0, w):
        return jnp.dot(h, w_ref[:, col0:col0 + w], preferred_element_type=jnp.float32)

    def emit(a, c, out, d, state):
        w = a.shape[1]
        out_ref, base = out
        if d == 1:
            out_ref[0, 0, :, base + c:base + c + w] = a.astype(out_ref.dtype)
        else:
            for s in range(w // LANES):
                tmp_ref[c // LANES + s] = a[:, s * LANES:(s + 1) * LANES]
        if state is not None:
            sref, kv, r = state
            if states_t:
                sref[0, kv, c:c + w, :] = a[tm - r:, :].T
            else:
                wd = sref.shape[1] // 2
                sref[:, kv * wd + c:kv * wd + c + w] = a[tm - r:, :]

    def finish(out, d, width):
        out_ref, base = out
        if d > 1:
            for cls in range(d):
                for s in range(width // LANES):
                    out_ref[0, cls, :, base + s * LANES:base + (s + 1) * LANES] = (
                        tmp_ref[s, pl.ds(cls, tm // d, stride=d), :].astype(out_ref.dtype))

    def normed_batch(arrays):
        chunks = [(ai, c) for ai in range(len(arrays)) for c in range(0, B_WIDTH, MXU_COLS)]
        for ci, (ai, c) in enumerate(chunks):
            a = proj(arrays[ai][0] + c, MXU_COLS)
            a_scr[ci] = a
            sq_scr[ci * tm:(ci + 1) * tm, :] = (a * a).astype(jnp.bfloat16)
        n = len(chunks) * tm
        ss_scr[0:n, :] = jnp.dot(sq_scr[0:n, :], ones, preferred_element_type=jnp.float32)
        for ci, (ai, c) in enumerate(chunks):
            _, gain_ref, gcol0, out_ref, d, state = arrays[ai]
            o = (a_scr[ci] * lax.rsqrt(ss_scr[ci * tm:(ci + 1) * tm, :] + EPS)
                 * gain_ref[:, gcol0 + c:gcol0 + c + MXU_COLS])
            emit(o, c, out_ref, d, state)
            if c + MXU_COLS == B_WIDTH:
                finish(out_ref, d, B_WIDTH)

    q_off = (0, A_WIDTH, A_WIDTH + B_WIDTH, A_WIDTH + 2 * B_WIDTH)
    kv_off = (0, A_KV_WIDTH, A_KV_WIDTH + B_WIDTH, A_KV_WIDTH + 2 * B_WIDTH)
    plan = [_state_plan(state_rows[g], seq, tm)[1] for g in range(4)]
    k_out = [(qkv_refs[g], A_WIDTH) for g in range(4)]
    v_out = [(qkv_refs[g], A_WIDTH + KV_WIDTHS[g]) for g in range(4)]
    normed_batch([(q_off[g], gq_ref, q_off[g], (qkv_refs[g], 0), dils[g], None) for g in range(4)])
    normed_batch([(Q_COLS + kv_off[g], gk_ref, kv_off[g], k_out[g], dils[g], (state_refs[g], 0, plan[g]))
                  for g in range(1, 4)])
    a = proj(Q_COLS, A_KV_WIDTH)
    ss = jnp.dot((a * a).astype(jnp.bfloat16), ones[:A_KV_WIDTH, :A_KV_WIDTH], preferred_element_type=jnp.float32)
    emit(a * lax.rsqrt(ss + EPS) * gk_ref[:, 0:A_KV_WIDTH], 0, k_out[0], dils[0], (state_refs[0], 0, plan[0]))
    for g in range(4):
        wd = KV_WIDTHS[g]
        for c in range(0, wd, MXU_COLS):
            w = min(MXU_COLS, wd - c)
            emit(proj(Q_COLS + KV_COLS + kv_off[g] + c, w), c, v_out[g], dils[g], (state_refs[g], 1, plan[g]))
        finish(v_out[g], dils[g], wd)


def _qkv_proj(x2d, ng, w_qkv, gq, gk, ones, *, nb, seq, state_rows, dils, tm, states_t):
    t = x2d.shape[0]
    tpb = seq // tm

    def cls_shape(d, width):
        return jax.ShapeDtypeStruct((nb, d, seq // d, width), jnp.bfloat16)

    def cls_spec(d, width):
        return pl.BlockSpec((1, d, tm // d, width), lambda i: (i // tpb, 0, i % tpb, 0))

    def state_shape(g):
        if states_t:
            return jax.ShapeDtypeStruct((nb, 2, KV_WIDTHS[g], state_rows[g]), jnp.float32)
        return jax.ShapeDtypeStruct((nb * state_rows[g], 2 * KV_WIDTHS[g]), jnp.float32)

    def state_spec(g):
        _, r, nblk, j0 = _state_plan(state_rows[g], seq, tm)
        if states_t:
            return pl.BlockSpec((1, 2, KV_WIDTHS[g], r), lambda i: (i // tpb, 0, 0, jnp.maximum(i % tpb - j0, 0)))
        return pl.BlockSpec((r, 2 * KV_WIDTHS[g]),
                            lambda i: ((i // tpb) * nblk + jnp.maximum(i % tpb - j0, 0), 0))

    const = lambda i: (0, 0)
    out_shape = ([cls_shape(d, A_WIDTH + 2 * w) for d, w in zip(dils, KV_WIDTHS)]
                 + [state_shape(g) for g in range(4)])
    out_specs = ([cls_spec(d, A_WIDTH + 2 * w) for d, w in zip(dils, KV_WIDTHS)]
                 + [state_spec(g) for g in range(4)])
    res = pl.pallas_call(
        functools.partial(_qkv_kernel, tm=tm, seq=seq, state_rows=state_rows, dils=dils, states_t=states_t),
        grid=(t // tm,),
        in_specs=[
            pl.BlockSpec((tm, D_MODEL), lambda i: (i, 0)),
            pl.BlockSpec((1, D_MODEL), const),
            pl.BlockSpec((D_MODEL, QKV_COLS), const, pipeline_mode=pl.Buffered(1)),
            pl.BlockSpec((1, Q_COLS), const),
            pl.BlockSpec((1, KV_COLS), const),
            pl.BlockSpec((MXU_COLS, MXU_COLS), const),
        ],
        out_specs=out_specs,
        out_shape=out_shape,
        scratch_shapes=[pltpu.VMEM((N_PAIRS, tm, LANES), jnp.float32),
                        pltpu.VMEM((NORM_BATCH, tm, MXU_COLS), jnp.float32),
                        pltpu.VMEM((NORM_BATCH * tm, MXU_COLS), jnp.bfloat16),
                        pltpu.VMEM((NORM_BATCH * tm, MXU_COLS), jnp.float32)],
        compiler_params=pltpu.CompilerParams(
            dimension_semantics=("arbitrary",), vmem_limit_bytes=VMEM_LIMIT),
        name="qkv_proj",
    )(x2d, ng, w_qkv, gq, gk, ones)
    return res[0:4], res[4:8]


def _lookup_rows(bucket_map, top, table_ref, col_top, col_bottom, fill):
    acc = jnp.full(bucket_map.shape, fill, jnp.float32)
    for b in range(N_BUCKETS):
        val = jnp.where(top, table_ref[b, col_top], table_ref[b, col_bottom])
        acc = jnp.where(bucket_map == b, val, acc)
    return acc


def _band_block(qkv_ref, kp_ref, vp_ref, o_ref, o_stage, lse_ref, bias_ref, sink_ref, *,
                cls, first, dil, shared_kv):
    blk = N_KEYS
    row = lax.broadcasted_iota(jnp.int32, (2 * blk, blk), 0)
    lane = lax.broadcasted_iota(jnp.int32, (2 * blk, blk), 1)
    ahead = lane - jnp.bitwise_and(row, blk - 1)
    tri = ahead <= 0
    valid = ahead <= jnp.where(first, 0, blk)
    cur_part = tri.astype(jnp.bfloat16)
    prev_part = jnp.logical_not(tri).astype(jnp.bfloat16)
    left = lax.broadcasted_iota(jnp.int32, (blk, LANES), 1) < HEAD_DIM
    mask_l = left.astype(jnp.bfloat16)
    mask_r = jnp.logical_not(left).astype(jnp.bfloat16)
    ones_kv = jnp.ones((2 * blk, LANES), jnp.bfloat16)
    nt = (((1,), (1,)), ((), ()))
    rows = pl.ds(0, blk) if dil == 1 else pl.ds(cls, blk, stride=dil)
    wkv = kp_ref.shape[2]
    k0, v0 = A_WIDTH, A_WIDTH + wkv
    for p in range(N_PAIRS):
        kcol = 0 if shared_kv else p * LANES
        qp = qkv_ref[0, 0, :, p * LANES:(p + 1) * LANES]
        qs = jnp.concatenate([qp * mask_l, qp * mask_r], axis=0)
        kc = jnp.concatenate([kp_ref[cls, :, kcol:kcol + LANES],
                              qkv_ref[0, 0, :, k0 + kcol:k0 + kcol + LANES]], axis=0)
        vc = jnp.concatenate([vp_ref[cls, :, kcol:kcol + LANES],
                              qkv_ref[0, 0, :, v0 + kcol:v0 + kcol + LANES]], axis=0)
        s2 = lax.dot_general(qs, kc, nt, preferred_element_type=jnp.float32)
        s = jnp.where(tri, s2[:, blk:], s2[:, :blk]) + bias_ref[p]
        s = jnp.where(valid, s, NEG_INF)
        m = jnp.max(s, axis=-1, keepdims=True)
        if sink_ref is not None:
            sk = sink_ref[p]
            m = jnp.maximum(m, sk)
        pb = jnp.exp(s - m).astype(jnp.bfloat16)
        p2 = jnp.concatenate([pb * prev_part, pb * cur_part], axis=1)
        ov = jnp.dot(p2, jnp.concatenate([vc, ones_kv], axis=1), preferred_element_type=jnp.float32)
        l = ov[:, LANES:]
        if sink_ref is not None:
            l = l + jnp.exp(sk - m)
        num = jnp.where(left, ov[:blk, :LANES], ov[blk:, :LANES])
        den = jnp.where(left, l[:blk], l[blk:])
        if dil == 1:
            o_ref[0, p, rows, :] = (num / den).astype(o_ref.dtype)
        else:
            o_stage[p, rows, :] = num / den
        if lse_ref is not None:
            m1 = jnp.where(left, m[:blk], m[blk:])
            lse_ref[0, p, rows, :] = m1 + jnp.log(den)
    kp_ref[cls] = qkv_ref[0, 0, :, k0:k0 + wkv]
    vp_ref[cls] = qkv_ref[0, 0, :, v0:v0 + wkv]


def _decode_kernel(qa_ref, qb1_ref, qb2_ref, qb3_ref, na_ref, nb1_ref, nb2_ref, nb3_ref,
                   ca_ref, cb1_ref, cb2_ref, cb3_ref, ma_ref, mb1_ref, mb2_ref, mb3_ref, mn_ref, relb_ref, sinks_ref,
                   oa_ref, ob_ref, ta_ref, tb1_ref, tb2_ref, tb3_ref, tn_ref, sink_ref,
                   *, dec_seq, head_cols, sink_idx, dils, init):
    t = dec_seq
    if init:
        for g, (m_ref, t_ref) in enumerate(zip((ma_ref, mb1_ref, mb2_ref, mb3_ref),
                                               (ta_ref, tb1_ref, tb2_ref, tb3_ref))):
            top = lax.broadcasted_iota(jnp.int32, m_ref.shape, 0) < t
            top_n = lax.broadcasted_iota(jnp.int32, (2 * t, LANES), 0) < t
            for p in range(N_PAIRS):
                cols = (head_cols[g][2 * p], head_cols[g][2 * p + 1])
                rows = slice(p * 2 * t, (p + 1) * 2 * t)
                t_ref[rows, :] = _lookup_rows(m_ref[...], top, relb_ref, cols[0], cols[1], NEG_INF)
                tn_ref[g, rows, :] = _lookup_rows(mn_ref[g], top_n, relb_ref, cols[0], cols[1], NEG_INF)
                if g == 0:
                    sink_ref[rows, :] = jnp.where(top_n, sinks_ref[0, sink_idx[2 * p]],
                                                  sinks_ref[0, sink_idx[2 * p + 1]])
        return

    left8 = lax.broadcasted_iota(jnp.int32, (2 * t, LANES), 1) < HEAD_DIM
    top8 = lax.broadcasted_iota(jnp.int32, (2 * t, LANES), 0) < t
    own = (left8 == top8).astype(jnp.float32)
    left4 = lax.broadcasted_iota(jnp.int32, (t, LANES), 1) < HEAD_DIM
    nt = (((1,), (1,)), ((), ()))

    seq_id = pl.program_id(0)

    def seq_rows(block):
        k = block.shape[0] // t
        which = seq_id % k
        rows = block[0:t]
        for j in range(1, k):
            rows = jnp.where(which == j, block[j * t:(j + 1) * t], rows)
        return rows

    q_refs = (qa_ref, qb1_ref, qb2_ref, qb3_ref)

    def q_rows(g, p):
        qp = seq_rows(q_refs[g][:, p * LANES:(p + 1) * LANES].astype(jnp.float32))
        return jnp.concatenate([qp, qp], axis=0) * own

    def attend(qrs, kts, vts, knews, vnews, tbl, tbl_new, sink, dil):
        shared = len(kts) == 1
        rows8 = lambda a, u: jnp.broadcast_to(a[u:u + 1, :], (2 * t, LANES))
        qr = jnp.concatenate(qrs, axis=0)
        qb = qr.astype(jnp.bfloat16)
        if shared:
            s_c = jnp.dot(qb, kts[0], preferred_element_type=jnp.float32)
        else:
            s_c = jnp.concatenate([jnp.dot(qb[2 * t * p:2 * t * (p + 1)], kts[p], preferred_element_type=jnp.float32)
                                   for p in range(N_PAIRS)], axis=0)
        yield
        s_c = s_c + tbl
        kn = [jnp.concatenate([rows8(knews[0 if shared else p], u) for p in range(N_PAIRS)], axis=0)
              for u in range(t)]
        vn = [jnp.concatenate([rows8(vnews[0 if shared else p], u) for p in range(N_PAIRS)], axis=0)
              for u in range(t)]
        s_n = [jnp.sum(qr * kn[u], axis=-1, keepdims=True) + tbl_new[:, u:u + 1] for u in range(t)]
        m = functools.reduce(jnp.maximum, s_n, jnp.max(s_c, axis=-1, keepdims=True))
        if sink is not None:
            m = jnp.maximum(m, sink)
        pc = jnp.exp(s_c - m)
        pn = [jnp.exp(x - m) for x in s_n]
        l = functools.reduce(lambda a, b: a + b, pn, jnp.sum(pc, axis=-1, keepdims=True))
        if sink is not None:
            l = l + jnp.exp(sink - m)
        pb = pc.astype(jnp.bfloat16)
        yield
        if shared:
            o = lax.dot_general(pb, vts[0], nt, preferred_element_type=jnp.float32)
        elif dil >= t:
            lane8 = lax.broadcasted_iota(jnp.int32, (2 * t, LANES), 1)
            row8 = lax.broadcasted_iota(jnp.int32, (2 * t, LANES), 0)
            sel = (lane8 % dil == row8 % t).astype(jnp.bfloat16)
            parts = []
            for p in range(N_PAIRS):
                accs = []
                for side in range(2):
                    r0 = 2 * t * p + t * side
                    w = jnp.sum(pc[r0:r0 + t], axis=0, keepdims=True)
                    v = vts[p][side * HEAD_DIM:(side + 1) * HEAD_DIM]
                    accs.append(functools.reduce(
                        lambda a, b: a + b,
                        [v[:, c:c + LANES] * w[:, c:c + LANES] for c in range(0, v.shape[1], LANES)]))
                acc = jnp.concatenate(accs, axis=0)
                hi = acc.astype(jnp.bfloat16)
                lo = (acc - hi.astype(jnp.float32)).astype(jnp.bfloat16)
                parts.append(lax.dot_general(sel, hi, nt, preferred_element_type=jnp.float32)
                             + lax.dot_general(sel, lo, nt, preferred_element_type=jnp.float32))
            o = jnp.concatenate(parts, axis=0)
        else:
            o = jnp.concatenate([lax.dot_general(pb[2 * t * p:2 * t * (p + 1)], vts[p], nt,
                                                 preferred_element_type=jnp.float32)
                                 for p in range(N_PAIRS)], axis=0)
        for u in range(t):
            o = o + pn[u] * vn[u]
        yield o / l, m + jnp.log(l)

    def fold(x, p):
        return jnp.where(left4, x[2 * t * p:2 * t * p + t], x[2 * t * p + t:2 * t * (p + 1)])

    def cache_kv(c_ref, kv, p, dtype=jnp.bfloat16):
        return c_ref[0, kv, 2 * p:2 * p + 2].reshape(LANES, c_ref.shape[-1]).astype(dtype)

    caches = (cb1_ref, cb2_ref, cb3_ref)
    news = (nb1_ref, nb2_ref, nb3_ref)
    tbls = (tb1_ref, tb2_ref, tb3_ref)
    new_a = seq_rows(na_ref[...])
    groups = [attend([q_rows(0, p) for p in range(N_PAIRS)],
                     [cache_kv(ca_ref, 0, 0)], [cache_kv(ca_ref, 1, 0)],
                     [new_a[:, 0:LANES]], [new_a[:, LANES:2 * LANES]],
                     ta_ref[...], tn_ref[0], sink_ref[:, 0:1], dils[0])]
    for g in range(N_B_GROUPS):
        new_g = seq_rows(news[g][...])
        groups.append(attend(
            [q_rows(1 + g, p) for p in range(N_PAIRS)],
            [cache_kv(caches[g], 0, p) for p in range(N_PAIRS)],
            [cache_kv(caches[g], 1, p, jnp.float32 if dils[1 + g] >= t else jnp.bfloat16) for p in range(N_PAIRS)],
            [new_g[:, p * LANES:(p + 1) * LANES] for p in range(N_PAIRS)],
            [new_g[:, B_WIDTH + p * LANES:B_WIDTH + (p + 1) * LANES] for p in range(N_PAIRS)],
            tbls[g][...], tn_ref[1 + g], None, dils[1 + g]))
    for phase in range(2):
        for gen in groups:
            next(gen)
    results = [next(gen) for gen in groups]
    oa = results[0][0]
    for p in range(N_PAIRS):
        oa_ref[0, p] = fold(oa, p).astype(oa_ref.dtype)
    outs = [o for o, _ in results[1:]]
    lses = [jnp.broadcast_to(lse, o.shape) for o, lse in results[1:]]
    mx = jnp.maximum(jnp.maximum(lses[0], lses[1]), lses[2])
    es = [jnp.exp(x - mx) for x in lses]
    den = es[0] + es[1] + es[2]
    comb = (es[0] * outs[0] + es[1] * outs[1] + es[2] * outs[2]) / den
    for p in range(N_PAIRS):
        ob_ref[0, p] = fold(comb, p).astype(ob_ref.dtype)


def _decode_bucket_maps(dils, cache_lens, dec_seq):
    t_idx = np.arange(dec_seq)
    maps = []
    new = np.full((len(dils), 2 * dec_seq, LANES), -1, np.int32)
    for g, (d, ln) in enumerate(zip(dils, cache_lens)):
        back = ln + t_idx[:, None] - np.arange(ln)[None, :]
        ok = (back % d == 0) & (back // d >= 1) & (back // d < N_KEYS)
        maps.append(jnp.asarray(np.tile(np.where(ok, _t5_bucket_np(back), -1), (2, 1)).astype(np.int32)))
        backn = t_idx[:, None] - t_idx[None, :]
        okn = (backn >= 0) & (backn % d == 0) & (backn // d < N_KEYS)
        new[g, :, :dec_seq] = np.tile(np.where(okn, _t5_bucket_np(backn), -1), (2, 1))
    return maps, jnp.asarray(new)


def _decode_band_kernel(*refs, dec_seq, head_cols, sink_idx, dils, blocks_per_batch):
    n_g = len(dils)
    n_dec_in = 4 * n_g + 3
    dec_in = refs[0:n_dec_in]
    relb_ref, sinks_ref = dec_in[-2:]
    band_in = [refs[n_dec_in + 2 * g:n_dec_in + 2 * (g + 1)] for g in range(n_g)]
    outs = refs[n_dec_in + 2 * n_g:n_dec_in + 2 * n_g + 2 + 2 * n_g - 1]
    oa_ref, ob_ref = outs[0:2]
    o_refs = outs[2:2 + n_g]
    lse_refs = (None,) + tuple(outs[2 + n_g:])
    scratch = refs[n_dec_in + 2 * n_g + 2 + 2 * n_g - 1:]
    dec_scratch = scratch[0:6]
    bias_refs = scratch[6:6 + n_g]
    sink_tile = scratch[6 + n_g]
    kprev_refs = scratch[7 + n_g:7 + 2 * n_g]
    vprev_refs = scratch[7 + 2 * n_g:7 + 3 * n_g]
    stages = iter(scratch[7 + 3 * n_g:])
    stage_refs = [None if d == 1 else next(stages) for d in dils]
    blk = N_KEYS
    s = pl.program_id(0)
    decode = functools.partial(_decode_kernel, *dec_in, oa_ref, ob_ref, *dec_scratch, dec_seq=dec_seq,
                               head_cols=head_cols, sink_idx=sink_idx, dils=dils)

    @pl.when(s == 0)
    def _():
        top = lax.broadcasted_iota(jnp.int32, (2 * blk, blk), 0) < blk
        for g in range(n_g):
            bmap = band_in[g][1][...]
            for p in range(N_PAIRS):
                bias_refs[g][p] = _lookup_rows(bmap, top, relb_ref, head_cols[g][2 * p], head_cols[g][2 * p + 1],
                                               0.0)
            kprev_refs[g][...] = jnp.zeros(kprev_refs[g].shape, kprev_refs[g].dtype)
            vprev_refs[g][...] = jnp.zeros(vprev_refs[g].shape, vprev_refs[g].dtype)
        for p in range(N_PAIRS):
            sink_tile[p] = jnp.where(top, sinks_ref[0, sink_idx[2 * p]], sinks_ref[0, sink_idx[2 * p + 1]])
        decode(init=True)

    r = s % blocks_per_batch

    decode(init=False)
    for g in range(n_g):
        _band_block(band_in[g][0], kprev_refs[g], vprev_refs[g], o_refs[g], stage_refs[g], lse_refs[g],
                    bias_refs[g], sink_tile if g == 0 else None, cls=r % dils[g], first=(r // dils[g]) == 0,
                    dil=dils[g], shared_kv=(g == 0))

    for g in range(n_g):
        if dils[g] > 1:
            @pl.when(r % dils[g] == dils[g] - 1)
            def _():
                for p in range(N_PAIRS):
                    o_refs[g][0, p] = stage_refs[g][p].astype(o_refs[g].dtype)


def _decode_band_attention(q_dec, news, caches, qkvs, rel_bias, sinks, *, dec_seq, dils, head_cols, sink_idx):
    t = dec_seq
    n = q_dec[0].shape[0] // t
    nb = qkvs[0].shape[0]
    seq = qkvs[0].shape[1] * qkvs[0].shape[2]
    ppb = seq // N_KEYS
    assert n == nb * ppb, "one prompt block of every group per sample sequence"
    seq3 = lambda i: (i, 0, 0)
    seq4 = lambda i: (i, 0, 0, 0)
    seq5 = lambda i: (i, 0, 0, 0, 0)
    cache_lens = tuple(c.shape[-1] for c in caches)
    maps, map_new = _decode_bucket_maps(dils, cache_lens, t)
    smem = functools.partial(pl.BlockSpec, memory_space=pltpu.SMEM)
    q_rows, new_rows = 16, 8
    in_specs = [pl.BlockSpec((q_rows, A_WIDTH), lambda i: (i // (q_rows // t), 0)) for _ in q_dec]
    in_specs += [pl.BlockSpec((new_rows, a.shape[1]), lambda i: (i // (new_rows // t), 0)) for a in news]
    in_specs += [pl.BlockSpec((1,) + c.shape[1:], seq5) for c in caches]
    in_specs += [pl.BlockSpec(m.shape, lambda i: (0, 0)) for m in maps]
    in_specs += [pl.BlockSpec(map_new.shape, lambda i: (0, 0, 0)), smem(), smem()]
    band_args = []
    out_specs = [pl.BlockSpec((1, N_PAIRS, t, LANES), seq4)] * 2
    out_shape = [jax.ShapeDtypeStruct((n, N_PAIRS, t, LANES), jnp.float32)] * 2
    qi = np.arange(N_KEYS)[:, None]
    ci = np.arange(N_KEYS)[None, :]
    for g, d in enumerate(dils):
        cur = lambda s, d=d: (s // ppb, (s % ppb) % d, (s % ppb) // d, 0)
        bmap = jnp.asarray(np.tile(_t5_bucket_np(((qi - ci) % N_KEYS) * d), (2, 1)).astype(np.int32))
        in_specs += [pl.BlockSpec((1, 1, N_KEYS, qkvs[g].shape[3]), cur), pl.BlockSpec(bmap.shape, lambda s: (0, 0))]
        band_args += [qkvs[g], bmap]
    tile = lambda d: pl.BlockSpec((1, N_PAIRS, N_KEYS * d, LANES), lambda s, d=d: (s // ppb, 0, (s % ppb) // d, 0))
    slab_shape = lambda dtype: jax.ShapeDtypeStruct((nb, N_PAIRS, seq, LANES), dtype)
    out_specs += [tile(d) for d in dils] + [tile(d) for d in dils[1:]]
    out_shape += [slab_shape(jnp.bfloat16)] * len(dils) + [slab_shape(jnp.float32)] * (len(dils) - 1)
    n_rows = 2 * t * N_PAIRS
    tile_scr = pltpu.VMEM((N_PAIRS, 2 * N_KEYS, N_KEYS), jnp.float32)
    res = pl.pallas_call(
        functools.partial(_decode_band_kernel, dec_seq=t, head_cols=head_cols, sink_idx=sink_idx, dils=dils,
                          blocks_per_batch=ppb),
        grid=(n,),
        in_specs=in_specs,
        out_specs=out_specs,
        out_shape=out_shape,
        scratch_shapes=[pltpu.VMEM((n_rows, ln), jnp.float32) for ln in cache_lens]
        + [pltpu.VMEM((len(dils), n_rows, LANES), jnp.float32), pltpu.VMEM((n_rows, LANES), jnp.float32)]
        + [tile_scr] * (len(dils) + 1)
        + [pltpu.VMEM((d, N_KEYS, w), jnp.bfloat16) for d, w in zip(dils, KV_WIDTHS)] * 2
        + [pltpu.VMEM((N_PAIRS, N_KEYS * d, LANES), jnp.float32) for d in dils if d > 1],
        compiler_params=pltpu.CompilerParams(
            dimension_semantics=("arbitrary",), vmem_limit_bytes=VMEM_LIMIT),
        name="decode_band_attn",
    )(*q_dec, *news, *caches, *maps, map_new, rel_bias, sinks, *band_args)
    ng = len(dils)
    return res[0], res[1], res[2:2 + ng], res[2 + ng:]


def _out_kernel(*refs, n_groups):
    x_ref, ng_ref, wg_ref, oa_ref = refs[0:4]
    ob_refs = refs[4:4 + n_groups]
    lse_refs = refs[4 + n_groups:4 + 2 * n_groups] if n_groups > 1 else ()
    wa_ref, wb_ref, wo_ref, y_ref = refs[-4:]
    x = x_ref[0]
    h = _rmsnorm_bf16(x, ng_ref[...])

    def gate(c0, width, silu):
        a = jnp.dot(h, wg_ref[:, c0:c0 + width], preferred_element_type=jnp.float32)
        sg = 0.5 * jnp.tanh(0.5 * a) + 0.5
        return a * sg if silu else sg

    def slabs(ref):
        return jnp.concatenate([ref[0, p].astype(jnp.float32) for p in range(N_PAIRS)], axis=1)

    m0 = A_WIDTH + B_WIDTH
    ga = gate(0, A_WIDTH, True)
    gb = gate(A_WIDTH, B_WIDTH, True)
    ma = gate(m0, D_MODEL, False)
    mb = gate(m0 + D_MODEL, D_MODEL, False)
    if n_groups > 1:
        parts = []
        for p in range(N_PAIRS):
            lses = [r[0, p] for r in lse_refs]
            mx = functools.reduce(jnp.maximum, lses)
            es = [jnp.exp(v - mx) for v in lses]
            den = functools.reduce(lambda a, b: a + b, es)
            num = functools.reduce(lambda a, b: a + b,
                                   [e * r[0, p].astype(jnp.float32) for e, r in zip(es, ob_refs)])
            parts.append(num / den)
        ob = jnp.concatenate(parts, axis=1)
    else:
        ob = slabs(ob_refs[0])
    oa = slabs(oa_ref)
    ya = jnp.dot((oa * ga).astype(jnp.bfloat16), wa_ref[...], preferred_element_type=jnp.float32)
    yb = jnp.dot((ob * gb).astype(jnp.bfloat16), wb_ref[...], preferred_element_type=jnp.float32)
    merged = (ma * ya + mb * yb).astype(jnp.bfloat16)
    y_ref[0] = x + jnp.dot(merged, wo_ref[...], preferred_element_type=jnp.float32)


def _out_proj(x3d, ng, wg, oa, obs, lses, wa, wb, wo, *, tm):
    nb, seq, _ = x3d.shape
    row = lambda b, i: (b, i, 0)
    slab = pl.BlockSpec((1, N_PAIRS, tm, LANES), lambda b, i: (b, 0, i, 0))
    const = lambda b, i: (0, 0)
    once = dict(pipeline_mode=pl.Buffered(1))
    in_specs = [pl.BlockSpec((1, tm, D_MODEL), row), pl.BlockSpec((1, D_MODEL), const),
                pl.BlockSpec(wg.shape, const, **once), slab]
    in_specs += [slab for _ in obs] + [slab for _ in lses]
    in_specs += [pl.BlockSpec(wa.shape, const, **once), pl.BlockSpec(wb.shape, const, **once),
                 pl.BlockSpec(wo.shape, const, **once)]
    return pl.pallas_call(
        functools.partial(_out_kernel, n_groups=len(obs)),
        grid=(nb, seq // tm),
        in_specs=in_specs,
        out_specs=pl.BlockSpec((1, tm, D_MODEL), row),
        out_shape=jax.ShapeDtypeStruct((nb, seq, D_MODEL), jnp.float32),
        compiler_params=pltpu.CompilerParams(
            dimension_semantics=("arbitrary", "arbitrary"), vmem_limit_bytes=VMEM_LIMIT),
        name="out_proj",
    )(x3d, ng, wg, oa, *obs, *lses, wa, wb, wo)


def _position_major_kernel(*refs, n_pos):
    n_g = (len(refs) - 1) // 2
    tmp_ref = refs[-1]
    for x_ref, o_ref in zip(refs[:n_g], refs[n_g:2 * n_g]):
        rows, w = x_ref.shape
        for s in range(w // LANES):
            tmp_ref[s] = x_ref[:, s * LANES:(s + 1) * LANES]
        for tt in range(n_pos):
            for s in range(w // LANES):
                o_ref[tt, s * LANES:(s + 1) * LANES, :] = tmp_ref[s, pl.ds(tt, rows // n_pos, stride=n_pos), :].T


def _position_major(news, *, n_pos):
    rows = news[0].shape[0]
    n_seq = rows // n_pos
    blk_w = N_PAIRS * LANES
    in_specs, out_specs, out_shape = [], [], []
    for a in news:
        w = min(a.shape[1], blk_w)
        last = a.shape[1] // w - 1
        in_specs.append(pl.BlockSpec((rows, w), lambda i, last=last: (0, jnp.minimum(i, last))))
        out_specs.append(pl.BlockSpec((n_pos, w, n_seq), lambda i, last=last: (0, jnp.minimum(i, last), 0)))
        out_shape.append(jax.ShapeDtypeStruct((n_pos, a.shape[1], n_seq), jnp.float32))
    return pl.pallas_call(
        functools.partial(_position_major_kernel, n_pos=n_pos),
        grid=(max(a.shape[1] for a in news) // blk_w,),
        in_specs=in_specs,
        out_specs=out_specs,
        out_shape=out_shape,
        scratch_shapes=[pltpu.VMEM((N_PAIRS, rows, LANES), jnp.float32)],
        compiler_params=pltpu.CompilerParams(
            dimension_semantics=("arbitrary",), vmem_limit_bytes=VMEM_LIMIT),
        name="sample_states",
    )(*news)


def _w_in_kernel(w_ref, qkv_ref, gate_ref):
    offs = np.cumsum((0, A_WIDTH, A_KV_WIDTH, A_KV_WIDTH, A_WIDTH, 3 * B_WIDTH, 3 * B_WIDTH, 3 * B_WIDTH,
                      B_WIDTH, D_MODEL, D_MODEL))

    def copy(out_ref, dst, seg, perm):
        src, width = int(offs[seg]), int(offs[seg + 1] - offs[seg])
        if perm:
            for j, h in enumerate(A_HEAD_ORDER):
                out_ref[:, dst + j * HEAD_DIM:dst + (j + 1) * HEAD_DIM] = (
                    w_ref[0, :, src + h * HEAD_DIM:src + (h + 1) * HEAD_DIM].astype(out_ref.dtype))
        else:
            out_ref[:, dst:dst + width] = w_ref[0, :, src:src + width].astype(out_ref.dtype)
        return dst + width

    dst = 0
    for seg, perm in ((0, True), (4, False), (1, False), (5, False), (2, False), (6, False)):
        dst = copy(qkv_ref, dst, seg, perm)
    dst = 0
    for seg, perm in ((3, True), (7, False), (8, False), (9, False)):
        dst = copy(gate_ref, dst, seg, perm)


def _prep_w_in(w_in):
    rows = LANES
    return pl.pallas_call(
        _w_in_kernel,
        grid=(D_MODEL // rows,),
        in_specs=[pl.BlockSpec((1, rows, w_in.shape[2]), lambda i: (0, i, 0))],
        out_specs=(pl.BlockSpec((rows, QKV_COLS), lambda i: (i, 0)), pl.BlockSpec((rows, G_COLS), lambda i: (i, 0))),
        out_shape=(jax.ShapeDtypeStruct((D_MODEL, QKV_COLS), jnp.bfloat16),
                   jax.ShapeDtypeStruct((D_MODEL, G_COLS), jnp.bfloat16)),
        compiler_params=pltpu.CompilerParams(dimension_semantics=("arbitrary",), vmem_limit_bytes=VMEM_LIMIT),
        name="prep_w_in",
    )(w_in)


def _prep_params(w_in, q_gain_a, k_gain_a, q_gain_b, k_gain_b, w_up_a, w_up_b):
    w_qkv, w_gate = _prep_w_in(w_in)
    gq = jnp.broadcast_to(jnp.concatenate([q_gain_a[None], q_gain_b], axis=0)[:, None, :] * Q_SCALE,
                          (1 + N_B_GROUPS, A_Q_HEADS, HEAD_DIM)).reshape(1, Q_COLS)
    gk = jnp.concatenate([jnp.broadcast_to(k_gain_a, (A_KV_HEADS, HEAD_DIM)).reshape(A_KV_WIDTH),
                          jnp.broadcast_to(k_gain_b[:, None, :], (N_B_GROUPS, B_HEADS, HEAD_DIM)).reshape(-1)])[None]
    hd = np.arange(MXU_COLS) // HEAD_DIM
    ones = jnp.asarray((hd[:, None] == hd[None, :]).astype(np.float32) / HEAD_DIM, jnp.bfloat16)
    wa = jnp.concatenate([w_up_a[h * HEAD_DIM:(h + 1) * HEAD_DIM] for h in A_HEAD_ORDER], axis=0)
    return w_qkv, w_gate, gq, gk, ones, wa.astype(jnp.bfloat16), w_up_b.astype(jnp.bfloat16)


def kernel(x_prompt, x_sample, cache_a_kv, cache_b1_kv, cache_b2_kv, cache_b3_kv, rel_bias, norm_gain, w_in,
           q_gain_a, k_gain_a, sinks_a, q_gain_b, k_gain_b, w_up_a, w_up_b, w_out):
    assert norm_gain.shape[0] == 1, "single layer"
    nb, seq, _ = x_prompt.shape
    n_dec, dec_seq, _ = x_sample.shape
    w_qkv, w_gate, gq, gk, ones, wa, wb = _prep_params(
        w_in, q_gain_a[0], k_gain_a[0], q_gain_b[0], k_gain_b[0], w_up_a[0], w_up_b[0])
    wo = w_out[0].astype(jnp.bfloat16)
    ng = norm_gain
    windows = (A_WINDOW,) + tuple(w for w, _ in B_GROUPS)
    dils = (1,) + tuple(d for _, d in B_GROUPS)
    head_cols = (A_HEAD_ORDER,) + tuple(tuple(range(A_Q_HEADS + g * B_HEADS, A_Q_HEADS + (g + 1) * B_HEADS))
                                        for g in range(N_B_GROUPS))

    p_rows = tuple(min(w, seq) for w in windows)
    qkvs, states = _qkv_proj(x_prompt.reshape(nb * seq, D_MODEL), ng, w_qkv, gq, gk, ones,
                             nb=nb, seq=seq, state_rows=p_rows, dils=dils, tm=min(seq, PROJ_TILE_ROWS),
                             states_t=True)
    t_dec = n_dec * dec_seq
    qd, news = _qkv_proj(x_sample.reshape(t_dec, D_MODEL), ng, w_qkv, gq, gk, ones,
                         nb=1, seq=t_dec, state_rows=(t_dec,) * 4, dils=(1, 1, 1, 1),
                         tm=min(t_dec, PROJ_TILE_ROWS), states_t=False)

    q_dec = [a.reshape(t_dec, a.shape[3]) for a in qd]
    caches = [jnp.transpose(c[0], (0, 2, 3, 4, 1)) for c in (cache_a_kv, cache_b1_kv, cache_b2_kv, cache_b3_kv)]
    oa_s, ob_s, o_prompt, lses = _decode_band_attention(
        q_dec, news, caches, qkvs, rel_bias, sinks_a, dec_seq=dec_seq, dils=dils, head_cols=head_cols,
        sink_idx=A_HEAD_ORDER)

    y_prompt = _out_proj(x_prompt, ng, w_gate, o_prompt[0], list(o_prompt[1:]), list(lses), wa, wb, wo,
                         tm=min(seq, PROJ_TILE_ROWS))
    heads = (A_KV_HEADS, B_HEADS, B_HEADS, B_HEADS)
    new_prompt = tuple(jnp.transpose(states[g].reshape(1, nb, 2, heads[g], HEAD_DIM, p_rows[g]), (0, 1, 5, 2, 3, 4))
                       for g in range(4))
    to_slabs = lambda o: jnp.transpose(o, (1, 0, 2, 3)).reshape(1, N_PAIRS, t_dec, LANES)
    y_sample = _out_proj(x_sample.reshape(1, t_dec, D_MODEL), ng, w_gate, to_slabs(oa_s), [to_slabs(ob_s)], [],
                         wa, wb, wo, tm=min(t_dec, PROJ_TILE_ROWS))
    y_sample = y_sample.reshape(n_dec, dec_seq, D_MODEL)
    new_sample = tuple(jnp.transpose(s.reshape(1, dec_seq, 2, heads[g], HEAD_DIM, n_dec), (0, 5, 1, 2, 3, 4))
                       for g, s in enumerate(_position_major(news, n_pos=dec_seq)))
    return (y_prompt, y_sample) + new_prompt + new_sample
```

```python
import functools
import math

import numpy as np
import jax
import jax.numpy as jnp
from jax import lax
from jax.experimental import pallas as pl
from jax.experimental.pallas import tpu as pltpu

D_MODEL = 1024
HEAD_DIM = 64
A_Q_HEADS = 8
A_KV_HEADS = 2
A_WINDOW = 128
B_GROUPS = ((128, 1), (512, 4), (2048, 16))
N_B_GROUPS = 3
B_HEADS = 8
N_KEYS = 128
A_WIDTH = A_Q_HEADS * HEAD_DIM
A_KV_WIDTH = A_KV_HEADS * HEAD_DIM
B_WIDTH = B_HEADS * HEAD_DIM
N_BUCKETS = 32
MAX_DISTANCE = 2048
EPS = 1e-6
NEG_INF = -1e30
Q_SCALE = HEAD_DIM ** -0.5

LANES = 128
MXU_COLS = 256
VMEM_LIMIT = 56 * 1024 * 1024
PROJ_TILE_ROWS = 512
SAMPLE_TILE_ROWS = 256
N_PAIRS = A_WIDTH // LANES

A_HEAD_ORDER = (0, 4, 1, 5, 2, 6, 3, 7)

Q_COLS = A_WIDTH + N_B_GROUPS * B_WIDTH
KV_COLS = A_KV_WIDTH + N_B_GROUPS * B_WIDTH
QKV_COLS = Q_COLS + 2 * KV_COLS
G_COLS = A_WIDTH + B_WIDTH + 2 * D_MODEL
KV_WIDTHS = (A_KV_WIDTH, B_WIDTH, B_WIDTH, B_WIDTH)
NORM_BATCH = 4 * B_WIDTH // MXU_COLS


def _t5_bucket_np(dist):
    max_exact = N_BUCKETS // 2
    d = np.maximum(dist, 0)
    df = np.maximum(d, 1).astype(np.float32)
    large = max_exact + (np.log(df / np.float32(max_exact)) / np.float32(math.log(MAX_DISTANCE / max_exact))
                         * np.float32(N_BUCKETS - max_exact)).astype(np.int32)
    large = np.minimum(large, N_BUCKETS - 1)
    return np.where(d < max_exact, d, large)


def _rmsnorm_bf16(x, gain):
    ms = jnp.mean(x * x, axis=-1, keepdims=True)
    return (x * lax.rsqrt(ms + EPS) * gain).astype(jnp.bfloat16)


def _state_plan(n_rows, seq, tm):
    tpb = seq // tm
    r = min(n_rows, tm)
    nblk = max(n_rows // tm, 1)
    return tpb, r, nblk, tpb - nblk


def _qkv_kernel(x_ref, ng_ref, w_ref, gq_ref, gk_ref, ones_ref, *rest, tm, seq, state_rows, dils, states_t):
    qkv_refs = rest[0:4]
    state_refs = rest[4:8]
    tmp_ref, a_scr, sq_scr, ss_scr = rest[8:12]
    h = _rmsnorm_bf16(x_ref[...], ng_ref[...])
    ones = ones_ref[...]

    def proj(col0, w):
        return jnp.dot(h, w_ref[:, col0:col0 + w], preferred_element_type=jnp.float32)

    def emit(a, c, out, d, state):
        w = a.shape[1]
        out_ref, base = out
        if d == 1:
            out_ref[0, 0, :, base + c:base + c + w] = a.astype(out_ref.dtype)
        else:
            for s in range(w // LANES):
                tmp_ref[c // LANES + s] = a[:, s * LANES:(s + 1) * LANES]
        if state is not None:
            sref, kv, r = state
            if states_t:
                sref[0, kv, c:c + w, :] = a[tm - r:, :].T
            else:
                wd = sref.shape[1] // 2
                sref[:, kv * wd + c:kv * wd + c + w] = a[tm - r:, :]

    def finish(out, d, width):
        out_ref, base = out
        if d > 1:
            for cls in range(d):
                for s in range(width // LANES):
                    out_ref[0, cls, :, base + s * LANES:base + (s + 1) * LANES] = (
                        tmp_ref[s, pl.ds(cls, tm // d, stride=d), :].astype(out_ref.dtype))

    def normed_batch(arrays):
        chunks = [(ai, c) for ai in range(len(arrays)) for c in range(0, B_WIDTH, MXU_COLS)]
        for ci, (ai, c) in enumerate(chunks):
            a = proj(arrays[ai][0] + c, MXU_COLS)
            a_scr[ci] = a
            sq_scr[ci * tm:(ci + 1) * tm, :] = (a * a).astype(jnp.bfloat16)
        n = len(chunks) * tm
        ss_scr[0:n, :] = jnp.dot(sq_scr[0:n, :], ones, preferred_element_type=jnp.float32)
        for ci, (ai, c) in enumerate(chunks):
            _, gain_ref, gcol0, out_ref, d, state = arrays[ai]
            o = (a_scr[ci] * lax.rsqrt(ss_scr[ci * tm:(ci + 1) * tm, :] + EPS)
                 * gain_ref[:, gcol0 + c:gcol0 + c + MXU_COLS])
            emit(o, c, out_ref, d, state)
            if c + MXU_COLS == B_WIDTH:
                finish(out_ref, d, B_WIDTH)

    q_off = (0, A_WIDTH, A_WIDTH + B_WIDTH, A_WIDTH + 2 * B_WIDTH)
    kv_off = (0, A_KV_WIDTH, A_KV_WIDTH + B_WIDTH, A_KV_WIDTH + 2 * B_WIDTH)
    plan = [_state_plan(state_rows[g], seq, tm)[1] for g in range(4)]
    k_out = [(qkv_refs[g], A_WIDTH) for g in range(4)]
    v_out = [(qkv_refs[g], A_WIDTH + KV_WIDTHS[g]) for g in range(4)]
    normed_batch([(q_off[g], gq_ref, q_off[g], (qkv_refs[g], 0), dils[g], None) for g in range(4)])
    normed_batch([(Q_COLS + kv_off[g], gk_ref, kv_off[g], k_out[g], dils[g], (state_refs[g], 0, plan[g]))
                  for g in range(1, 4)])
    a = proj(Q_COLS, A_KV_WIDTH)
    ss = jnp.dot((a * a).astype(jnp.bfloat16), ones[:A_KV_WIDTH, :A_KV_WIDTH], preferred_element_type=jnp.float32)
    emit(a * lax.rsqrt(ss + EPS) * gk_ref[:, 0:A_KV_WIDTH], 0, k_out[0], dils[0], (state_refs[0], 0, plan[0]))
    for g in range(4):
        wd = KV_WIDTHS[g]
        for c in range(0, wd, MXU_COLS):
            w = min(MXU_COLS, wd - c)
            emit(proj(Q_COLS + KV_COLS + kv_off[g] + c, w), c, v_out[g], dils[g], (state_refs[g], 1, plan[g]))
        finish(v_out[g], dils[g], wd)


def _qkv_proj(x2d, ng, w_qkv, gq, gk, ones, *, nb, seq, state_rows, dils, tm, states_t):
    t = x2d.shape[0]
    tpb = seq // tm

    def cls_shape(d, width):
        return jax.ShapeDtypeStruct((nb, d, seq // d, width), jnp.bfloat16)

    def cls_spec(d, width):
        return pl.BlockSpec((1, d, tm // d, width), lambda i: (i // tpb, 0, i % tpb, 0))

    def state_shape(g):
        if states_t:
            return jax.ShapeDtypeStruct((nb, 2, KV_WIDTHS[g], state_rows[g]), jnp.float32)
        return jax.ShapeDtypeStruct((nb * state_rows[g], 2 * KV_WIDTHS[g]), jnp.float32)

    def state_spec(g):
        _, r, nblk, j0 = _state_plan(state_rows[g], seq, tm)
        if states_t:
            return pl.BlockSpec((1, 2, KV_WIDTHS[g], r), lambda i: (i // tpb, 0, 0, jnp.maximum(i % tpb - j0, 0)))
        return pl.BlockSpec((r, 2 * KV_WIDTHS[g]),
                            lambda i: ((i // tpb) * nblk + jnp.maximum(i % tpb - j0, 0), 0))

    const = lambda i: (0, 0)
    out_shape = ([cls_shape(d, A_WIDTH + 2 * w) for d, w in zip(dils, KV_WIDTHS)]
                 + [state_shape(g) for g in range(4)])
    out_specs = ([cls_spec(d, A_WIDTH + 2 * w) for d, w in zip(dils, KV_WIDTHS)]
                 + [state_spec(g) for g in range(4)])
    res = pl.pallas_call(
        functools.partial(_qkv_kernel, tm=tm, seq=seq, state_rows=state_rows, dils=dils, states_t=states_t),
        grid=(t // tm,),
        in_specs=[
            pl.BlockSpec((tm, D_MODEL), lambda i: (i, 0)),
            pl.BlockSpec((1, D_MODEL), const),
            pl.BlockSpec((D_MODEL, QKV_COLS), const, pipeline_mode=pl.Buffered(1)),
            pl.BlockSpec((1, Q_COLS), const),
            pl.BlockSpec((1, KV_COLS), const),
            pl.BlockSpec((MXU_COLS, MXU_COLS), const),
        ],
        out_specs=out_specs,
        out_shape=out_shape,
        scratch_shapes=[pltpu.VMEM((N_PAIRS, tm, LANES), jnp.float32),
                        pltpu.VMEM((NORM_BATCH, tm, MXU_COLS), jnp.float32),
                        pltpu.VMEM((NORM_BATCH * tm, MXU_COLS), jnp.bfloat16),
                        pltpu.VMEM((NORM_BATCH * tm, MXU_COLS), jnp.float32)],
        compiler_params=pltpu.CompilerParams(
            dimension_semantics=("arbitrary",), vmem_limit_bytes=VMEM_LIMIT),
        name="qkv_proj",
    )(x2d, ng, w_qkv, gq, gk, ones)
    return res[0:4], res[4:8]


def _lookup_rows(bucket_map, top, table_ref, col_top, col_bottom, fill):
    acc = jnp.full(bucket_map.shape, fill, jnp.float32)
    for b in range(N_BUCKETS):
        val = jnp.where(top, table_ref[b, col_top], table_ref[b, col_bottom])
        acc = jnp.where(bucket_map == b, val, acc)
    return acc


def _band_block(qkv_ref, kp_ref, vp_ref, o_ref, o_stage, lse_ref, bias_ref, sink_ref, *,
                cls, first, dil, shared_kv):
    blk = N_KEYS
    row = lax.broadcasted_iota(jnp.int32, (2 * blk, blk), 0)
    lane = lax.broadcasted_iota(jnp.int32, (2 * blk, blk), 1)
    ahead = lane - jnp.bitwise_and(row, blk - 1)
    tri = ahead <= 0
    valid = ahead <= jnp.where(first, 0, blk)
    cur_part = tri.astype(jnp.bfloat16)
    prev_part = jnp.logical_not(tri).astype(jnp.bfloat16)
    left = lax.broadcasted_iota(jnp.int32, (blk, LANES), 1) < HEAD_DIM
    mask_l = left.astype(jnp.bfloat16)
    mask_r = jnp.logical_not(left).astype(jnp.bfloat16)
    ones_kv = jnp.ones((2 * blk, LANES), jnp.bfloat16)
    nt = (((1,), (1,)), ((), ()))
    rows = pl.ds(0, blk) if dil == 1 else pl.ds(cls, blk, stride=dil)
    wkv = kp_ref.shape[2]
    k0, v0 = A_WIDTH, A_WIDTH + wkv
    for p in range(N_PAIRS):
        kcol = 0 if shared_kv else p * LANES
        qp = qkv_ref[0, 0, :, p * LANES:(p + 1) * LANES]
        qs = jnp.concatenate([qp * mask_l, qp * mask_r], axis=0)
        kc = jnp.concatenate([kp_ref[cls, :, kcol:kcol + LANES],
                              qkv_ref[0, 0, :, k0 + kcol:k0 + kcol + LANES]], axis=0)
        vc = jnp.concatenate([vp_ref[cls, :, kcol:kcol + LANES],
                              qkv_ref[0, 0, :, v0 + kcol:v0 + kcol + LANES]], axis=0)
        s2 = lax.dot_general(qs, kc, nt, preferred_element_type=jnp.float32)
        s = jnp.where(tri, s2[:, blk:], s2[:, :blk]) + bias_ref[p]
        s = jnp.where(valid, s, NEG_INF)
        m = jnp.max(s, axis=-1, keepdims=True)
        if sink_ref is not None:
            sk = sink_ref[p]
            m = jnp.maximum(m, sk)
        pb = jnp.exp(s - m).astype(jnp.bfloat16)
        p2 = jnp.concatenate([pb * prev_part, pb * cur_part], axis=1)
        ov = jnp.dot(p2, jnp.concatenate([vc, ones_kv], axis=1), preferred_element_type=jnp.float32)
        l = ov[:, LANES:]
        if sink_ref is not None:
            l = l + jnp.exp(sk - m)
        num = jnp.where(left, ov[:blk, :LANES], ov[blk:, :LANES])
        den = jnp.where(left, l[:blk], l[blk:])
        if dil == 1:
            o_ref[0, p, rows, :] = (num / den).astype(o_ref.dtype)
        else:
            o_stage[p, rows, :] = num / den
        if lse_ref is not None:
            m1 = jnp.where(left, m[:blk], m[blk:])
            lse_ref[0, p, rows, :] = m1 + jnp.log(den)
    kp_ref[cls] = qkv_ref[0, 0, :, k0:k0 + wkv]
    vp_ref[cls] = qkv_ref[0, 0, :, v0:v0 + wkv]


def _decode_kernel(qa_ref, qb1_ref, qb2_ref, qb3_ref, na_ref, nb1_ref, nb2_ref, nb3_ref,
                   ca_ref, cb1_ref, cb2_ref, cb3_ref, ma_ref, mb1_ref, mb2_ref, mb3_ref, mn_ref, relb_ref, sinks_ref,
                   oa_ref, ob_ref, ta_ref, tb1_ref, tb2_ref, tb3_ref, tn_ref, sink_ref,
                   *, dec_seq, head_cols, sink_idx, dils, init):
    t = dec_seq
    if init:
        for g, (m_ref, t_ref) in enumerate(zip((ma_ref, mb1_ref, mb2_ref, mb3_ref),
                                               (ta_ref, tb1_ref, tb2_ref, tb3_ref))):
            top = lax.broadcasted_iota(jnp.int32, m_ref.shape, 0) < t
            top_n = lax.broadcasted_iota(jnp.int32, (2 * t, LANES), 0) < t
            for p in range(N_PAIRS):
                cols = (head_cols[g][2 * p], head_cols[g][2 * p + 1])
                rows = slice(p * 2 * t, (p + 1) * 2 * t)
                t_ref[rows, :] = _lookup_rows(m_ref[...], top, relb_ref, cols[0], cols[1], NEG_INF)
                tn_ref[g, rows, :] = _lookup_rows(mn_ref[g], top_n, relb_ref, cols[0], cols[1], NEG_INF)
                if g == 0:
                    sink_ref[rows, :] = jnp.where(top_n, sinks_ref[0, sink_idx[2 * p]],
                                                  sinks_ref[0, sink_idx[2 * p + 1]])
        return

    left8 = lax.broadcasted_iota(jnp.int32, (2 * t, LANES), 1) < HEAD_DIM
    top8 = lax.broadcasted_iota(jnp.int32, (2 * t, LANES), 0) < t
    own = (left8 == top8).astype(jnp.float32)
    left4 = lax.broadcasted_iota(jnp.int32, (t, LANES), 1) < HEAD_DIM
    nt = (((1,), (1,)), ((), ()))

    seq_id = pl.program_id(0)

    def seq_rows(block):
        k = block.shape[0] // t
        which = seq_id % k
        rows = block[0:t]
        for j in range(1, k):
            rows = jnp.where(which == j, block[j * t:(j + 1) * t], rows)
        return rows

    q_refs = (qa_ref, qb1_ref, qb2_ref, qb3_ref)

    def q_rows(g, p):
        qp = seq_rows(q_refs[g][:, p * LANES:(p + 1) * LANES].astype(jnp.float32))
        return jnp.concatenate([qp, qp], axis=0) * own

    def attend(qrs, kts, vts, knews, vnews, tbl, tbl_new, sink, dil):
        shared = len(kts) == 1
        rows8 = lambda a, u: jnp.broadcast_to(a[u:u + 1, :], (2 * t, LANES))
        qr = jnp.concatenate(qrs, axis=0)
        qb = qr.astype(jnp.bfloat16)
        if shared:
            s_c = jnp.dot(qb, kts[0], preferred_element_type=jnp.float32)
        else:
            s_c = jnp.concatenate([jnp.dot(qb[2 * t * p:2 * t * (p + 1)], kts[p], preferred_element_type=jnp.float32)
                                   for p in range(N_PAIRS)], axis=0)
        yield
        s_c = s_c + tbl
        kn = [jnp.concatenate([rows8(knews[0 if shared else p], u) for p in range(N_PAIRS)], axis=0)
              for u in range(t)]
        vn = [jnp.concatenate([rows8(vnews[0 if shared else p], u) for p in range(N_PAIRS)], axis=0)
              for u in range(t)]
        s_n = [jnp.sum(qr * kn[u], axis=-1, keepdims=True) + tbl_new[:, u:u + 1] for u in range(t)]
        m = functools.reduce(jnp.maximum, s_n, jnp.max(s_c, axis=-1, keepdims=True))
        if sink is not None:
            m = jnp.maximum(m, sink)
        pc = jnp.exp(s_c - m)
        pn = [jnp.exp(x - m) for x in s_n]
        l = functools.reduce(lambda a, b: a + b, pn, jnp.sum(pc, axis=-1, keepdims=True))
        if sink is not None:
            l = l + jnp.exp(sink - m)
        pb = pc.astype(jnp.bfloat16)
        yield
        if shared:
            o = lax.dot_general(pb, vts[0], nt, preferred_element_type=jnp.float32)
        elif dil >= t:
            lane8 = lax.broadcasted_iota(jnp.int32, (2 * t, LANES), 1)
            row8 = lax.broadcasted_iota(jnp.int32, (2 * t, LANES), 0)
            sel = (lane8 % dil == row8 % t).astype(jnp.bfloat16)
            parts = []
            for p in range(N_PAIRS):
                accs = []
                for side in range(2):
                    r0 = 2 * t * p + t * side
                    w = jnp.sum(pc[r0:r0 + t], axis=0, keepdims=True)
                    v = vts[p][side * HEAD_DIM:(side + 1) * HEAD_DIM]
                    accs.append(functools.reduce(
                        lambda a, b: a + b,
                        [v[:, c:c + LANES] * w[:, c:c + LANES] for c in range(0, v.shape[1], LANES)]))
                acc = jnp.concatenate(accs, axis=0)
                hi = acc.astype(jnp.bfloat16)
                lo = (acc - hi.astype(jnp.float32)).astype(jnp.bfloat16)
                parts.append(lax.dot_general(sel, hi, nt, preferred_element_type=jnp.float32)
                             + lax.dot_general(sel, lo, nt, preferred_element_type=jnp.float32))
            o = jnp.concatenate(parts, axis=0)
        else:
            o = jnp.concatenate([lax.dot_general(pb[2 * t * p:2 * t * (p + 1)], vts[p], nt,
                                                 preferred_element_type=jnp.float32)
                                 for p in range(N_PAIRS)], axis=0)
        for u in range(t):
            o = o + pn[u] * vn[u]
        yield o / l, m + jnp.log(l)

    def fold(x, p):
        return jnp.where(left4, x[2 * t * p:2 * t * p + t], x[2 * t * p + t:2 * t * (p + 1)])

    def cache_kv(c_ref, kv, p, dtype=jnp.bfloat16):
        return c_ref[0, kv, 2 * p:2 * p + 2].reshape(LANES, c_ref.shape[-1]).astype(dtype)

    caches = (cb1_ref, cb2_ref, cb3_ref)
    news = (nb1_ref, nb2_ref, nb3_ref)
    tbls = (tb1_ref, tb2_ref, tb3_ref)
    new_a = seq_rows(na_ref[...])
    groups = [attend([q_rows(0, p) for p in range(N_PAIRS)],
                     [cache_kv(ca_ref, 0, 0)], [cache_kv(ca_ref, 1, 0)],
                     [new_a[:, 0:LANES]], [new_a[:, LANES:2 * LANES]],
                     ta_ref[...], tn_ref[0], sink_ref[:, 0:1], dils[0])]
    for g in range(N_B_GROUPS):
        new_g = seq_rows(news[g][...])
        groups.append(attend(
            [q_rows(1 + g, p) for p in range(N_PAIRS)],
            [cache_kv(caches[g], 0, p) for p in range(N_PAIRS)],
            [cache_kv(caches[g], 1, p, jnp.float32 if dils[1 + g] >= t else jnp.bfloat16) for p in range(N_PAIRS)],
            [new_g[:, p * LANES:(p + 1) * LANES] for p in range(N_PAIRS)],
            [new_g[:, B_WIDTH + p * LANES:B_WIDTH + (p + 1) * LANES] for p in range(N_PAIRS)],
            tbls[g][...], tn_ref[1 + g], None, dils[1 + g]))
    for phase in range(2):
        for gen in groups:
            next(gen)
    results = [next(gen) for gen in groups]
    oa = results[0][0]
    for p in range(N_PAIRS):
        oa_ref[0, p] = fold(oa, p).astype(oa_ref.dtype)
    outs = [o for o, _ in results[1:]]
    lses = [jnp.broadcast_to(lse, o.shape) for o, lse in results[1:]]
    mx = jnp.maximum(jnp.maximum(lses[0], lses[1]), lses[2])
    es = [jnp.exp(x - mx) for x in lses]
    den = es[0] + es[1] + es[2]
    comb = (es[0] * outs[0] + es[1] * outs[1] + es[2] * outs[2]) / den
    for p in range(N_PAIRS):
        ob_ref[0, p] = fold(comb, p).astype(ob_ref.dtype)


def _decode_bucket_maps(dils, cache_lens, dec_seq):
    t_idx = np.arange(dec_seq)
    maps = []
    new = np.full((len(dils), 2 * dec_seq, LANES), -1, np.int32)
    for g, (d, ln) in enumerate(zip(dils, cache_lens)):
        back = ln + t_idx[:, None] - np.arange(ln)[None, :]
        ok = (back % d == 0) & (back // d >= 1) & (back // d < N_KEYS)
        maps.append(jnp.asarray(np.tile(np.where(ok, _t5_bucket_np(back), -1), (2, 1)).astype(np.int32)))
        backn = t_idx[:, None] - t_idx[None, :]
        okn = (backn >= 0) & (backn % d == 0) & (backn // d < N_KEYS)
        new[g, :, :dec_seq] = np.tile(np.where(okn, _t5_bucket_np(backn), -1), (2, 1))
    return maps, jnp.asarray(new)


def _decode_band_kernel(*refs, dec_seq, head_cols, sink_idx, dils, blocks_per_batch):
    n_g = len(dils)
    n_dec_in = 4 * n_g + 3
    dec_in = refs[0:n_dec_in]
    relb_ref, sinks_ref = dec_in[-2:]
    band_in = [refs[n_dec_in + 2 * g:n_dec_in + 2 * (g + 1)] for g in range(n_g)]
    outs = refs[n_dec_in + 2 * n_g:n_dec_in + 2 * n_g + 2 + 2 * n_g - 1]
    oa_ref, ob_ref = outs[0:2]
    o_refs = outs[2:2 + n_g]
    lse_refs = (None,) + tuple(outs[2 + n_g:])
    scratch = refs[n_dec_in + 2 * n_g + 2 + 2 * n_g - 1:]
    dec_scratch = scratch[0:6]
    bias_refs = scratch[6:6 + n_g]
    sink_tile = scratch[6 + n_g]
    kprev_refs = scratch[7 + n_g:7 + 2 * n_g]
    vprev_refs = scratch[7 + 2 * n_g:7 + 3 * n_g]
    stages = iter(scratch[7 + 3 * n_g:])
    stage_refs = [None if d == 1 else next(stages) for d in dils]
    blk = N_KEYS
    s = pl.program_id(0)
    decode = functools.partial(_decode_kernel, *dec_in, oa_ref, ob_ref, *dec_scratch, dec_seq=dec_seq,
                               head_cols=head_cols, sink_idx=sink_idx, dils=dils)

    @pl.when(s == 0)
    def _():
        top = lax.broadcasted_iota(jnp.int32, (2 * blk, blk), 0) < blk
        for g in range(n_g):
            bmap = band_in[g][1][...]
            for p in range(N_PAIRS):
                bias_refs[g][p] = _lookup_rows(bmap, top, relb_ref, head_cols[g][2 * p], head_cols[g][2 * p + 1],
                                               0.0)
            kprev_refs[g][...] = jnp.zeros(kprev_refs[g].shape, kprev_refs[g].dtype)
            vprev_refs[g][...] = jnp.zeros(vprev_refs[g].shape, vprev_refs[g].dtype)
        for p in range(N_PAIRS):
            sink_tile[p] = jnp.where(top, sinks_ref[0, sink_idx[2 * p]], sinks_ref[0, sink_idx[2 * p + 1]])
        decode(init=True)

    r = s % blocks_per_batch

    decode(init=False)
    for g in range(n_g):
        _band_block(band_in[g][0], kprev_refs[g], vprev_refs[g], o_refs[g], stage_refs[g], lse_refs[g],
                    bias_refs[g], sink_tile if g == 0 else None, cls=r % dils[g], first=(r // dils[g]) == 0,
                    dil=dils[g], shared_kv=(g == 0))

    for g in range(n_g):
        if dils[g] > 1:
            @pl.when(r % dils[g] == dils[g] - 1)
            def _():
                for p in range(N_PAIRS):
                    o_refs[g][0, p] = stage_refs[g][p].astype(o_refs[g].dtype)


def _decode_band_attention(q_dec, news, caches, qkvs, rel_bias, sinks, *, dec_seq, dils, head_cols, sink_idx):
    t = dec_seq
    n = q_dec[0].shape[0] // t
    nb = qkvs[0].shape[0]
    seq = qkvs[0].shape[1] * qkvs[0].shape[2]
    ppb = seq // N_KEYS
    assert n == nb * ppb, "one prompt block of every group per sample sequence"
    seq3 = lambda i: (i, 0, 0)
    seq4 = lambda i: (i, 0, 0, 0)
    seq5 = lambda i: (i, 0, 0, 0, 0)
    cache_lens = tuple(c.shape[-1] for c in caches)
    maps, map_new = _decode_bucket_maps(dils, cache_lens, t)
    smem = functools.partial(pl.BlockSpec, memory_space=pltpu.SMEM)
    q_rows, new_rows = 16, 8
    in_specs = [pl.BlockSpec((q_rows, A_WIDTH), lambda i: (i // (q_rows // t), 0)) for _ in q_dec]
    in_specs += [pl.BlockSpec((new_rows, a.shape[1]), lambda i: (i // (new_rows // t), 0)) for a in news]
    in_specs += [pl.BlockSpec((1,) + c.shape[1:], seq5) for c in caches]
    in_specs += [pl.BlockSpec(m.shape, lambda i: (0, 0)) for m in maps]
    in_specs += [pl.BlockSpec(map_new.shape, lambda i: (0, 0, 0)), smem(), smem()]
    band_args = []
    out_specs = [pl.BlockSpec((1, N_PAIRS, t, LANES), seq4)] * 2
    out_shape = [jax.ShapeDtypeStruct((n, N_PAIRS, t, LANES), jnp.float32)] * 2
    qi = np.arange(N_KEYS)[:, None]
    ci = np.arange(N_KEYS)[None, :]
    for g, d in enumerate(dils):
        cur = lambda s, d=d: (s // ppb, (s % ppb) % d, (s % ppb) // d, 0)
        bmap = jnp.asarray(np.tile(_t5_bucket_np(((qi - ci) % N_KEYS) * d), (2, 1)).astype(np.int32))
        in_specs += [pl.BlockSpec((1, 1, N_KEYS, qkvs[g].shape[3]), cur), pl.BlockSpec(bmap.shape, lambda s: (0, 0))]
        band_args += [qkvs[g], bmap]
    tile = lambda d: pl.BlockSpec((1, N_PAIRS, N_KEYS * d, LANES), lambda s, d=d: (s // ppb, 0, (s % ppb) // d, 0))
    slab_shape = lambda dtype: jax.ShapeDtypeStruct((nb, N_PAIRS, seq, LANES), dtype)
    out_specs += [tile(d) for d in dils] + [tile(d) for d in dils[1:]]
    out_shape += [slab_shape(jnp.bfloat16)] * len(dils) + [slab_shape(jnp.float32)] * (len(dils) - 1)
    n_rows = 2 * t * N_PAIRS
    tile_scr = pltpu.VMEM((N_PAIRS, 2 * N_KEYS, N_KEYS), jnp.float32)
    res = pl.pallas_call(
        functools.partial(_decode_band_kernel, dec_seq=t, head_cols=head_cols, sink_idx=sink_idx, dils=dils,
                          blocks_per_batch=ppb),
        grid=(n,),
        in_specs=in_specs,
        out_specs=out_specs,
        out_shape=out_shape,
        scratch_shapes=[pltpu.VMEM((n_rows, ln), jnp.float32) for ln in cache_lens]
        + [pltpu.VMEM((len(dils), n_rows, LANES), jnp.float32), pltpu.VMEM((n_rows, LANES), jnp.float32)]
        + [tile_scr] * (len(dils) + 1)
        + [pltpu.VMEM((d, N_KEYS, w), jnp.bfloat16) for d, w in zip(dils, KV_WIDTHS)] * 2
        + [pltpu.VMEM((N_PAIRS, N_KEYS * d, LANES), jnp.float32) for d in dils if d > 1],
        compiler_params=pltpu.CompilerParams(
            dimension_semantics=("arbitrary",), vmem_limit_bytes=VMEM_LIMIT),
        name="decode_band_attn",
    )(*q_dec, *news, *caches, *maps, map_new, rel_bias, sinks, *band_args)
    ng = len(dils)
    return res[0], res[1], res[2:2 + ng], res[2 + ng:]


def _out_kernel(*refs, n_groups):
    x_ref, ng_ref, wg_ref, oa_ref = refs[0:4]
    ob_refs = refs[4:4 + n_groups]
    lse_refs = refs[4 + n_groups:4 + 2 * n_groups] if n_groups > 1 else ()
    wa_ref, wb_ref, wo_ref, y_ref = refs[-4:]
    x = x_ref[0]
    h = _rmsnorm_bf16(x, ng_ref[...])

    def gate(c0, width, silu):
        a = jnp.dot(h, wg_ref[:, c0:c0 + width], preferred_element_type=jnp.float32)
        sg = 0.5 * jnp.tanh(0.5 * a) + 0.5
        return a * sg if silu else sg

    def slabs(ref):
        return jnp.concatenate([ref[0, p].astype(jnp.float32) for p in range(N_PAIRS)], axis=1)

    m0 = A_WIDTH + B_WIDTH
    ga = gate(0, A_WIDTH, True)
    gb = gate(A_WIDTH, B_WIDTH, True)
    ma = gate(m0, D_MODEL, False)
    mb = gate(m0 + D_MODEL, D_MODEL, False)
    if n_groups > 1:
        parts = []
        for p in range(N_PAIRS):
            lses = [r[0, p] for r in lse_refs]
            mx = functools.reduce(jnp.maximum, lses)
            es = [jnp.exp(v - mx) for v in lses]
            den = functools.reduce(lambda a, b: a + b, es)
            num = functools.reduce(lambda a, b: a + b,
                                   [e * r[0, p].astype(jnp.float32) for e, r in zip(es, ob_refs)])
            parts.append(num / den)
        ob = jnp.concatenate(parts, axis=1)
    else:
        ob = slabs(ob_refs[0])
    oa = slabs(oa_ref)
    ya = jnp.dot((oa * ga).astype(jnp.bfloat16), wa_ref[...], preferred_element_type=jnp.float32)
    yb = jnp.dot((ob * gb).astype(jnp.bfloat16), wb_ref[...], preferred_element_type=jnp.float32)
    merged = (ma * ya + mb * yb).astype(jnp.bfloat16)
    y_ref[0] = x + jnp.dot(merged, wo_ref[...], preferred_element_type=jnp.float32)


def _out_proj(x3d, ng, wg, oa, obs, lses, wa, wb, wo, *, tm):
    nb, seq, _ = x3d.shape
    row = lambda b, i: (b, i, 0)
    slab = pl.BlockSpec((1, N_PAIRS, tm, LANES), lambda b, i: (b, 0, i, 0))
    const = lambda b, i: (0, 0)
    once = dict(pipeline_mode=pl.Buffered(1))
    in_specs = [pl.BlockSpec((1, tm, D_MODEL), row), pl.BlockSpec((1, D_MODEL), const),
                pl.BlockSpec(wg.shape, const, **once), slab]
    in_specs += [slab for _ in obs] + [slab for _ in lses]
    in_specs += [pl.BlockSpec(wa.shape, const, **once), pl.BlockSpec(wb.shape, const, **once),
                 pl.BlockSpec(wo.shape, const, **once)]
    return pl.pallas_call(
        functools.partial(_out_kernel, n_groups=len(obs)),
        grid=(nb, seq // tm),
        in_specs=in_specs,
        out_specs=pl.BlockSpec((1, tm, D_MODEL), row),
        out_shape=jax.ShapeDtypeStruct((nb, seq, D_MODEL), jnp.float32),
        compiler_params=pltpu.CompilerParams(
            dimension_semantics=("arbitrary", "arbitrary"), vmem_limit_bytes=VMEM_LIMIT),
        name="out_proj",
    )(x3d, ng, wg, oa, *obs, *lses, wa, wb, wo)


def _position_major_kernel(*refs, n_pos):
    n_g = (len(refs) - 1) // 2
    tmp_ref = refs[-1]
    for x_ref, o_ref in zip(refs[:n_g], refs[n_g:2 * n_g]):
        rows, w = x_ref.shape
        for s in range(w // LANES):
            tmp_ref[s] = x_ref[:, s * LANES:(s + 1) * LANES]
        for tt in range(n_pos):
            for s in range(w // LANES):
                o_ref[tt, s * LANES:(s + 1) * LANES, :] = tmp_ref[s, pl.ds(tt, rows // n_pos, stride=n_pos), :].T


def _position_major(news, *, n_pos):
    rows = news[0].shape[0]
    n_seq = rows // n_pos
    blk_w = N_PAIRS * LANES
    in_specs, out_specs, out_shape = [], [], []
    for a in news:
        w = min(a.shape[1], blk_w)
        last = a.shape[1] // w - 1
        in_specs.append(pl.BlockSpec((rows, w), lambda i, last=last: (0, jnp.minimum(i, last))))
        out_specs.append(pl.BlockSpec((n_pos, w, n_seq), lambda i, last=last: (0, jnp.minimum(i, last), 0)))
        out_shape.append(jax.ShapeDtypeStruct((n_pos, a.shape[1], n_seq), jnp.float32))
    return pl.pallas_call(
        functools.partial(_position_major_kernel, n_pos=n_pos),
        grid=(max(a.shape[1] for a in news) // blk_w,),
        in_specs=in_specs,
        out_specs=out_specs,
        out_shape=out_shape,
        scratch_shapes=[pltpu.VMEM((N_PAIRS, rows, LANES), jnp.float32)],
        compiler_params=pltpu.CompilerParams(
            dimension_semantics=("arbitrary",), vmem_limit_bytes=VMEM_LIMIT),
        name="sample_states",
    )(*news)


def _w_in_kernel(w_ref, qkv_ref, gate_ref):
    offs = np.cumsum((0, A_WIDTH, A_KV_WIDTH, A_KV_WIDTH, A_WIDTH, 3 * B_WIDTH, 3 * B_WIDTH, 3 * B_WIDTH,
                      B_WIDTH, D_MODEL, D_MODEL))

    def copy(out_ref, dst, seg, perm):
        src, width = int(offs[seg]), int(offs[seg + 1] - offs[seg])
        if perm:
            for j, h in enumerate(A_HEAD_ORDER):
                out_ref[:, dst + j * HEAD_DIM:dst + (j + 1) * HEAD_DIM] = (
                    w_ref[0, :, src + h * HEAD_DIM:src + (h + 1) * HEAD_DIM].astype(out_ref.dtype))
        else:
            out_ref[:, dst:dst + width] = w_ref[0, :, src:src + width].astype(out_ref.dtype)
        return dst + width

    dst = 0
    for seg, perm in ((0, True), (4, False), (1, False), (5, False), (2, False), (6, False)):
        dst = copy(qkv_ref, dst, seg, perm)
    dst = 0
    for seg, perm in ((3, True), (7, False), (8, False), (9, False)):
        dst = copy(gate_ref, dst, seg, perm)


def _prep_w_in(w_in):
    rows = LANES
    return pl.pallas_call(
        _w_in_kernel,
        grid=(D_MODEL // rows,),
        in_specs=[pl.BlockSpec((1, rows, w_in.shape[2]), lambda i: (0, i, 0))],
        out_specs=(pl.BlockSpec((rows, QKV_COLS), lambda i: (i, 0)), pl.BlockSpec((rows, G_COLS), lambda i: (i, 0))),
        out_shape=(jax.ShapeDtypeStruct((D_MODEL, QKV_COLS), jnp.bfloat16),
                   jax.ShapeDtypeStruct((D_MODEL, G_COLS), jnp.bfloat16)),
        compiler_params=pltpu.CompilerParams(dimension_semantics=("arbitrary",), vmem_limit_bytes=VMEM_LIMIT),
        name="prep_w_in",
    )(w_in)


def _prep_params(w_in, q_gain_a, k_gain_a, q_gain_b, k_gain_b, w_up_a, w_up_b):
    w_qkv, w_gate = _prep_w_in(w_in)
    gq = jnp.broadcast_to(jnp.concatenate([q_gain_a[None], q_gain_b], axis=0)[:, None, :] * Q_SCALE,
                          (1 + N_B_GROUPS, A_Q_HEADS, HEAD_DIM)).reshape(1, Q_COLS)
    gk = jnp.concatenate([jnp.broadcast_to(k_gain_a, (A_KV_HEADS, HEAD_DIM)).reshape(A_KV_WIDTH),
                          jnp.broadcast_to(k_gain_b[:, None, :], (N_B_GROUPS, B_HEADS, HEAD_DIM)).reshape(-1)])[None]
    hd = np.arange(MXU_COLS) // HEAD_DIM
    ones = jnp.asarray((hd[:, None] == hd[None, :]).astype(np.float32) / HEAD_DIM, jnp.bfloat16)
    wa = jnp.concatenate([w_up_a[h * HEAD_DIM:(h + 1) * HEAD_DIM] for h in A_HEAD_ORDER], axis=0)
    return w_qkv, w_gate, gq, gk, ones, wa.astype(jnp.bfloat16), w_up_b.astype(jnp.bfloat16)


def kernel(x_prompt, x_sample, cache_a_kv, cache_b1_kv, cache_b2_kv, cache_b3_kv, rel_bias, norm_gain, w_in,
           q_gain_a, k_gain_a, sinks_a, q_gain_b, k_gain_b, w_up_a, w_up_b, w_out):
    assert norm_gain.shape[0] == 1, "single layer"
    nb, seq, _ = x_prompt.shape
    n_dec, dec_seq, _ = x_sample.shape
    w_qkv, w_gate, gq, gk, ones, wa, wb = _prep_params(
        w_in, q_gain_a[0], k_gain_a[0], q_gain_b[0], k_gain_b[0], w_up_a[0], w_up_b[0])
    wo = w_out[0].astype(jnp.bfloat16)
    ng = norm_gain
    windows = (A_WINDOW,) + tuple(w for w, _ in B_GROUPS)
    dils = (1,) + tuple(d for _, d in B_GROUPS)
    head_cols = (A_HEAD_ORDER,) + tuple(tuple(range(A_Q_HEADS + g * B_HEADS, A_Q_HEADS + (g + 1) * B_HEADS))
                                        for g in range(N_B_GROUPS))

    p_rows = tuple(min(w, seq) for w in windows)
    qkvs, states = _qkv_proj(x_prompt.reshape(nb * seq, D_MODEL), ng, w_qkv, gq, gk, ones,
                             nb=nb, seq=seq, state_rows=p_rows, dils=dils, tm=min(seq, PROJ_TILE_ROWS),
                             states_t=True)
    t_dec = n_dec * dec_seq
    x_s = x_sample.reshape(t_dec, D_MODEL)
    qd, news = _qkv_proj(x_s, ng, w_qkv, gq, gk, ones,
                         nb=1, seq=t_dec, state_rows=(t_dec,) * 4, dils=(1, 1, 1, 1),
                         tm=min(t_dec, SAMPLE_TILE_ROWS), states_t=False)

    q_dec = [a.reshape(t_dec, a.shape[3]) for a in qd]
    caches = [jnp.transpose(c[0], (0, 2, 3, 4, 1)) for c in (cache_a_kv, cache_b1_kv, cache_b2_kv, cache_b3_kv)]
    oa_s, ob_s, o_prompt, lses = _decode_band_attention(
        q_dec, news, caches, qkvs, rel_bias, sinks_a, dec_seq=dec_seq, dils=dils, head_cols=head_cols,
        sink_idx=A_HEAD_ORDER)

    y_prompt = _out_proj(x_prompt, ng, w_gate, o_prompt[0], list(o_prompt[1:]), list(lses), wa, wb, wo,
                         tm=min(seq, PROJ_TILE_ROWS))
    heads = (A_KV_HEADS, B_HEADS, B_HEADS, B_HEADS)
    new_prompt = tuple(jnp.transpose(states[g].reshape(1, nb, 2, heads[g], HEAD_DIM, p_rows[g]), (0, 1, 5, 2, 3, 4))
                       for g in range(4))
    to_slabs = lambda o: jnp.transpose(o, (1, 0, 2, 3)).reshape(1, N_PAIRS, t_dec, LANES)
    y_sample = _out_proj(x_s.reshape(1, t_dec, D_MODEL), ng, w_gate, to_slabs(oa_s), [to_slabs(ob_s)], [],
                         wa, wb, wo, tm=min(t_dec, SAMPLE_TILE_ROWS))
    y_sample = y_sample.reshape(n_dec, dec_seq, D_MODEL)
    new_sample = tuple(jnp.transpose(s.reshape(1, dec_seq, 2, heads[g], HEAD_DIM, n_dec), (0, 5, 1, 2, 3, 4))
                       for g, s in enumerate(_position_major(news, n_pos=dec_seq)))
    return (y_prompt, y_sample) + new_prompt + new_sample
```

```python
import functools
import math

import numpy as np
import jax
import jax.numpy as jnp
from jax import lax
from jax.experimental import pallas as pl
from jax.experimental.pallas import tpu as pltpu

D_MODEL = 1024
HEAD_DIM = 64
A_Q_HEADS = 8
A_KV_HEADS = 2
A_WINDOW = 128
B_GROUPS = ((128, 1), (512, 4), (2048, 16))
N_B_GROUPS = 3
B_HEADS = 8
N_KEYS = 128
A_WIDTH = A_Q_HEADS * HEAD_DIM
A_KV_WIDTH = A_KV_HEADS * HEAD_DIM
B_WIDTH = B_HEADS * HEAD_DIM
N_BUCKETS = 32
MAX_DISTANCE = 2048
EPS = 1e-6
NEG_INF = -1e30
Q_SCALE = HEAD_DIM ** -0.5

LANES = 128
MXU_COLS = 256
VMEM_LIMIT = 56 * 1024 * 1024
FUSED_VMEM_LIMIT = 61 * 1024 * 1024
PROJ_TILE_ROWS = 512
N_PAIRS = A_WIDTH // LANES

A_HEAD_ORDER = (0, 4, 1, 5, 2, 6, 3, 7)

Q_COLS = A_WIDTH + N_B_GROUPS * B_WIDTH
KV_COLS = A_KV_WIDTH + N_B_GROUPS * B_WIDTH
QKV_COLS = Q_COLS + 2 * KV_COLS
G_COLS = A_WIDTH + B_WIDTH + 2 * D_MODEL
KV_WIDTHS = (A_KV_WIDTH, B_WIDTH, B_WIDTH, B_WIDTH)
NORM_BATCH = 4 * B_WIDTH // MXU_COLS


def _t5_bucket_np(dist):
    max_exact = N_BUCKETS // 2
    d = np.maximum(dist, 0)
    df = np.maximum(d, 1).astype(np.float32)
    large = max_exact + (np.log(df / np.float32(max_exact)) / np.float32(math.log(MAX_DISTANCE / max_exact))
                         * np.float32(N_BUCKETS - max_exact)).astype(np.int32)
    large = np.minimum(large, N_BUCKETS - 1)
    return np.where(d < max_exact, d, large)


def _rmsnorm_bf16(x, gain):
    ms = jnp.mean(x * x, axis=-1, keepdims=True)
    return (x * lax.rsqrt(ms + EPS) * gain).astype(jnp.bfloat16)


def _state_plan(n_rows, seq, tm):
    tpb = seq // tm
    r = min(n_rows, tm)
    nblk = max(n_rows // tm, 1)
    return tpb, r, nblk, tpb - nblk


def _qkv_kernel(x_ref, ng_ref, w_ref, gq_ref, gk_ref, ones_ref, *rest, tm, seq, state_rows, dils, states_t):
    qkv_refs = rest[0:4]
    state_refs = rest[4:8]
    tmp_ref, a_scr, sq_scr, ss_scr = rest[8:12]
    h = _rmsnorm_bf16(x_ref[...], ng_ref[...])
    ones = ones_ref[...]

    def proj(col0, w):
        return jnp.dot(h, w_ref[:, col0:col0 + w], preferred_element_type=jnp.float32)

    def emit(a, c, out, d, state):
        w = a.shape[1]
        out_ref, base = out
        if d == 1:
            out_ref[0, 0, :, base + c:base + c + w] = a.astype(out_ref.dtype)
        else:
            for s in range(w // LANES):
                tmp_ref[c // LANES + s] = a[:, s * LANES:(s + 1) * LANES]
        if state is not None:
            sref, kv, r = state
            if states_t:
                sref[0, kv, c:c + w, :] = a[tm - r:, :].T
            else:
                wd = sref.shape[1] // 2
                sref[:, kv * wd + c:kv * wd + c + w] = a[tm - r:, :]

    def finish(out, d, width):
        out_ref, base = out
        if d > 1:
            for cls in range(d):
                for s in range(width // LANES):
                    out_ref[0, cls, :, base + s * LANES:base + (s + 1) * LANES] = (
                        tmp_ref[s, pl.ds(cls, tm // d, stride=d), :].astype(out_ref.dtype))

    def normed_batch(arrays):
        chunks = [(ai, c) for ai in range(len(arrays)) for c in range(0, B_WIDTH, MXU_COLS)]
        for ci, (ai, c) in enumerate(chunks):
            a = proj(arrays[ai][0] + c, MXU_COLS)
            a_scr[ci] = a
            sq_scr[ci * tm:(ci + 1) * tm, :] = (a * a).astype(jnp.bfloat16)
        n = len(chunks) * tm
        ss_scr[0:n, :] = jnp.dot(sq_scr[0:n, :], ones, preferred_element_type=jnp.float32)
        for ci, (ai, c) in enumerate(chunks):
            _, gain_ref, gcol0, out_ref, d, state = arrays[ai]
            o = (a_scr[ci] * lax.rsqrt(ss_scr[ci * tm:(ci + 1) * tm, :] + EPS)
                 * gain_ref[:, gcol0 + c:gcol0 + c + MXU_COLS])
            emit(o, c, out_ref, d, state)
            if c + MXU_COLS == B_WIDTH:
                finish(out_ref, d, B_WIDTH)

    q_off = (0, A_WIDTH, A_WIDTH + B_WIDTH, A_WIDTH + 2 * B_WIDTH)
    kv_off = (0, A_KV_WIDTH, A_KV_WIDTH + B_WIDTH, A_KV_WIDTH + 2 * B_WIDTH)
    plan = [_state_plan(state_rows[g], seq, tm)[1] for g in range(4)]
    k_out = [(qkv_refs[g], A_WIDTH) for g in range(4)]
    v_out = [(qkv_refs[g], A_WIDTH + KV_WIDTHS[g]) for g in range(4)]
    normed_batch([(q_off[g], gq_ref, q_off[g], (qkv_refs[g], 0), dils[g], None) for g in range(4)])
    normed_batch([(Q_COLS + kv_off[g], gk_ref, kv_off[g], k_out[g], dils[g], (state_refs[g], 0, plan[g]))
                  for g in range(1, 4)])
    a = proj(Q_COLS, A_KV_WIDTH)
    ss = jnp.dot((a * a).astype(jnp.bfloat16), ones[:A_KV_WIDTH, :A_KV_WIDTH], preferred_element_type=jnp.float32)
    emit(a * lax.rsqrt(ss + EPS) * gk_ref[:, 0:A_KV_WIDTH], 0, k_out[0], dils[0], (state_refs[0], 0, plan[0]))
    for g in range(4):
        wd = KV_WIDTHS[g]
        for c in range(0, wd, MXU_COLS):
            w = min(MXU_COLS, wd - c)
            emit(proj(Q_COLS + KV_COLS + kv_off[g] + c, w), c, v_out[g], dils[g], (state_refs[g], 1, plan[g]))
        finish(v_out[g], dils[g], wd)


def _qkv_proj(x2d, ng, w_qkv, gq, gk, ones, *, nb, seq, state_rows, dils, tm, states_t):
    t = x2d.shape[0]
    tpb = seq // tm

    def cls_shape(d, width):
        return jax.ShapeDtypeStruct((nb, d, seq // d, width), jnp.bfloat16)

    def cls_spec(d, width):
        return pl.BlockSpec((1, d, tm // d, width), lambda i: (i // tpb, 0, i % tpb, 0))

    def state_shape(g):
        if states_t:
            return jax.ShapeDtypeStruct((nb, 2, KV_WIDTHS[g], state_rows[g]), jnp.float32)
        return jax.ShapeDtypeStruct((nb * state_rows[g], 2 * KV_WIDTHS[g]), jnp.float32)

    def state_spec(g):
        _, r, nblk, j0 = _state_plan(state_rows[g], seq, tm)
        if states_t:
            return pl.BlockSpec((1, 2, KV_WIDTHS[g], r), lambda i: (i // tpb, 0, 0, jnp.maximum(i % tpb - j0, 0)))
        return pl.BlockSpec((r, 2 * KV_WIDTHS[g]),
                            lambda i: ((i // tpb) * nblk + jnp.maximum(i % tpb - j0, 0), 0))

    const = lambda i: (0, 0)
    out_shape = ([cls_shape(d, A_WIDTH + 2 * w) for d, w in zip(dils, KV_WIDTHS)]
                 + [state_shape(g) for g in range(4)])
    out_specs = ([cls_spec(d, A_WIDTH + 2 * w) for d, w in zip(dils, KV_WIDTHS)]
                 + [state_spec(g) for g in range(4)])
    res = pl.pallas_call(
        functools.partial(_qkv_kernel, tm=tm, seq=seq, state_rows=state_rows, dils=dils, states_t=states_t),
        grid=(t // tm,),
        in_specs=[
            pl.BlockSpec((tm, D_MODEL), lambda i: (i, 0)),
            pl.BlockSpec((1, D_MODEL), const),
            pl.BlockSpec((D_MODEL, QKV_COLS), const, pipeline_mode=pl.Buffered(1)),
            pl.BlockSpec((1, Q_COLS), const),
            pl.BlockSpec((1, KV_COLS), const),
            pl.BlockSpec((MXU_COLS, MXU_COLS), const),
        ],
        out_specs=out_specs,
        out_shape=out_shape,
        scratch_shapes=[pltpu.VMEM((N_PAIRS, tm, LANES), jnp.float32),
                        pltpu.VMEM((NORM_BATCH, tm, MXU_COLS), jnp.float32),
                        pltpu.VMEM((NORM_BATCH * tm, MXU_COLS), jnp.bfloat16),
                        pltpu.VMEM((NORM_BATCH * tm, MXU_COLS), jnp.float32)],
        compiler_params=pltpu.CompilerParams(
            dimension_semantics=("arbitrary",), vmem_limit_bytes=VMEM_LIMIT),
        name="qkv_proj",
    )(x2d, ng, w_qkv, gq, gk, ones)
    return res[0:4], res[4:8]


def _lookup_rows(bucket_map, top, table_ref, col_top, col_bottom, fill):
    acc = jnp.full(bucket_map.shape, fill, jnp.float32)
    for b in range(N_BUCKETS):
        val = jnp.where(top, table_ref[b, col_top], table_ref[b, col_bottom])
        acc = jnp.where(bucket_map == b, val, acc)
    return acc


def _band_block(qkv_ref, kp_ref, vp_ref, o_ref, o_stage, lse_ref, bias_ref, sink_ref, *,
                cls, first, dil, shared_kv):
    blk = N_KEYS
    row = lax.broadcasted_iota(jnp.int32, (2 * blk, blk), 0)
    lane = lax.broadcasted_iota(jnp.int32, (2 * blk, blk), 1)
    ahead = lane - jnp.bitwise_and(row, blk - 1)
    tri = ahead <= 0
    valid = ahead <= jnp.where(first, 0, blk)
    cur_part = tri.astype(jnp.bfloat16)
    prev_part = jnp.logical_not(tri).astype(jnp.bfloat16)
    left = lax.broadcasted_iota(jnp.int32, (blk, LANES), 1) < HEAD_DIM
    mask_l = left.astype(jnp.bfloat16)
    mask_r = jnp.logical_not(left).astype(jnp.bfloat16)
    ones_kv = jnp.ones((2 * blk, LANES), jnp.bfloat16)
    nt = (((1,), (1,)), ((), ()))
    rows = pl.ds(0, blk) if dil == 1 else pl.ds(cls, blk, stride=dil)
    wkv = kp_ref.shape[2]
    k0, v0 = A_WIDTH, A_WIDTH + wkv
    for p in range(N_PAIRS):
        kcol = 0 if shared_kv else p * LANES
        qp = qkv_ref[0, 0, :, p * LANES:(p + 1) * LANES]
        qs = jnp.concatenate([qp * mask_l, qp * mask_r], axis=0)
        kc = jnp.concatenate([kp_ref[cls, :, kcol:kcol + LANES],
                              qkv_ref[0, 0, :, k0 + kcol:k0 + kcol + LANES]], axis=0)
        vc = jnp.concatenate([vp_ref[cls, :, kcol:kcol + LANES],
                              qkv_ref[0, 0, :, v0 + kcol:v0 + kcol + LANES]], axis=0)
        s2 = lax.dot_general(qs, kc, nt, preferred_element_type=jnp.float32)
        s = jnp.where(tri, s2[:, blk:], s2[:, :blk]) + bias_ref[p]
        s = jnp.where(valid, s, NEG_INF)
        m = jnp.max(s, axis=-1, keepdims=True)
        if sink_ref is not None:
            sk = sink_ref[p]
            m = jnp.maximum(m, sk)
        pb = jnp.exp(s - m).astype(jnp.bfloat16)
        p2 = jnp.concatenate([pb * prev_part, pb * cur_part], axis=1)
        ov = jnp.dot(p2, jnp.concatenate([vc, ones_kv], axis=1), preferred_element_type=jnp.float32)
        l = ov[:, LANES:]
        if sink_ref is not None:
            l = l + jnp.exp(sk - m)
        num = jnp.where(left, ov[:blk, :LANES], ov[blk:, :LANES])
        den = jnp.where(left, l[:blk], l[blk:])
        if dil == 1:
            o_ref[0, p, rows, :] = (num / den).astype(o_ref.dtype)
        else:
            o_stage[p, rows, :] = num / den
        if lse_ref is not None:
            m1 = jnp.where(left, m[:blk], m[blk:])
            lse_ref[0, p, rows, :] = m1 + jnp.log(den)
    kp_ref[cls] = qkv_ref[0, 0, :, k0:k0 + wkv]
    vp_ref[cls] = qkv_ref[0, 0, :, v0:v0 + wkv]


def _decode_kernel(qa_ref, qb1_ref, qb2_ref, qb3_ref, na_ref, nb1_ref, nb2_ref, nb3_ref,
                   ca_ref, cb1_ref, cb2_ref, cb3_ref, ma_ref, mb1_ref, mb2_ref, mb3_ref, mn_ref, relb_ref, sinks_ref,
                   oa_ref, ob_ref, ta_ref, tb1_ref, tb2_ref, tb3_ref, tn_ref, sink_ref,
                   *, dec_seq, head_cols, sink_idx, dils, init):
    t = dec_seq
    if init:
        for g, (m_ref, t_ref) in enumerate(zip((ma_ref, mb1_ref, mb2_ref, mb3_ref),
                                               (ta_ref, tb1_ref, tb2_ref, tb3_ref))):
            top = lax.broadcasted_iota(jnp.int32, m_ref.shape, 0) < t
            top_n = lax.broadcasted_iota(jnp.int32, (2 * t, LANES), 0) < t
            for p in range(N_PAIRS):
                cols = (head_cols[g][2 * p], head_cols[g][2 * p + 1])
                rows = slice(p * 2 * t, (p + 1) * 2 * t)
                t_ref[rows, :] = _lookup_rows(m_ref[...], top, relb_ref, cols[0], cols[1], NEG_INF)
                tn_ref[g, rows, :] = _lookup_rows(mn_ref[g], top_n, relb_ref, cols[0], cols[1], NEG_INF)
                if g == 0:
                    sink_ref[rows, :] = jnp.where(top_n, sinks_ref[0, sink_idx[2 * p]],
                                                  sinks_ref[0, sink_idx[2 * p + 1]])
        return

    left8 = lax.broadcasted_iota(jnp.int32, (2 * t, LANES), 1) < HEAD_DIM
    top8 = lax.broadcasted_iota(jnp.int32, (2 * t, LANES), 0) < t
    own = (left8 == top8).astype(jnp.float32)
    left4 = lax.broadcasted_iota(jnp.int32, (t, LANES), 1) < HEAD_DIM
    nt = (((1,), (1,)), ((), ()))

    seq_id = pl.program_id(0)

    def seq_rows(block):
        k = block.shape[0] // t
        which = seq_id % k
        rows = block[0:t]
        for j in range(1, k):
            rows = jnp.where(which == j, block[j * t:(j + 1) * t], rows)
        return rows

    q_refs = (qa_ref, qb1_ref, qb2_ref, qb3_ref)

    def q_rows(g, p):
        qp = seq_rows(q_refs[g][:, p * LANES:(p + 1) * LANES].astype(jnp.float32))
        return jnp.concatenate([qp, qp], axis=0) * own

    def attend(qrs, kts, vts, knews, vnews, tbl, tbl_new, sink, dil):
        shared = len(kts) == 1
        rows8 = lambda a, u: jnp.broadcast_to(a[u:u + 1, :], (2 * t, LANES))
        qr = jnp.concatenate(qrs, axis=0)
        qb = qr.astype(jnp.bfloat16)
        if shared:
            s_c = jnp.dot(qb, kts[0], preferred_element_type=jnp.float32)
        else:
            s_c = jnp.concatenate([jnp.dot(qb[2 * t * p:2 * t * (p + 1)], kts[p], preferred_element_type=jnp.float32)
                                   for p in range(N_PAIRS)], axis=0)
        yield
        s_c = s_c + tbl
        kn = [jnp.concatenate([rows8(knews[0 if shared else p], u) for p in range(N_PAIRS)], axis=0)
              for u in range(t)]
        vn = [jnp.concatenate([rows8(vnews[0 if shared else p], u) for p in range(N_PAIRS)], axis=0)
              for u in range(t)]
        s_n = [jnp.sum(qr * kn[u], axis=-1, keepdims=True) + tbl_new[:, u:u + 1] for u in range(t)]
        m = functools.reduce(jnp.maximum, s_n, jnp.max(s_c, axis=-1, keepdims=True))
        if sink is not None:
            m = jnp.maximum(m, sink)
        pc = jnp.exp(s_c - m)
        pn = [jnp.exp(x - m) for x in s_n]
        l = functools.reduce(lambda a, b: a + b, pn, jnp.sum(pc, axis=-1, keepdims=True))
        if sink is not None:
            l = l + jnp.exp(sink - m)
        pb = pc.astype(jnp.bfloat16)
        yield
        if shared:
            o = lax.dot_general(pb, vts[0], nt, preferred_element_type=jnp.float32)
        elif dil >= t:
            lane8 = lax.broadcasted_iota(jnp.int32, (2 * t, LANES), 1)
            row8 = lax.broadcasted_iota(jnp.int32, (2 * t, LANES), 0)
            sel = (lane8 % dil == row8 % t).astype(jnp.bfloat16)
            parts = []
            for p in range(N_PAIRS):
                accs = []
                for side in range(2):
                    r0 = 2 * t * p + t * side
                    w = jnp.sum(pc[r0:r0 + t], axis=0, keepdims=True)
                    v = vts[p][side * HEAD_DIM:(side + 1) * HEAD_DIM]
                    accs.append(functools.reduce(
                        lambda a, b: a + b,
                        [v[:, c:c + LANES] * w[:, c:c + LANES] for c in range(0, v.shape[1], LANES)]))
                acc = jnp.concatenate(accs, axis=0)
                hi = acc.astype(jnp.bfloat16)
                lo = (acc - hi.astype(jnp.float32)).astype(jnp.bfloat16)
                parts.append(lax.dot_general(sel, hi, nt, preferred_element_type=jnp.float32)
                             + lax.dot_general(sel, lo, nt, preferred_element_type=jnp.float32))
            o = jnp.concatenate(parts, axis=0)
        else:
            o = jnp.concatenate([lax.dot_general(pb[2 * t * p:2 * t * (p + 1)], vts[p], nt,
                                                 preferred_element_type=jnp.float32)
                                 for p in range(N_PAIRS)], axis=0)
        for u in range(t):
            o = o + pn[u] * vn[u]
        yield o / l, m + jnp.log(l)

    def fold(x, p):
        return jnp.where(left4, x[2 * t * p:2 * t * p + t], x[2 * t * p + t:2 * t * (p + 1)])

    def cache_kv(c_ref, kv, p, dtype=jnp.bfloat16):
        if isinstance(c_ref, tuple):
            c_ref, kv = c_ref[kv], 0
        return c_ref[0, kv, 2 * p:2 * p + 2].reshape(LANES, c_ref.shape[-1]).astype(dtype)

    caches = (cb1_ref, cb2_ref, cb3_ref)
    news = (nb1_ref, nb2_ref, nb3_ref)
    tbls = (tb1_ref, tb2_ref, tb3_ref)
    new_a = seq_rows(na_ref[...])
    groups = [attend([q_rows(0, p) for p in range(N_PAIRS)],
                     [cache_kv(ca_ref, 0, 0)], [cache_kv(ca_ref, 1, 0)],
                     [new_a[:, 0:LANES]], [new_a[:, LANES:2 * LANES]],
                     ta_ref[...], tn_ref[0], sink_ref[:, 0:1], dils[0])]
    for g in range(N_B_GROUPS):
        new_g = seq_rows(news[g][...])
        groups.append(attend(
            [q_rows(1 + g, p) for p in range(N_PAIRS)],
            [cache_kv(caches[g], 0, p) for p in range(N_PAIRS)],
            [cache_kv(caches[g], 1, p, jnp.float32 if dils[1 + g] >= t else jnp.bfloat16) for p in range(N_PAIRS)],
            [new_g[:, p * LANES:(p + 1) * LANES] for p in range(N_PAIRS)],
            [new_g[:, B_WIDTH + p * LANES:B_WIDTH + (p + 1) * LANES] for p in range(N_PAIRS)],
            tbls[g][...], tn_ref[1 + g], None, dils[1 + g]))
    for phase in range(2):
        for gen in groups:
            next(gen)
    results = [next(gen) for gen in groups]
    oa = results[0][0]
    for p in range(N_PAIRS):
        oa_ref[0, p] = fold(oa, p).astype(oa_ref.dtype)
    outs = [o for o, _ in results[1:]]
    lses = [jnp.broadcast_to(lse, o.shape) for o, lse in results[1:]]
    mx = jnp.maximum(jnp.maximum(lses[0], lses[1]), lses[2])
    es = [jnp.exp(x - mx) for x in lses]
    den = es[0] + es[1] + es[2]
    comb = (es[0] * outs[0] + es[1] * outs[1] + es[2] * outs[2]) / den
    for p in range(N_PAIRS):
        ob_ref[0, p] = fold(comb, p).astype(ob_ref.dtype)


def _decode_bucket_maps(dils, cache_lens, dec_seq):
    t_idx = np.arange(dec_seq)
    maps = []
    new = np.full((len(dils), 2 * dec_seq, LANES), -1, np.int32)
    for g, (d, ln) in enumerate(zip(dils, cache_lens)):
        back = ln + t_idx[:, None] - np.arange(ln)[None, :]
        ok = (back % d == 0) & (back // d >= 1) & (back // d < N_KEYS)
        maps.append(jnp.asarray(np.tile(np.where(ok, _t5_bucket_np(back), -1), (2, 1)).astype(np.int32)))
        backn = t_idx[:, None] - t_idx[None, :]
        okn = (backn >= 0) & (backn % d == 0) & (backn // d < N_KEYS)
        new[g, :, :dec_seq] = np.tile(np.where(okn, _t5_bucket_np(backn), -1), (2, 1))
    return maps, jnp.asarray(new)


def _decode_band_kernel(*refs, dec_seq, head_cols, sink_idx, dils, blocks_per_batch):
    n_g = len(dils)
    n_dec_in = 4 * n_g + 3
    dec_in = refs[0:n_dec_in]
    relb_ref, sinks_ref = dec_in[-2:]
    band_in = [refs[n_dec_in + 2 * g:n_dec_in + 2 * (g + 1)] for g in range(n_g)]
    outs = refs[n_dec_in + 2 * n_g:n_dec_in + 2 * n_g + 2 + 2 * n_g - 1]
    oa_ref, ob_ref = outs[0:2]
    o_refs = outs[2:2 + n_g]
    lse_refs = (None,) + tuple(outs[2 + n_g:])
    scratch = refs[n_dec_in + 2 * n_g + 2 + 2 * n_g - 1:-4]
    kring, vring, ksem, vsem = refs[-4:]
    dec_scratch = scratch[0:6]
    bias_refs = scratch[6:6 + n_g]
    sink_tile = scratch[6 + n_g]
    kprev_refs = scratch[7 + n_g:7 + 2 * n_g]
    vprev_refs = scratch[7 + 2 * n_g:7 + 3 * n_g]
    stages = iter(scratch[7 + 3 * n_g:])
    stage_refs = [None if d == 1 else next(stages) for d in dils]
    blk = N_KEYS
    s = pl.program_id(0)
    n_steps = pl.num_programs(0)
    big = 3 * n_g - 1
    big_hbm = dec_in[big]
    k_slots, v_slots = kring.shape[0], vring.shape[0]

    def k_copy(i):
        return pltpu.make_async_copy(big_hbm.at[pl.ds(i, 1), pl.ds(0, 1)], kring.at[i % k_slots],
                                     ksem.at[i % k_slots])

    def v_copy(i):
        return pltpu.make_async_copy(big_hbm.at[pl.ds(i, 1), pl.ds(1, 1)], vring.at[i % v_slots],
                                     vsem.at[i % v_slots])

    dec_in = list(dec_in)
    dec_in[big] = (kring.at[s % k_slots], vring.at[s % v_slots])
    decode = functools.partial(_decode_kernel, *dec_in, oa_ref, ob_ref, *dec_scratch, dec_seq=dec_seq,
                               head_cols=head_cols, sink_idx=sink_idx, dils=dils)

    @pl.when(s == 0)
    def _():
        k_copy(0).start()
        k_copy(1).start()
        v_copy(0).start()
        top = lax.broadcasted_iota(jnp.int32, (2 * blk, blk), 0) < blk
        for g in range(n_g):
            bmap = band_in[g][1][...]
            for p in range(N_PAIRS):
                bias_refs[g][p] = _lookup_rows(bmap, top, relb_ref, head_cols[g][2 * p], head_cols[g][2 * p + 1],
                                               0.0)
            kprev_refs[g][...] = jnp.zeros(kprev_refs[g].shape, kprev_refs[g].dtype)
            vprev_refs[g][...] = jnp.zeros(vprev_refs[g].shape, vprev_refs[g].dtype)
        for p in range(N_PAIRS):
            sink_tile[p] = jnp.where(top, sinks_ref[0, sink_idx[2 * p]], sinks_ref[0, sink_idx[2 * p + 1]])
        decode(init=True)

    r = s % blocks_per_batch

    @pl.when(s + 2 < n_steps)
    def _():
        k_copy(s + 2).start()

    @pl.when(s + 1 < n_steps)
    def _():
        v_copy(s + 1).start()

    k_copy(s).wait()
    v_copy(s).wait()
    decode(init=False)
    for g in range(n_g):
        _band_block(band_in[g][0], kprev_refs[g], vprev_refs[g], o_refs[g], stage_refs[g], lse_refs[g],
                    bias_refs[g], sink_tile if g == 0 else None, cls=r % dils[g], first=(r // dils[g]) == 0,
                    dil=dils[g], shared_kv=(g == 0))

    for g in range(n_g):
        if dils[g] > 1:
            @pl.when(r % dils[g] == dils[g] - 1)
            def _():
                for p in range(N_PAIRS):
                    o_refs[g][0, p] = stage_refs[g][p].astype(o_refs[g].dtype)


def _decode_band_attention(q_dec, news, caches, qkvs, rel_bias, sinks, *, dec_seq, dils, head_cols, sink_idx):
    t = dec_seq
    n = q_dec[0].shape[0] // t
    nb = qkvs[0].shape[0]
    seq = qkvs[0].shape[1] * qkvs[0].shape[2]
    ppb = seq // N_KEYS
    assert n == nb * ppb, "one prompt block of every group per sample sequence"
    seq3 = lambda i: (i, 0, 0)
    seq4 = lambda i: (i, 0, 0, 0)
    seq5 = lambda i: (i, 0, 0, 0, 0)
    cache_lens = tuple(c.shape[-1] for c in caches)
    maps, map_new = _decode_bucket_maps(dils, cache_lens, t)
    smem = functools.partial(pl.BlockSpec, memory_space=pltpu.SMEM)
    q_rows, new_rows = 16, 8
    in_specs = [pl.BlockSpec((q_rows, A_WIDTH), lambda i: (i // (q_rows // t), 0)) for _ in q_dec]
    in_specs += [pl.BlockSpec((new_rows, a.shape[1]), lambda i: (i // (new_rows // t), 0)) for a in news]
    assert n >= 2, "the hand-streamed cache prefetches two sequences ahead"
    in_specs += [pl.BlockSpec((1,) + c.shape[1:], seq5) for c in caches[:-1]]
    in_specs += [pl.BlockSpec(memory_space=pl.ANY)]
    half = (1, 1) + caches[-1].shape[2:]
    in_specs += [pl.BlockSpec(m.shape, lambda i: (0, 0)) for m in maps]
    in_specs += [pl.BlockSpec(map_new.shape, lambda i: (0, 0, 0)), smem(), smem()]
    band_args = []
    out_specs = [pl.BlockSpec((1, N_PAIRS, t, LANES), seq4)] * 2
    out_shape = [jax.ShapeDtypeStruct((n, N_PAIRS, t, LANES), jnp.float32)] * 2
    qi = np.arange(N_KEYS)[:, None]
    ci = np.arange(N_KEYS)[None, :]
    for g, d in enumerate(dils):
        cur = lambda s, d=d: (s // ppb, (s % ppb) % d, (s % ppb) // d, 0)
        bmap = jnp.asarray(np.tile(_t5_bucket_np(((qi - ci) % N_KEYS) * d), (2, 1)).astype(np.int32))
        in_specs += [pl.BlockSpec((1, 1, N_KEYS, qkvs[g].shape[3]), cur), pl.BlockSpec(bmap.shape, lambda s: (0, 0))]
        band_args += [qkvs[g], bmap]
    tile = lambda d: pl.BlockSpec((1, N_PAIRS, N_KEYS * d, LANES), lambda s, d=d: (s // ppb, 0, (s % ppb) // d, 0))
    slab_shape = lambda dtype: jax.ShapeDtypeStruct((nb, N_PAIRS, seq, LANES), dtype)
    out_specs += [tile(d) for d in dils] + [tile(d) for d in dils[1:]]
    out_shape += [slab_shape(jnp.bfloat16)] * len(dils) + [slab_shape(jnp.float32)] * (len(dils) - 1)
    n_rows = 2 * t * N_PAIRS
    tile_scr = pltpu.VMEM((N_PAIRS, 2 * N_KEYS, N_KEYS), jnp.float32)
    res = pl.pallas_call(
        functools.partial(_decode_band_kernel, dec_seq=t, head_cols=head_cols, sink_idx=sink_idx, dils=dils,
                          blocks_per_batch=ppb),
        grid=(n,),
        in_specs=in_specs,
        out_specs=out_specs,
        out_shape=out_shape,
        scratch_shapes=[pltpu.VMEM((n_rows, ln), jnp.float32) for ln in cache_lens]
        + [pltpu.VMEM((len(dils), n_rows, LANES), jnp.float32), pltpu.VMEM((n_rows, LANES), jnp.float32)]
        + [tile_scr] * (len(dils) + 1)
        + [pltpu.VMEM((d, N_KEYS, w), jnp.bfloat16) for d, w in zip(dils, KV_WIDTHS)] * 2
        + [pltpu.VMEM((N_PAIRS, N_KEYS * d, LANES), jnp.float32) for d in dils if d > 1]
        + [pltpu.VMEM((3,) + half, jnp.float32), pltpu.VMEM((2,) + half, jnp.float32),
           pltpu.SemaphoreType.DMA((3,)), pltpu.SemaphoreType.DMA((2,))],
        compiler_params=pltpu.CompilerParams(
            dimension_semantics=("arbitrary",), vmem_limit_bytes=FUSED_VMEM_LIMIT),
        name="decode_band_attn",
    )(*q_dec, *news, *caches, *maps, map_new, rel_bias, sinks, *band_args)
    ng = len(dils)
    return res[0], res[1], res[2:2 + ng], res[2 + ng:]


def _out_kernel(*refs, n_groups):
    x_ref, ng_ref, wg_ref, oa_ref = refs[0:4]
    ob_refs = refs[4:4 + n_groups]
    lse_refs = refs[4 + n_groups:4 + 2 * n_groups] if n_groups > 1 else ()
    wa_ref, wb_ref, wo_ref, y_ref = refs[-4:]
    x = x_ref[0]
    h = _rmsnorm_bf16(x, ng_ref[...])

    def gate(c0, width, silu):
        a = jnp.dot(h, wg_ref[:, c0:c0 + width], preferred_element_type=jnp.float32)
        sg = 0.5 * jnp.tanh(0.5 * a) + 0.5
        return a * sg if silu else sg

    def slabs(ref):
        return jnp.concatenate([ref[0, p].astype(jnp.float32) for p in range(N_PAIRS)], axis=1)

    m0 = A_WIDTH + B_WIDTH
    ga = gate(0, A_WIDTH, True)
    gb = gate(A_WIDTH, B_WIDTH, True)
    ma = gate(m0, D_MODEL, False)
    mb = gate(m0 + D_MODEL, D_MODEL, False)
    if n_groups > 1:
        parts = []
        for p in range(N_PAIRS):
            lses = [r[0, p] for r in lse_refs]
            mx = functools.reduce(jnp.maximum, lses)
            es = [jnp.exp(v - mx) for v in lses]
            den = functools.reduce(lambda a, b: a + b, es)
            num = functools.reduce(lambda a, b: a + b,
                                   [e * r[0, p].astype(jnp.float32) for e, r in zip(es, ob_refs)])
            parts.append(num / den)
        ob = jnp.concatenate(parts, axis=1)
    else:
        ob = slabs(ob_refs[0])
    oa = slabs(oa_ref)
    ya = jnp.dot((oa * ga).astype(jnp.bfloat16), wa_ref[...], preferred_element_type=jnp.float32)
    yb = jnp.dot((ob * gb).astype(jnp.bfloat16), wb_ref[...], preferred_element_type=jnp.float32)
    merged = (ma * ya + mb * yb).astype(jnp.bfloat16)
    y_ref[0] = x + jnp.dot(merged, wo_ref[...], preferred_element_type=jnp.float32)


def _out_proj(x3d, ng, wg, oa, obs, lses, wa, wb, wo, *, tm):
    nb, seq, _ = x3d.shape
    row = lambda b, i: (b, i, 0)
    slab = pl.BlockSpec((1, N_PAIRS, tm, LANES), lambda b, i: (b, 0, i, 0))
    const = lambda b, i: (0, 0)
    once = dict(pipeline_mode=pl.Buffered(1))
    in_specs = [pl.BlockSpec((1, tm, D_MODEL), row), pl.BlockSpec((1, D_MODEL), const),
                pl.BlockSpec(wg.shape, const, **once), slab]
    in_specs += [slab for _ in obs] + [slab for _ in lses]
    in_specs += [pl.BlockSpec(wa.shape, const, **once), pl.BlockSpec(wb.shape, const, **once),
                 pl.BlockSpec(wo.shape, const, **once)]
    return pl.pallas_call(
        functools.partial(_out_kernel, n_groups=len(obs)),
        grid=(nb, seq // tm),
        in_specs=in_specs,
        out_specs=pl.BlockSpec((1, tm, D_MODEL), row),
        out_shape=jax.ShapeDtypeStruct((nb, seq, D_MODEL), jnp.float32),
        compiler_params=pltpu.CompilerParams(
            dimension_semantics=("arbitrary", "arbitrary"), vmem_limit_bytes=VMEM_LIMIT),
        name="out_proj",
    )(x3d, ng, wg, oa, *obs, *lses, wa, wb, wo)


def _position_major_kernel(*refs, n_pos):
    n_g = (len(refs) - 1) // 2
    tmp_ref = refs[-1]
    for x_ref, o_ref in zip(refs[:n_g], refs[n_g:2 * n_g]):
        rows, w = x_ref.shape
        for s in range(w // LANES):
            tmp_ref[s] = x_ref[:, s * LANES:(s + 1) * LANES]
        for tt in range(n_pos):
            for s in range(w // LANES):
                o_ref[tt, s * LANES:(s + 1) * LANES, :] = tmp_ref[s, pl.ds(tt, rows // n_pos, stride=n_pos), :].T


def _position_major(news, *, n_pos):
    rows = news[0].shape[0]
    n_seq = rows // n_pos
    blk_w = N_PAIRS * LANES
    in_specs, out_specs, out_shape = [], [], []
    for a in news:
        w = min(a.shape[1], blk_w)
        last = a.shape[1] // w - 1
        in_specs.append(pl.BlockSpec((rows, w), lambda i, last=last: (0, jnp.minimum(i, last))))
        out_specs.append(pl.BlockSpec((n_pos, w, n_seq), lambda i, last=last: (0, jnp.minimum(i, last), 0)))
        out_shape.append(jax.ShapeDtypeStruct((n_pos, a.shape[1], n_seq), jnp.float32))
    return pl.pallas_call(
        functools.partial(_position_major_kernel, n_pos=n_pos),
        grid=(max(a.shape[1] for a in news) // blk_w,),
        in_specs=in_specs,
        out_specs=out_specs,
        out_shape=out_shape,
        scratch_shapes=[pltpu.VMEM((N_PAIRS, rows, LANES), jnp.float32)],
        compiler_params=pltpu.CompilerParams(
            dimension_semantics=("arbitrary",), vmem_limit_bytes=VMEM_LIMIT),
        name="sample_states",
    )(*news)


def _w_in_kernel(w_ref, qkv_ref, gate_ref):
    offs = np.cumsum((0, A_WIDTH, A_KV_WIDTH, A_KV_WIDTH, A_WIDTH, 3 * B_WIDTH, 3 * B_WIDTH, 3 * B_WIDTH,
                      B_WIDTH, D_MODEL, D_MODEL))

    def copy(out_ref, dst, seg, perm):
        src, width = int(offs[seg]), int(offs[seg + 1] - offs[seg])
        if perm:
            for j, h in enumerate(A_HEAD_ORDER):
                out_ref[:, dst + j * HEAD_DIM:dst + (j + 1) * HEAD_DIM] = (
                    w_ref[0, :, src + h * HEAD_DIM:src + (h + 1) * HEAD_DIM].astype(out_ref.dtype))
        else:
            out_ref[:, dst:dst + width] = w_ref[0, :, src:src + width].astype(out_ref.dtype)
        return dst + width

    dst = 0
    for seg, perm in ((0, True), (4, False), (1, False), (5, False), (2, False), (6, False)):
        dst = copy(qkv_ref, dst, seg, perm)
    dst = 0
    for seg, perm in ((3, True), (7, False), (8, False), (9, False)):
        dst = copy(gate_ref, dst, seg, perm)


def _prep_w_in(w_in):
    rows = LANES
    return pl.pallas_call(
        _w_in_kernel,
        grid=(D_MODEL // rows,),
        in_specs=[pl.BlockSpec((1, rows, w_in.shape[2]), lambda i: (0, i, 0))],
        out_specs=(pl.BlockSpec((rows, QKV_COLS), lambda i: (i, 0)), pl.BlockSpec((rows, G_COLS), lambda i: (i, 0))),
        out_shape=(jax.ShapeDtypeStruct((D_MODEL, QKV_COLS), jnp.bfloat16),
                   jax.ShapeDtypeStruct((D_MODEL, G_COLS), jnp.bfloat16)),
        compiler_params=pltpu.CompilerParams(dimension_semantics=("arbitrary",), vmem_limit_bytes=VMEM_LIMIT),
        name="prep_w_in",
    )(w_in)


def _prep_params(w_in, q_gain_a, k_gain_a, q_gain_b, k_gain_b, w_up_a, w_up_b):
    w_qkv, w_gate = _prep_w_in(w_in)
    gq = jnp.broadcast_to(jnp.concatenate([q_gain_a[None], q_gain_b], axis=0)[:, None, :] * Q_SCALE,
                          (1 + N_B_GROUPS, A_Q_HEADS, HEAD_DIM)).reshape(1, Q_COLS)
    gk = jnp.concatenate([jnp.broadcast_to(k_gain_a, (A_KV_HEADS, HEAD_DIM)).reshape(A_KV_WIDTH),
                          jnp.broadcast_to(k_gain_b[:, None, :], (N_B_GROUPS, B_HEADS, HEAD_DIM)).reshape(-1)])[None]
    hd = np.arange(MXU_COLS) // HEAD_DIM
    ones = jnp.asarray((hd[:, None] == hd[None, :]).astype(np.float32) / HEAD_DIM, jnp.bfloat16)
    wa = jnp.concatenate([w_up_a[h * HEAD_DIM:(h + 1) * HEAD_DIM] for h in A_HEAD_ORDER], axis=0)
    return w_qkv, w_gate, gq, gk, ones, wa.astype(jnp.bfloat16), w_up_b.astype(jnp.bfloat16)


def kernel(x_prompt, x_sample, cache_a_kv, cache_b1_kv, cache_b2_kv, cache_b3_kv, rel_bias, norm_gain, w_in,
           q_gain_a, k_gain_a, sinks_a, q_gain_b, k_gain_b, w_up_a, w_up_b, w_out):
    assert norm_gain.shape[0] == 1, "single layer"
    nb, seq, _ = x_prompt.shape
    n_dec, dec_seq, _ = x_sample.shape
    w_qkv, w_gate, gq, gk, ones, wa, wb = _prep_params(
        w_in, q_gain_a[0], k_gain_a[0], q_gain_b[0], k_gain_b[0], w_up_a[0], w_up_b[0])
    wo = w_out[0].astype(jnp.bfloat16)
    ng = norm_gain
    windows = (A_WINDOW,) + tuple(w for w, _ in B_GROUPS)
    dils = (1,) + tuple(d for _, d in B_GROUPS)
    head_cols = (A_HEAD_ORDER,) + tuple(tuple(range(A_Q_HEADS + g * B_HEADS, A_Q_HEADS + (g + 1) * B_HEADS))
                                        for g in range(N_B_GROUPS))

    p_rows = tuple(min(w, seq) for w in windows)
    qkvs, states = _qkv_proj(x_prompt.reshape(nb * seq, D_MODEL), ng, w_qkv, gq, gk, ones,
                             nb=nb, seq=seq, state_rows=p_rows, dils=dils, tm=min(seq, PROJ_TILE_ROWS),
                             states_t=True)
    t_dec = n_dec * dec_seq
    qd, news = _qkv_proj(x_sample.reshape(t_dec, D_MODEL), ng, w_qkv, gq, gk, ones,
                         nb=1, seq=t_dec, state_rows=(t_dec,) * 4, dils=(1, 1, 1, 1),
                         tm=min(t_dec, PROJ_TILE_ROWS), states_t=False)

    q_dec = [a.reshape(t_dec, a.shape[3]) for a in qd]
    caches = [jnp.transpose(c[0], (0, 2, 3, 4, 1)) for c in (cache_a_kv, cache_b1_kv, cache_b2_kv, cache_b3_kv)]
    oa_s, ob_s, o_prompt, lses = _decode_band_attention(
        q_dec, news, caches, qkvs, rel_bias, sinks_a, dec_seq=dec_seq, dils=dils, head_cols=head_cols,
        sink_idx=A_HEAD_ORDER)

    y_prompt = _out_proj(x_prompt, ng, w_gate, o_prompt[0], list(o_prompt[1:]), list(lses), wa, wb, wo,
                         tm=min(seq, PROJ_TILE_ROWS))
    heads = (A_KV_HEADS, B_HEADS, B_HEADS, B_HEADS)
    new_prompt = tuple(jnp.transpose(states[g].reshape(1, nb, 2, heads[g], HEAD_DIM, p_rows[g]), (0, 1, 5, 2, 3, 4))
                       for g in range(4))
    to_slabs = lambda o: jnp.transpose(o, (1, 0, 2, 3)).reshape(1, N_PAIRS, t_dec, LANES)
    y_sample = _out_proj(x_sample.reshape(1, t_dec, D_MODEL), ng, w_gate, to_slabs(oa_s), [to_slabs(ob_s)], [],
                         wa, wb, wo, tm=min(t_dec, PROJ_TILE_ROWS))
    y_sample = y_sample.reshape(n_dec, dec_seq, D_MODEL)
    new_sample = tuple(jnp.transpose(s.reshape(1, dec_seq, 2, heads[g], HEAD_DIM, n_dec), (0, 5, 1, 2, 3, 4))
                       for g, s in enumerate(_position_major(news, n_pos=dec_seq)))
    return (y_prompt, y_sample) + new_prompt + new_sample
```

```python
import functools
import math

import numpy as np
import jax
import jax.numpy as jnp
from jax import lax
from jax.experimental import pallas as pl
from jax.experimental.pallas import tpu as pltpu

D_MODEL = 1024
HEAD_DIM = 64
A_Q_HEADS = 8
A_KV_HEADS = 2
A_WINDOW = 128
B_GROUPS = ((128, 1), (512, 4), (2048, 16))
N_B_GROUPS = 3
B_HEADS = 8
N_KEYS = 128
A_WIDTH = A_Q_HEADS * HEAD_DIM
A_KV_WIDTH = A_KV_HEADS * HEAD_DIM
B_WIDTH = B_HEADS * HEAD_DIM
N_BUCKETS = 32
MAX_DISTANCE = 2048
EPS = 1e-6
NEG_INF = -1e30
Q_SCALE = HEAD_DIM ** -0.5

LANES = 128
MXU_COLS = 256
VMEM_LIMIT = 56 * 1024 * 1024
FUSED_VMEM_LIMIT = 61 * 1024 * 1024
PROJ_TILE_ROWS = 512
N_PAIRS = A_WIDTH // LANES

A_HEAD_ORDER = (0, 4, 1, 5, 2, 6, 3, 7)

Q_COLS = A_WIDTH + N_B_GROUPS * B_WIDTH
KV_COLS = A_KV_WIDTH + N_B_GROUPS * B_WIDTH
QKV_COLS = Q_COLS + 2 * KV_COLS
G_COLS = A_WIDTH + B_WIDTH + 2 * D_MODEL
KV_WIDTHS = (A_KV_WIDTH, B_WIDTH, B_WIDTH, B_WIDTH)
NORM_BATCH = 4 * B_WIDTH // MXU_COLS


def _t5_bucket_np(dist):
    max_exact = N_BUCKETS // 2
    d = np.maximum(dist, 0)
    df = np.maximum(d, 1).astype(np.float32)
    large = max_exact + (np.log(df / np.float32(max_exact)) / np.float32(math.log(MAX_DISTANCE / max_exact))
                         * np.float32(N_BUCKETS - max_exact)).astype(np.int32)
    large = np.minimum(large, N_BUCKETS - 1)
    return np.where(d < max_exact, d, large)


def _rmsnorm_bf16(x, gain):
    ms = jnp.mean(x * x, axis=-1, keepdims=True)
    return (x * lax.rsqrt(ms + EPS) * gain).astype(jnp.bfloat16)


def _state_plan(n_rows, seq, tm):
    tpb = seq // tm
    r = min(n_rows, tm)
    nblk = max(n_rows // tm, 1)
    return tpb, r, nblk, tpb - nblk


def _qkv_kernel(x_ref, ng_ref, w_ref, gq_ref, gk_ref, ones_ref, *rest, tm, seq, state_rows, dils, states_t):
    qkv_refs = rest[0:4]
    state_refs = rest[4:8]
    tmp_ref, a_scr, sq_scr, ss_scr = rest[8:12]
    h = _rmsnorm_bf16(x_ref[...], ng_ref[...])
    ones = ones_ref[...]

    def proj(col0, w):
        return jnp.dot(h, w_ref[:, col0:col0 + w], preferred_element_type=jnp.float32)

    def emit(a, c, out, d, state):
        w = a.shape[1]
        out_ref, base = out
        if d == 1:
            out_ref[0, 0, :, base + c:base + c + w] = a.astype(out_ref.dtype)
        else:
            for s in range(w // LANES):
                tmp_ref[c // LANES + s] = a[:, s * LANES:(s + 1) * LANES]
        if state is not None:
            sref, kv, r = state
            if states_t:
                sref[0, kv, c:c + w, :] = a[tm - r:, :].T
            else:
                wd = sref.shape[1] // 2
                sref[:, kv * wd + c:kv * wd + c + w] = a[tm - r:, :]

    def finish(out, d, width):
        out_ref, base = out
        if d > 1:
            for cls in range(d):
                for s in range(width // LANES):
                    out_ref[0, cls, :, base + s * LANES:base + (s + 1) * LANES] = (
                        tmp_ref[s, pl.ds(cls, tm // d, stride=d), :].astype(out_ref.dtype))

    def normed_batch(arrays):
        chunks = [(ai, c) for ai in range(len(arrays)) for c in range(0, B_WIDTH, MXU_COLS)]
        for ci, (ai, c) in enumerate(chunks):
            a = proj(arrays[ai][0] + c, MXU_COLS)
            a_scr[ci] = a
            sq_scr[ci * tm:(ci + 1) * tm, :] = (a * a).astype(jnp.bfloat16)
        n = len(chunks) * tm
        ss_scr[0:n, :] = jnp.dot(sq_scr[0:n, :], ones, preferred_element_type=jnp.float32)
        for ci, (ai, c) in enumerate(chunks):
            _, gain_ref, gcol0, out_ref, d, state = arrays[ai]
            o = (a_scr[ci] * lax.rsqrt(ss_scr[ci * tm:(ci + 1) * tm, :] + EPS)
                 * gain_ref[:, gcol0 + c:gcol0 + c + MXU_COLS])
            emit(o, c, out_ref, d, state)
            if c + MXU_COLS == B_WIDTH:
                finish(out_ref, d, B_WIDTH)

    q_off = (0, A_WIDTH, A_WIDTH + B_WIDTH, A_WIDTH + 2 * B_WIDTH)
    kv_off = (0, A_KV_WIDTH, A_KV_WIDTH + B_WIDTH, A_KV_WIDTH + 2 * B_WIDTH)
    plan = [_state_plan(state_rows[g], seq, tm)[1] for g in range(4)]
    k_out = [(qkv_refs[g], A_WIDTH) for g in range(4)]
    v_out = [(qkv_refs[g], A_WIDTH + KV_WIDTHS[g]) for g in range(4)]
    normed_batch([(q_off[g], gq_ref, q_off[g], (qkv_refs[g], 0), dils[g], None) for g in range(4)])
    normed_batch([(Q_COLS + kv_off[g], gk_ref, kv_off[g], k_out[g], dils[g], (state_refs[g], 0, plan[g]))
                  for g in range(1, 4)])
    a = proj(Q_COLS, A_KV_WIDTH)
    ss = jnp.dot((a * a).astype(jnp.bfloat16), ones[:A_KV_WIDTH, :A_KV_WIDTH], preferred_element_type=jnp.float32)
    emit(a * lax.rsqrt(ss + EPS) * gk_ref[:, 0:A_KV_WIDTH], 0, k_out[0], dils[0], (state_refs[0], 0, plan[0]))
    for g in range(4):
        wd = KV_WIDTHS[g]
        for c in range(0, wd, MXU_COLS):
            w = min(MXU_COLS, wd - c)
            emit(proj(Q_COLS + KV_COLS + kv_off[g] + c, w), c, v_out[g], dils[g], (state_refs[g], 1, plan[g]))
        finish(v_out[g], dils[g], wd)


def _qkv_proj(x2d, ng, w_qkv, gq, gk, ones, *, nb, seq, state_rows, dils, tm, states_t):
    t = x2d.shape[0]
    tpb = seq // tm

    def cls_shape(d, width):
        return jax.ShapeDtypeStruct((nb, d, seq // d, width), jnp.bfloat16)

    def cls_spec(d, width):
        return pl.BlockSpec((1, d, tm // d, width), lambda i: (i // tpb, 0, i % tpb, 0))

    def state_shape(g):
        if states_t:
            return jax.ShapeDtypeStruct((nb, 2, KV_WIDTHS[g], state_rows[g]), jnp.float32)
        return jax.ShapeDtypeStruct((nb * state_rows[g], 2 * KV_WIDTHS[g]), jnp.float32)

    def state_spec(g):
        _, r, nblk, j0 = _state_plan(state_rows[g], seq, tm)
        if states_t:
            return pl.BlockSpec((1, 2, KV_WIDTHS[g], r), lambda i: (i // tpb, 0, 0, jnp.maximum(i % tpb - j0, 0)))
        return pl.BlockSpec((r, 2 * KV_WIDTHS[g]),
                            lambda i: ((i // tpb) * nblk + jnp.maximum(i % tpb - j0, 0), 0))

    const = lambda i: (0, 0)
    out_shape = ([cls_shape(d, A_WIDTH + 2 * w) for d, w in zip(dils, KV_WIDTHS)]
                 + [state_shape(g) for g in range(4)])
    out_specs = ([cls_spec(d, A_WIDTH + 2 * w) for d, w in zip(dils, KV_WIDTHS)]
                 + [state_spec(g) for g in range(4)])
    res = pl.pallas_call(
        functools.partial(_qkv_kernel, tm=tm, seq=seq, state_rows=state_rows, dils=dils, states_t=states_t),
        grid=(t // tm,),
        in_specs=[
            pl.BlockSpec((tm, D_MODEL), lambda i: (i, 0)),
            pl.BlockSpec((1, D_MODEL), const),
            pl.BlockSpec((D_MODEL, QKV_COLS), const, pipeline_mode=pl.Buffered(1)),
            pl.BlockSpec((1, Q_COLS), const),
            pl.BlockSpec((1, KV_COLS), const),
            pl.BlockSpec((MXU_COLS, MXU_COLS), const),
        ],
        out_specs=out_specs,
        out_shape=out_shape,
        scratch_shapes=[pltpu.VMEM((N_PAIRS, tm, LANES), jnp.float32),
                        pltpu.VMEM((NORM_BATCH, tm, MXU_COLS), jnp.float32),
                        pltpu.VMEM((NORM_BATCH * tm, MXU_COLS), jnp.bfloat16),
                        pltpu.VMEM((NORM_BATCH * tm, MXU_COLS), jnp.float32)],
        compiler_params=pltpu.CompilerParams(
            dimension_semantics=("arbitrary",), vmem_limit_bytes=VMEM_LIMIT),
        name="qkv_proj",
    )(x2d, ng, w_qkv, gq, gk, ones)
    return res[0:4], res[4:8]


def _lookup_rows(bucket_map, top, table_ref, col_top, col_bottom, fill):
    acc = jnp.full(bucket_map.shape, fill, jnp.float32)
    for b in range(N_BUCKETS):
        val = jnp.where(top, table_ref[b, col_top], table_ref[b, col_bottom])
        acc = jnp.where(bucket_map == b, val, acc)
    return acc


def _band_block(qkv_ref, kp_ref, vp_ref, o_ref, o_stage, lse_ref, bias_ref, sink_ref, *,
                cls, first, dil, shared_kv):
    blk = N_KEYS
    row = lax.broadcasted_iota(jnp.int32, (2 * blk, blk), 0)
    lane = lax.broadcasted_iota(jnp.int32, (2 * blk, blk), 1)
    ahead = lane - jnp.bitwise_and(row, blk - 1)
    tri = ahead <= 0
    valid = ahead <= jnp.where(first, 0, blk)
    cur_part = tri.astype(jnp.bfloat16)
    prev_part = jnp.logical_not(tri).astype(jnp.bfloat16)
    left = lax.broadcasted_iota(jnp.int32, (blk, LANES), 1) < HEAD_DIM
    mask_l = left.astype(jnp.bfloat16)
    mask_r = jnp.logical_not(left).astype(jnp.bfloat16)
    ones_kv = jnp.ones((2 * blk, LANES), jnp.bfloat16)
    nt = (((1,), (1,)), ((), ()))
    rows = pl.ds(0, blk) if dil == 1 else pl.ds(cls, blk, stride=dil)
    wkv = kp_ref.shape[2]
    k0, v0 = A_WIDTH, A_WIDTH + wkv
    for p in range(N_PAIRS):
        kcol = 0 if shared_kv else p * LANES
        qp = qkv_ref[0, 0, :, p * LANES:(p + 1) * LANES]
        qs = jnp.concatenate([qp * mask_l, qp * mask_r], axis=0)
        kc = jnp.concatenate([kp_ref[cls, :, kcol:kcol + LANES],
                              qkv_ref[0, 0, :, k0 + kcol:k0 + kcol + LANES]], axis=0)
        vc = jnp.concatenate([vp_ref[cls, :, kcol:kcol + LANES],
                              qkv_ref[0, 0, :, v0 + kcol:v0 + kcol + LANES]], axis=0)
        s2 = lax.dot_general(qs, kc, nt, preferred_element_type=jnp.float32)
        s = jnp.where(tri, s2[:, blk:], s2[:, :blk]) + bias_ref[p]
        s = jnp.where(valid, s, NEG_INF)
        m = jnp.max(s, axis=-1, keepdims=True)
        if sink_ref is not None:
            sk = sink_ref[p]
            m = jnp.maximum(m, sk)
        pb = jnp.exp(s - m).astype(jnp.bfloat16)
        p2 = jnp.concatenate([pb * prev_part, pb * cur_part], axis=1)
        ov = jnp.dot(p2, jnp.concatenate([vc, ones_kv], axis=1), preferred_element_type=jnp.float32)
        l = ov[:, LANES:]
        if sink_ref is not None:
            l = l + jnp.exp(sk - m)
        num = jnp.where(left, ov[:blk, :LANES], ov[blk:, :LANES])
        den = jnp.where(left, l[:blk], l[blk:])
        if dil == 1:
            o_ref[0, p, rows, :] = (num / den).astype(o_ref.dtype)
        else:
            o_stage[p, rows, :] = num / den
        if lse_ref is not None:
            m1 = jnp.where(left, m[:blk], m[blk:])
            lse_ref[0, p, rows, :] = m1 + jnp.log(den)
    kp_ref[cls] = qkv_ref[0, 0, :, k0:k0 + wkv]
    vp_ref[cls] = qkv_ref[0, 0, :, v0:v0 + wkv]


def _decode_kernel(qa_ref, qb1_ref, qb2_ref, qb3_ref, na_ref, nb1_ref, nb2_ref, nb3_ref,
                   ca_ref, cb1_ref, cb2_ref, cb3_ref, ma_ref, mb1_ref, mb2_ref, mb3_ref, mn_ref, relb_ref, sinks_ref,
                   oa_ref, ob_ref, ta_ref, tb1_ref, tb2_ref, tb3_ref, tn_ref, sink_ref,
                   *, dec_seq, head_cols, sink_idx, dils, init):
    t = dec_seq
    if init:
        for g, (m_ref, t_ref) in enumerate(zip((ma_ref, mb1_ref, mb2_ref, mb3_ref),
                                               (ta_ref, tb1_ref, tb2_ref, tb3_ref))):
            top = lax.broadcasted_iota(jnp.int32, m_ref.shape, 0) < t
            top_n = lax.broadcasted_iota(jnp.int32, (2 * t, LANES), 0) < t
            for p in range(N_PAIRS):
                cols = (head_cols[g][2 * p], head_cols[g][2 * p + 1])
                rows = slice(p * 2 * t, (p + 1) * 2 * t)
                t_ref[rows, :] = _lookup_rows(m_ref[...], top, relb_ref, cols[0], cols[1], NEG_INF)
                tn_ref[g, rows, :] = _lookup_rows(mn_ref[g], top_n, relb_ref, cols[0], cols[1], NEG_INF)
                if g == 0:
                    sink_ref[rows, :] = jnp.where(top_n, sinks_ref[0, sink_idx[2 * p]],
                                                  sinks_ref[0, sink_idx[2 * p + 1]])
        return

    left8 = lax.broadcasted_iota(jnp.int32, (2 * t, LANES), 1) < HEAD_DIM
    top8 = lax.broadcasted_iota(jnp.int32, (2 * t, LANES), 0) < t
    own = (left8 == top8).astype(jnp.float32)
    left4 = lax.broadcasted_iota(jnp.int32, (t, LANES), 1) < HEAD_DIM
    nt = (((1,), (1,)), ((), ()))

    seq_id = pl.program_id(0)

    def seq_rows(block):
        k = block.shape[0] // t
        which = seq_id % k
        rows = block[0:t]
        for j in range(1, k):
            rows = jnp.where(which == j, block[j * t:(j + 1) * t], rows)
        return rows

    q_refs = (qa_ref, qb1_ref, qb2_ref, qb3_ref)

    def q_rows(g, p):
        qp = seq_rows(q_refs[g][:, p * LANES:(p + 1) * LANES].astype(jnp.float32))
        return jnp.concatenate([qp, qp], axis=0) * own

    def attend(qrs, kts, vts, knews, vnews, tbl, tbl_new, sink, dil):
        shared = len(kts) == 1
        rows8 = lambda a, u: jnp.broadcast_to(a[u:u + 1, :], (2 * t, LANES))
        qr = jnp.concatenate(qrs, axis=0)
        qb = qr.astype(jnp.bfloat16)
        if shared:
            s_c = jnp.dot(qb, kts[0], preferred_element_type=jnp.float32)
        else:
            s_c = jnp.concatenate([jnp.dot(qb[2 * t * p:2 * t * (p + 1)], kts[p], preferred_element_type=jnp.float32)
                                   for p in range(N_PAIRS)], axis=0)
        yield
        s_c = s_c + tbl
        kn = [jnp.concatenate([rows8(knews[0 if shared else p], u) for p in range(N_PAIRS)], axis=0)
              for u in range(t)]
        vn = [jnp.concatenate([rows8(vnews[0 if shared else p], u) for p in range(N_PAIRS)], axis=0)
              for u in range(t)]
        s_n = [jnp.sum(qr * kn[u], axis=-1, keepdims=True) + tbl_new[:, u:u + 1] for u in range(t)]
        m = functools.reduce(jnp.maximum, s_n, jnp.max(s_c, axis=-1, keepdims=True))
        if sink is not None:
            m = jnp.maximum(m, sink)
        pc = jnp.exp(s_c - m)
        pn = [jnp.exp(x - m) for x in s_n]
        l = functools.reduce(lambda a, b: a + b, pn, jnp.sum(pc, axis=-1, keepdims=True))
        if sink is not None:
            l = l + jnp.exp(sink - m)
        pb = pc.astype(jnp.bfloat16)
        yield
        if shared:
            o = lax.dot_general(pb, vts[0], nt, preferred_element_type=jnp.float32)
        elif dil >= t:
            lane8 = lax.broadcasted_iota(jnp.int32, (2 * t, LANES), 1)
            row8 = lax.broadcasted_iota(jnp.int32, (2 * t, LANES), 0)
            sel = (lane8 % dil == row8 % t).astype(jnp.bfloat16)
            parts = []
            for p in range(N_PAIRS):
                accs = []
                for side in range(2):
                    r0 = 2 * t * p + t * side
                    w = jnp.sum(pc[r0:r0 + t], axis=0, keepdims=True)
                    v = vts[p][side * HEAD_DIM:(side + 1) * HEAD_DIM]
                    accs.append(functools.reduce(
                        lambda a, b: a + b,
                        [v[:, c:c + LANES] * w[:, c:c + LANES] for c in range(0, v.shape[1], LANES)]))
                acc = jnp.concatenate(accs, axis=0)
                hi = acc.astype(jnp.bfloat16)
                lo = (acc - hi.astype(jnp.float32)).astype(jnp.bfloat16)
                parts.append(lax.dot_general(sel, hi, nt, preferred_element_type=jnp.float32)
                             + lax.dot_general(sel, lo, nt, preferred_element_type=jnp.float32))
            o = jnp.concatenate(parts, axis=0)
        else:
            o = jnp.concatenate([lax.dot_general(pb[2 * t * p:2 * t * (p + 1)], vts[p], nt,
                                                 preferred_element_type=jnp.float32)
                                 for p in range(N_PAIRS)], axis=0)
        for u in range(t):
            o = o + pn[u] * vn[u]
        yield o / l, m + jnp.log(l)

    def fold(x, p):
        return jnp.where(left4, x[2 * t * p:2 * t * p + t], x[2 * t * p + t:2 * t * (p + 1)])

    def cache_kv(c_ref, kv, p, dtype=jnp.bfloat16):
        if isinstance(c_ref, tuple):
            c_ref, kv = c_ref[kv], 0
        return c_ref[0, kv, 2 * p:2 * p + 2].reshape(LANES, c_ref.shape[-1]).astype(dtype)

    caches = (cb1_ref, cb2_ref, cb3_ref)
    news = (nb1_ref, nb2_ref, nb3_ref)
    tbls = (tb1_ref, tb2_ref, tb3_ref)
    new_a = seq_rows(na_ref[...])
    groups = [attend([q_rows(0, p) for p in range(N_PAIRS)],
                     [cache_kv(ca_ref, 0, 0)], [cache_kv(ca_ref, 1, 0)],
                     [new_a[:, 0:LANES]], [new_a[:, LANES:2 * LANES]],
                     ta_ref[...], tn_ref[0], sink_ref[:, 0:1], dils[0])]
    for g in range(N_B_GROUPS):
        new_g = seq_rows(news[g][...])
        groups.append(attend(
            [q_rows(1 + g, p) for p in range(N_PAIRS)],
            [cache_kv(caches[g], 0, p) for p in range(N_PAIRS)],
            [cache_kv(caches[g], 1, p, jnp.float32 if dils[1 + g] >= t else jnp.bfloat16) for p in range(N_PAIRS)],
            [new_g[:, p * LANES:(p + 1) * LANES] for p in range(N_PAIRS)],
            [new_g[:, B_WIDTH + p * LANES:B_WIDTH + (p + 1) * LANES] for p in range(N_PAIRS)],
            tbls[g][...], tn_ref[1 + g], None, dils[1 + g]))
    for phase in range(2):
        for gen in groups:
            next(gen)
    results = [next(gen) for gen in groups]
    oa = results[0][0]
    for p in range(N_PAIRS):
        oa_ref[0, p] = fold(oa, p).astype(oa_ref.dtype)
    outs = [o for o, _ in results[1:]]
    lses = [jnp.broadcast_to(lse, o.shape) for o, lse in results[1:]]
    mx = jnp.maximum(jnp.maximum(lses[0], lses[1]), lses[2])
    es = [jnp.exp(x - mx) for x in lses]
    den = es[0] + es[1] + es[2]
    comb = (es[0] * outs[0] + es[1] * outs[1] + es[2] * outs[2]) / den
    for p in range(N_PAIRS):
        ob_ref[0, p] = fold(comb, p).astype(ob_ref.dtype)


def _decode_bucket_maps(dils, cache_lens, dec_seq):
    t_idx = np.arange(dec_seq)
    maps = []
    new = np.full((len(dils), 2 * dec_seq, LANES), -1, np.int32)
    for g, (d, ln) in enumerate(zip(dils, cache_lens)):
        back = ln + t_idx[:, None] - np.arange(ln)[None, :]
        ok = (back % d == 0) & (back // d >= 1) & (back // d < N_KEYS)
        maps.append(jnp.asarray(np.tile(np.where(ok, _t5_bucket_np(back), -1), (2, 1)).astype(np.int32)))
        backn = t_idx[:, None] - t_idx[None, :]
        okn = (backn >= 0) & (backn % d == 0) & (backn // d < N_KEYS)
        new[g, :, :dec_seq] = np.tile(np.where(okn, _t5_bucket_np(backn), -1), (2, 1))
    return maps, jnp.asarray(new)


def _decode_band_kernel(*refs, dec_seq, head_cols, sink_idx, dils, blocks_per_batch):
    n_g = len(dils)
    n_dec_in = 4 * n_g + 3
    dec_in = refs[0:n_dec_in]
    relb_ref, sinks_ref = dec_in[-2:]
    band_in = [refs[n_dec_in + 2 * g:n_dec_in + 2 * (g + 1)] for g in range(n_g)]
    outs = refs[n_dec_in + 2 * n_g:n_dec_in + 2 * n_g + 2 + 2 * n_g - 1]
    oa_ref, ob_ref = outs[0:2]
    o_refs = outs[2:2 + n_g]
    lse_refs = (None,) + tuple(outs[2 + n_g:])
    scratch = refs[n_dec_in + 2 * n_g + 2 + 2 * n_g - 1:-4]
    kring, vring, ksem, vsem = refs[-4:]
    dec_scratch = scratch[0:6]
    bias_refs = scratch[6:6 + n_g]
    sink_tile = scratch[6 + n_g]
    kprev_refs = scratch[7 + n_g:7 + 2 * n_g]
    vprev_refs = scratch[7 + 2 * n_g:7 + 3 * n_g]
    stages = iter(scratch[7 + 3 * n_g:])
    stage_refs = [None if d == 1 else next(stages) for d in dils]
    blk = N_KEYS
    s = pl.program_id(0)
    n_steps = pl.num_programs(0)
    big = 3 * n_g - 1
    big_hbm = dec_in[big]
    k_slots, v_slots = kring.shape[0], vring.shape[0]

    def k_copy(i):
        return pltpu.make_async_copy(big_hbm.at[pl.ds(i, 1), pl.ds(0, 1)], kring.at[i % k_slots],
                                     ksem.at[i % k_slots])

    def v_copy(i):
        return pltpu.make_async_copy(big_hbm.at[pl.ds(i, 1), pl.ds(1, 1)], vring.at[i % v_slots],
                                     vsem.at[i % v_slots])

    dec_in = list(dec_in)
    dec_in[big] = (kring.at[s % k_slots], vring.at[s % v_slots])
    decode = functools.partial(_decode_kernel, *dec_in, oa_ref, ob_ref, *dec_scratch, dec_seq=dec_seq,
                               head_cols=head_cols, sink_idx=sink_idx, dils=dils)

    @pl.when(s == 0)
    def _():
        k_copy(0).start()
        k_copy(1).start()
        v_copy(0).start()
        top = lax.broadcasted_iota(jnp.int32, (2 * blk, blk), 0) < blk
        for g in range(n_g):
            bmap = band_in[g][1][...]
            for p in range(N_PAIRS):
                bias_refs[g][p] = _lookup_rows(bmap, top, relb_ref, head_cols[g][2 * p], head_cols[g][2 * p + 1],
                                               0.0)
            kprev_refs[g][...] = jnp.zeros(kprev_refs[g].shape, kprev_refs[g].dtype)
            vprev_refs[g][...] = jnp.zeros(vprev_refs[g].shape, vprev_refs[g].dtype)
        for p in range(N_PAIRS):
            sink_tile[p] = jnp.where(top, sinks_ref[0, sink_idx[2 * p]], sinks_ref[0, sink_idx[2 * p + 1]])
        decode(init=True)

    r = s % blocks_per_batch

    @pl.when(s + 1 < n_steps)
    def _():
        v_copy(s + 1).start()

    k_copy(s).wait()
    v_copy(s).wait()
    decode(init=False)
    for g in range(n_g):
        _band_block(band_in[g][0], kprev_refs[g], vprev_refs[g], o_refs[g], stage_refs[g], lse_refs[g],
                    bias_refs[g], sink_tile if g == 0 else None, cls=r % dils[g], first=(r // dils[g]) == 0,
                    dil=dils[g], shared_kv=(g == 0))

    @pl.when(s + 2 < n_steps)
    def _():
        k_copy(s + 2).start()

    for g in range(n_g):
        if dils[g] > 1:
            @pl.when(r % dils[g] == dils[g] - 1)
            def _():
                for p in range(N_PAIRS):
                    o_refs[g][0, p] = stage_refs[g][p].astype(o_refs[g].dtype)


def _decode_band_attention(q_dec, news, caches, qkvs, rel_bias, sinks, *, dec_seq, dils, head_cols, sink_idx):
    t = dec_seq
    n = q_dec[0].shape[0] // t
    nb = qkvs[0].shape[0]
    seq = qkvs[0].shape[1] * qkvs[0].shape[2]
    ppb = seq // N_KEYS
    assert n == nb * ppb, "one prompt block of every group per sample sequence"
    seq3 = lambda i: (i, 0, 0)
    seq4 = lambda i: (i, 0, 0, 0)
    seq5 = lambda i: (i, 0, 0, 0, 0)
    cache_lens = tuple(c.shape[-1] for c in caches)
    maps, map_new = _decode_bucket_maps(dils, cache_lens, t)
    smem = functools.partial(pl.BlockSpec, memory_space=pltpu.SMEM)
    q_rows, new_rows = 16, 8
    in_specs = [pl.BlockSpec((q_rows, A_WIDTH), lambda i: (i // (q_rows // t), 0)) for _ in q_dec]
    in_specs += [pl.BlockSpec((new_rows, a.shape[1]), lambda i: (i // (new_rows // t), 0)) for a in news]
    assert n >= 2, "the hand-streamed cache prefetches two sequences ahead"
    in_specs += [pl.BlockSpec((1,) + c.shape[1:], seq5) for c in caches[:-1]]
    in_specs += [pl.BlockSpec(memory_space=pl.ANY)]
    half = (1, 1) + caches[-1].shape[2:]
    in_specs += [pl.BlockSpec(m.shape, lambda i: (0, 0)) for m in maps]
    in_specs += [pl.BlockSpec(map_new.shape, lambda i: (0, 0, 0)), smem(), smem()]
    band_args = []
    out_specs = [pl.BlockSpec((1, N_PAIRS, t, LANES), seq4)] * 2
    out_shape = [jax.ShapeDtypeStruct((n, N_PAIRS, t, LANES), jnp.float32)] * 2
    qi = np.arange(N_KEYS)[:, None]
    ci = np.arange(N_KEYS)[None, :]
    for g, d in enumerate(dils):
        cur = lambda s, d=d: (s // ppb, (s % ppb) % d, (s % ppb) // d, 0)
        bmap = jnp.asarray(np.tile(_t5_bucket_np(((qi - ci) % N_KEYS) * d), (2, 1)).astype(np.int32))
        in_specs += [pl.BlockSpec((1, 1, N_KEYS, qkvs[g].shape[3]), cur), pl.BlockSpec(bmap.shape, lambda s: (0, 0))]
        band_args += [qkvs[g], bmap]
    tile = lambda d: pl.BlockSpec((1, N_PAIRS, N_KEYS * d, LANES), lambda s, d=d: (s // ppb, 0, (s % ppb) // d, 0))
    slab_shape = lambda dtype: jax.ShapeDtypeStruct((nb, N_PAIRS, seq, LANES), dtype)
    out_specs += [tile(d) for d in dils] + [tile(d) for d in dils[1:]]
    out_shape += [slab_shape(jnp.bfloat16)] * len(dils) + [slab_shape(jnp.float32)] * (len(dils) - 1)
    n_rows = 2 * t * N_PAIRS
    tile_scr = pltpu.VMEM((N_PAIRS, 2 * N_KEYS, N_KEYS), jnp.float32)
    res = pl.pallas_call(
        functools.partial(_decode_band_kernel, dec_seq=t, head_cols=head_cols, sink_idx=sink_idx, dils=dils,
                          blocks_per_batch=ppb),
        grid=(n,),
        in_specs=in_specs,
        out_specs=out_specs,
        out_shape=out_shape,
        scratch_shapes=[pltpu.VMEM((n_rows, ln), jnp.float32) for ln in cache_lens]
        + [pltpu.VMEM((len(dils), n_rows, LANES), jnp.float32), pltpu.VMEM((n_rows, LANES), jnp.float32)]
        + [tile_scr] * (len(dils) + 1)
        + [pltpu.VMEM((d, N_KEYS, w), jnp.bfloat16) for d, w in zip(dils, KV_WIDTHS)] * 2
        + [pltpu.VMEM((N_PAIRS, N_KEYS * d, LANES), jnp.float32) for d in dils if d > 1]
        + [pltpu.VMEM((3,) + half, jnp.float32), pltpu.VMEM((2,) + half, jnp.float32),
           pltpu.SemaphoreType.DMA((3,)), pltpu.SemaphoreType.DMA((2,))],
        compiler_params=pltpu.CompilerParams(
            dimension_semantics=("arbitrary",), vmem_limit_bytes=FUSED_VMEM_LIMIT),
        name="decode_band_attn",
    )(*q_dec, *news, *caches, *maps, map_new, rel_bias, sinks, *band_args)
    ng = len(dils)
    return res[0], res[1], res[2:2 + ng], res[2 + ng:]


def _out_kernel(*refs, n_groups):
    x_ref, ng_ref, wg_ref, oa_ref = refs[0:4]
    ob_refs = refs[4:4 + n_groups]
    lse_refs = refs[4 + n_groups:4 + 2 * n_groups] if n_groups > 1 else ()
    wa_ref, wb_ref, wo_ref, y_ref = refs[-4:]
    x = x_ref[0]
    h = _rmsnorm_bf16(x, ng_ref[...])

    def gate(c0, width, silu):
        a = jnp.dot(h, wg_ref[:, c0:c0 + width], preferred_element_type=jnp.float32)
        sg = 0.5 * jnp.tanh(0.5 * a) + 0.5
        return a * sg if silu else sg

    def slabs(ref):
        return jnp.concatenate([ref[0, p].astype(jnp.float32) for p in range(N_PAIRS)], axis=1)

    m0 = A_WIDTH + B_WIDTH
    ga = gate(0, A_WIDTH, True)
    gb = gate(A_WIDTH, B_WIDTH, True)
    ma = gate(m0, D_MODEL, False)
    mb = gate(m0 + D_MODEL, D_MODEL, False)
    if n_groups > 1:
        parts = []
        for p in range(N_PAIRS):
            lses = [r[0, p] for r in lse_refs]
            mx = functools.reduce(jnp.maximum, lses)
            es = [jnp.exp(v - mx) for v in lses]
            den = functools.reduce(lambda a, b: a + b, es)
            num = functools.reduce(lambda a, b: a + b,
                                   [e * r[0, p].astype(jnp.float32) for e, r in zip(es, ob_refs)])
            parts.append(num / den)
        ob = jnp.concatenate(parts, axis=1)
    else:
        ob = slabs(ob_refs[0])
    oa = slabs(oa_ref)
    ya = jnp.dot((oa * ga).astype(jnp.bfloat16), wa_ref[...], preferred_element_type=jnp.float32)
    yb = jnp.dot((ob * gb).astype(jnp.bfloat16), wb_ref[...], preferred_element_type=jnp.float32)
    merged = (ma * ya + mb * yb).astype(jnp.bfloat16)
    y_ref[0] = x + jnp.dot(merged, wo_ref[...], preferred_element_type=jnp.float32)


def _out_proj(x3d, ng, wg, oa, obs, lses, wa, wb, wo, *, tm):
    nb, seq, _ = x3d.shape
    row = lambda b, i: (b, i, 0)
    slab = pl.BlockSpec((1, N_PAIRS, tm, LANES), lambda b, i: (b, 0, i, 0))
    const = lambda b, i: (0, 0)
    once = dict(pipeline_mode=pl.Buffered(1))
    in_specs = [pl.BlockSpec((1, tm, D_MODEL), row), pl.BlockSpec((1, D_MODEL), const),
                pl.BlockSpec(wg.shape, const, **once), slab]
    in_specs += [slab for _ in obs] + [slab for _ in lses]
    in_specs += [pl.BlockSpec(wa.shape, const, **once), pl.BlockSpec(wb.shape, const, **once),
                 pl.BlockSpec(wo.shape, const, **once)]
    return pl.pallas_call(
        functools.partial(_out_kernel, n_groups=len(obs)),
        grid=(nb, seq // tm),
        in_specs=in_specs,
        out_specs=pl.BlockSpec((1, tm, D_MODEL), row),
        out_shape=jax.ShapeDtypeStruct((nb, seq, D_MODEL), jnp.float32),
        compiler_params=pltpu.CompilerParams(
            dimension_semantics=("arbitrary", "arbitrary"), vmem_limit_bytes=VMEM_LIMIT),
        name="out_proj",
    )(x3d, ng, wg, oa, *obs, *lses, wa, wb, wo)


def _position_major_kernel(*refs, n_pos):
    n_g = (len(refs) - 1) // 2
    tmp_ref = refs[-1]
    for x_ref, o_ref in zip(refs[:n_g], refs[n_g:2 * n_g]):
        rows, w = x_ref.shape
        for s in range(w // LANES):
            tmp_ref[s] = x_ref[:, s * LANES:(s + 1) * LANES]
        for tt in range(n_pos):
            for s in range(w // LANES):
                o_ref[tt, s * LANES:(s + 1) * LANES, :] = tmp_ref[s, pl.ds(tt, rows // n_pos, stride=n_pos), :].T


def _position_major(news, *, n_pos):
    rows = news[0].shape[0]
    n_seq = rows // n_pos
    blk_w = N_PAIRS * LANES
    in_specs, out_specs, out_shape = [], [], []
    for a in news:
        w = min(a.shape[1], blk_w)
        last = a.shape[1] // w - 1
        in_specs.append(pl.BlockSpec((rows, w), lambda i, last=last: (0, jnp.minimum(i, last))))
        out_specs.append(pl.BlockSpec((n_pos, w, n_seq), lambda i, last=last: (0, jnp.minimum(i, last), 0)))
        out_shape.append(jax.ShapeDtypeStruct((n_pos, a.shape[1], n_seq), jnp.float32))
    return pl.pallas_call(
        functools.partial(_position_major_kernel, n_pos=n_pos),
        grid=(max(a.shape[1] for a in news) // blk_w,),
        in_specs=in_specs,
        out_specs=out_specs,
        out_shape=out_shape,
        scratch_shapes=[pltpu.VMEM((N_PAIRS, rows, LANES), jnp.float32)],
        compiler_params=pltpu.CompilerParams(
            dimension_semantics=("arbitrary",), vmem_limit_bytes=VMEM_LIMIT),
        name="sample_states",
    )(*news)


def _w_in_kernel(w_ref, qkv_ref, gate_ref):
    offs = np.cumsum((0, A_WIDTH, A_KV_WIDTH, A_KV_WIDTH, A_WIDTH, 3 * B_WIDTH, 3 * B_WIDTH, 3 * B_WIDTH,
                      B_WIDTH, D_MODEL, D_MODEL))

    def copy(out_ref, dst, seg, perm):
        src, width = int(offs[seg]), int(offs[seg + 1] - offs[seg])
        if perm:
            for j, h in enumerate(A_HEAD_ORDER):
                out_ref[:, dst + j * HEAD_DIM:dst + (j + 1) * HEAD_DIM] = (
                    w_ref[0, :, src + h * HEAD_DIM:src + (h + 1) * HEAD_DIM].astype(out_ref.dtype))
        else:
            out_ref[:, dst:dst + width] = w_ref[0, :, src:src + width].astype(out_ref.dtype)
        return dst + width

    dst = 0
    for seg, perm in ((0, True), (4, False), (1, False), (5, False), (2, False), (6, False)):
        dst = copy(qkv_ref, dst, seg, perm)
    dst = 0
    for seg, perm in ((3, True), (7, False), (8, False), (9, False)):
        dst = copy(gate_ref, dst, seg, perm)


def _prep_w_in(w_in):
    rows = LANES
    return pl.pallas_call(
        _w_in_kernel,
        grid=(D_MODEL // rows,),
        in_specs=[pl.BlockSpec((1, rows, w_in.shape[2]), lambda i: (0, i, 0))],
        out_specs=(pl.BlockSpec((rows, QKV_COLS), lambda i: (i, 0)), pl.BlockSpec((rows, G_COLS), lambda i: (i, 0))),
        out_shape=(jax.ShapeDtypeStruct((D_MODEL, QKV_COLS), jnp.bfloat16),
                   jax.ShapeDtypeStruct((D_MODEL, G_COLS), jnp.bfloat16)),
        compiler_params=pltpu.CompilerParams(dimension_semantics=("arbitrary",), vmem_limit_bytes=VMEM_LIMIT),
        name="prep_w_in",
    )(w_in)


def _prep_params(w_in, q_gain_a, k_gain_a, q_gain_b, k_gain_b, w_up_a, w_up_b):
    w_qkv, w_gate = _prep_w_in(w_in)
    gq = jnp.broadcast_to(jnp.concatenate([q_gain_a[None], q_gain_b], axis=0)[:, None, :] * Q_SCALE,
                          (1 + N_B_GROUPS, A_Q_HEADS, HEAD_DIM)).reshape(1, Q_COLS)
    gk = jnp.concatenate([jnp.broadcast_to(k_gain_a, (A_KV_HEADS, HEAD_DIM)).reshape(A_KV_WIDTH),
                          jnp.broadcast_to(k_gain_b[:, None, :], (N_B_GROUPS, B_HEADS, HEAD_DIM)).reshape(-1)])[None]
    hd = np.arange(MXU_COLS) // HEAD_DIM
    ones = jnp.asarray((hd[:, None] == hd[None, :]).astype(np.float32) / HEAD_DIM, jnp.bfloat16)
    wa = jnp.concatenate([w_up_a[h * HEAD_DIM:(h + 1) * HEAD_DIM] for h in A_HEAD_ORDER], axis=0)
    return w_qkv, w_gate, gq, gk, ones, wa.astype(jnp.bfloat16), w_up_b.astype(jnp.bfloat16)


def kernel(x_prompt, x_sample, cache_a_kv, cache_b1_kv, cache_b2_kv, cache_b3_kv, rel_bias, norm_gain, w_in,
           q_gain_a, k_gain_a, sinks_a, q_gain_b, k_gain_b, w_up_a, w_up_b, w_out):
    assert norm_gain.shape[0] == 1, "single layer"
    nb, seq, _ = x_prompt.shape
    n_dec, dec_seq, _ = x_sample.shape
    w_qkv, w_gate, gq, gk, ones, wa, wb = _prep_params(
        w_in, q_gain_a[0], k_gain_a[0], q_gain_b[0], k_gain_b[0], w_up_a[0], w_up_b[0])
    wo = w_out[0].astype(jnp.bfloat16)
    ng = norm_gain
    windows = (A_WINDOW,) + tuple(w for w, _ in B_GROUPS)
    dils = (1,) + tuple(d for _, d in B_GROUPS)
    head_cols = (A_HEAD_ORDER,) + tuple(tuple(range(A_Q_HEADS + g * B_HEADS, A_Q_HEADS + (g + 1) * B_HEADS))
                                        for g in range(N_B_GROUPS))

    p_rows = tuple(min(w, seq) for w in windows)
    qkvs, states = _qkv_proj(x_prompt.reshape(nb * seq, D_MODEL), ng, w_qkv, gq, gk, ones,
                             nb=nb, seq=seq, state_rows=p_rows, dils=dils, tm=min(seq, PROJ_TILE_ROWS),
                             states_t=True)
    t_dec = n_dec * dec_seq
    qd, news = _qkv_proj(x_sample.reshape(t_dec, D_MODEL), ng, w_qkv, gq, gk, ones,
                         nb=1, seq=t_dec, state_rows=(t_dec,) * 4, dils=(1, 1, 1, 1),
                         tm=min(t_dec, PROJ_TILE_ROWS), states_t=False)

    q_dec = [a.reshape(t_dec, a.shape[3]) for a in qd]
    caches = [jnp.transpose(c[0], (0, 2, 3, 4, 1)) for c in (cache_a_kv, cache_b1_kv, cache_b2_kv, cache_b3_kv)]
    oa_s, ob_s, o_prompt, lses = _decode_band_attention(
        q_dec, news, caches, qkvs, rel_bias, sinks_a, dec_seq=dec_seq, dils=dils, head_cols=head_cols,
        sink_idx=A_HEAD_ORDER)

    y_prompt = _out_proj(x_prompt, ng, w_gate, o_prompt[0], list(o_prompt[1:]), list(lses), wa, wb, wo,
                         tm=min(seq, PROJ_TILE_ROWS))
    heads = (A_KV_HEADS, B_HEADS, B_HEADS, B_HEADS)
    new_prompt = tuple(jnp.transpose(states[g].reshape(1, nb, 2, heads[g], HEAD_DIM, p_rows[g]), (0, 1, 5, 2, 3, 4))
                       for g in range(4))
    to_slabs = lambda o: jnp.transpose(o, (1, 0, 2, 3)).reshape(1, N_PAIRS, t_dec, LANES)
    y_sample = _out_proj(x_sample.reshape(1, t_dec, D_MODEL), ng, w_gate, to_slabs(oa_s), [to_slabs(ob_s)], [],
                         wa, wb, wo, tm=min(t_dec, PROJ_TILE_ROWS))
    y_sample = y_sample.reshape(n_dec, dec_seq, D_MODEL)
    new_sample = tuple(jnp.transpose(s.reshape(1, dec_seq, 2, heads[g], HEAD_DIM, n_dec), (0, 5, 1, 2, 3, 4))
                       for g, s in enumerate(_position_major(news, n_pos=dec_seq)))
    return (y_prompt, y_sample) + new_prompt + new_sample
```
